```python
import math
import jax, jax.numpy as jnp
from jax import lax
import numpy as np

D_MODEL = 1024
BATCH = 8
SEQ = 2048
DEPTH = 2
DEC_BATCH = 128
DEC_SEQ = 1
PAST_LEN = 16384
PAGE_SIZE = 128

HEAD_A = 64
N_HEADS_A = 8
D_A = N_HEADS_A * HEAD_A
D_DECAY_LORA = 64
D_ICLR_LORA = 64
D_VRES_LORA = 32
GN_EPS = 64e-5
CHUNK = 128
N_GROUPS_B = 4
D_B = D_MODEL // 2
GROUP_B = D_B // N_GROUPS_B
D_PLE = 256
EPS = 1e-6
N_SHIFT = 3 * D_A + D_DECAY_LORA + D_ICLR_LORA
D_IN = N_SHIFT + D_A + 2 * D_B + D_B + 2 * D_MODEL

kernel_name = "rwkv7_gmlp_gated_hybrid_step"


def _rms_norm(x, g):
    xf = x.astype(jnp.float32)
    y = xf * lax.rsqrt(jnp.mean(xf * xf, axis=-1, keepdims=True) + EPS)
    return (y * g.astype(jnp.float32)).astype(x.dtype)


def _layer_norm(x, g, b):
    xf = x.astype(jnp.float32)
    mu = jnp.mean(xf, axis=-1, keepdims=True)
    var = jnp.mean((xf - mu) ** 2, axis=-1, keepdims=True)
    y = (xf - mu) * lax.rsqrt(var + 1e-5)
    return (y * g.astype(jnp.float32) + b.astype(jnp.float32)).astype(x.dtype)


def _wkv_step(S, inp):
    r, w, k, v, aa, bb = inp
    sa = jnp.einsum('bhij,bhj->bhi', S, aa)
    S = S * w[:, :, None, :] + sa[..., :, None] * bb[..., None, :] + v[..., :, None] * k[..., None, :]
    y = jnp.einsum('bhij,bhj->bhi', S, r)
    return S, y


def _rwkv7(z, shift_prev, xn, v_first, wkv0, mu, w0, w_up, a0, a_up, k_k, k_a, r_k, gn_g, gn_b, vres):
    B, T, _ = z.shape
    z_prev = jnp.concatenate([shift_prev[:, None, :].astype(z.dtype), z[:, :-1]], axis=1)
    zs = z + (z_prev - z) * mu
    r = zs[..., :D_A]
    k = zs[..., D_A:2 * D_A]
    v = zs[..., 2 * D_A:3 * D_A]
    wd = zs[..., 3 * D_A:3 * D_A + D_DECAY_LORA]
    ad = zs[..., 3 * D_A + D_DECAY_LORA:]
    w = -jax.nn.softplus(-(w0 + jnp.tanh(wd) @ w_up)) - 0.5
    decay = jnp.exp(-jnp.exp(w.astype(jnp.float32)))
    a = jax.nn.sigmoid(a0 + ad @ a_up)
    if vres is not None:
        vd, vu, vb = vres
        v = v + (v_first - v) * jax.nn.sigmoid(vb + (xn @ vd) @ vu)
    heads = lambda t: t.reshape(B, T, N_HEADS_A, HEAD_A).astype(jnp.float32)
    kk = heads(k * k_k)
    kk = kk / jnp.maximum(jnp.sqrt(jnp.sum(kk * kk, axis=-1, keepdims=True)), 1e-12)
    k = k * (1.0 + (a - 1.0) * k_a)
    rh, kh, vh, ah = heads(r), heads(k), heads(v), heads(a)
    wh = decay.reshape(B, T, N_HEADS_A, HEAD_A)
    aa = -kk
    bb = kk * ah
    tm = lambda t: jnp.moveaxis(t, 1, 0)
    S, y = lax.scan(_wkv_step, wkv0.astype(jnp.float32),
                    (tm(rh), tm(wh), tm(kh), tm(vh), tm(aa), tm(bb)))
    y = jnp.moveaxis(y, 0, 1)
    m = jnp.mean(y, axis=-1, keepdims=True)
    var = jnp.mean((y - m) ** 2, axis=-1, keepdims=True)
    y = ((y - m) * lax.rsqrt(var + GN_EPS)).reshape(B, T, D_A)
    y = y * gn_g.astype(jnp.float32) + gn_b.astype(jnp.float32)
    bonus = jnp.sum(rh * kh * r_k.astype(jnp.float32), axis=-1, keepdims=True) * vh
    y = y + bonus.reshape(B, T, D_A)
    return y.astype(z.dtype), S, z[:, -1], v


def _chunk_spatial(v_n, w_s, b_s):
    B, T, _ = v_n.shape
    n_chunks = -(-T // CHUNK)
    pad = n_chunks * CHUNK - T
    vp = jnp.pad(v_n, ((0, 0), (0, pad), (0, 0))).reshape(B, n_chunks, CHUNK, N_GROUPS_B, GROUP_B)
    mask = jnp.tril(jnp.ones((CHUNK, CHUNK), dtype=bool))
    wm = jnp.where(mask[None], w_s, jnp.zeros_like(w_s))
    mixed = jnp.einsum('gts,bnsgc->bntgc', wm, vp) + jnp.swapaxes(b_s, 0, 1)[None, None, :, :, None]
    return mixed.reshape(B, n_chunks * CHUNK, D_B)[:, :T]


def _forward(x, p, wkv0, shift0, W):
    h = x
    v_first = None
    wkvs, shifts, chunk_vs = [], [], []
    for l in range(DEPTH):
        xn = _rms_norm(h, W['norm_g'][l])
        zall = jnp.einsum('btd,de->bte', xn, W['w_in'][l])
        o = N_SHIFT
        z_shift = zall[..., :o]
        g_a = zall[..., o:o + D_A]; o += D_A
        u = jax.nn.gelu(zall[..., o:o + D_B]); o += D_B
        vb_ = jax.nn.gelu(zall[..., o:o + D_B]); o += D_B
        g_b = zall[..., o:o + D_B]; o += D_B
        m_a = zall[..., o:o + D_MODEL]; o += D_MODEL
        m_b = zall[..., o:o + D_MODEL]
        vres = None if l == 0 else (W['vres_down'][l - 1], W['vres_up'][l - 1], W['vres_b'][l - 1])
        y_a, S, last, v = _rwkv7(z_shift, shift0[l], xn, v_first, wkv0[l], W['shift_mu'][l],
                                 W['w0'][l], W['w_up'][l], W['a0'][l], W['a_up'][l],
                                 W['k_k'][l], W['k_a'][l], W['r_k'][l], W['gn_g'][l], W['gn_b'][l], vres)
        if l == 0:
            v_first = v
        wkvs.append(S.astype(wkv0.dtype))
        shifts.append(last.astype(shift0.dtype))
        v_n = _layer_norm(vb_, W['ln_v_g'][l], W['ln_v_b'][l])
        chunk_vs.append(v_n)
        y_b = u * _chunk_spatial(v_n, W['w_spatial'][l], W['b_spatial'][l])
        br_a = (y_a * jax.nn.silu(g_a)) @ W['w_br_a'][l]
        br_b = (y_b * jax.nn.silu(g_b)) @ W['w_br_b'][l]
        merged = jax.nn.sigmoid(m_a) * br_a + jax.nn.sigmoid(m_b) * br_b
        h = h + merged @ W['w_out'][l]
        ple = p[l] @ W['w_ple'][l]
        h = h + jax.nn.sigmoid(h @ W['w_ple_gate'][l] + W['b_ple_gate'][l]) * ple
    y = _rms_norm(h, W['final_g'])
    return y, jnp.stack(wkvs), jnp.stack(shifts), jnp.stack(chunk_vs)


def setup_inputs(seed: int = 0) -> dict:
    key = jax.random.key(seed)
    ks = iter(jax.random.split(key, 64))
    f32 = jnp.float32
    nrm = lambda shape, s: jax.random.normal(next(ks), shape, f32) * s
    L = DEPTH
    return {
        'x_prompt': nrm((BATCH, SEQ, D_MODEL), 1.0),
        'x_sample': nrm((DEC_BATCH, DEC_SEQ, D_MODEL), 1.0),
        'state_rwkv_wkv': nrm((L, DEC_BATCH, N_HEADS_A, HEAD_A, HEAD_A), 0.5),
        'state_rwkv_shift': nrm((L, DEC_BATCH, N_SHIFT), 1.0),
        'p_prompt': nrm((L, BATCH, SEQ, D_PLE), 1.0),
        'p_sample': nrm((L, DEC_BATCH, DEC_SEQ, D_PLE), 1.0),
        'norm_g': 1.0 + nrm((L, D_MODEL), 0.05),
        'w_in': nrm((L, D_MODEL, D_IN), D_MODEL ** -0.5),
        'shift_mu': jax.random.uniform(next(ks), (L, N_SHIFT), f32),
        'w0': nrm((L, D_A), 0.5),
        'w_up': nrm((L, D_DECAY_LORA, D_A), 0.5 * D_DECAY_LORA ** -0.5),
        'a0': nrm((L, D_A), 0.5),
        'a_up': nrm((L, D_ICLR_LORA, D_A), 0.5 * D_ICLR_LORA ** -0.5),
        'vres_down': nrm((L - 1, D_MODEL, D_VRES_LORA), D_MODEL ** -0.5),
        'vres_up': nrm((L - 1, D_VRES_LORA, D_A), 0.5 * D_VRES_LORA ** -0.5),
        'vres_b': nrm((L - 1, D_A), 0.5),
        'k_k': 0.85 + nrm((L, D_A), 0.05),
        'k_a': 1.0 + nrm((L, D_A), 0.05),
        'r_k': nrm((L, N_HEADS_A, HEAD_A), 0.1),
        'gn_g': 1.0 + nrm((L, D_A), 0.05),
        'gn_b': nrm((L, D_A), 0.01),
        'ln_v_g': 1.0 + nrm((L, D_B), 0.05),
        'ln_v_b': nrm((L, D_B), 0.01),
        'w_spatial': nrm((L, N_GROUPS_B, CHUNK, CHUNK), CHUNK ** -0.5),
        'b_spatial': 1.0 + nrm((L, N_GROUPS_B, CHUNK), 0.1),
        'w_br_a': nrm((L, D_A, D_MODEL), D_A ** -0.5),
        'w_br_b': nrm((L, D_B, D_MODEL), D_B ** -0.5),
        'w_out': nrm((L, D_MODEL, D_MODEL), 0.5 * D_MODEL ** -0.5),
        'w_ple': nrm((L, D_PLE, D_MODEL), 0.5 * D_PLE ** -0.5),
        'w_ple_gate': nrm((L, D_MODEL, D_MODEL), D_MODEL ** -0.5),
        'b_ple_gate': nrm((L, D_MODEL), 0.01),
        'final_g': 1.0 + nrm((D_MODEL,), 0.05),
    }


def reference(x_prompt, x_sample, state_rwkv_wkv, state_rwkv_shift, p_prompt, p_sample,
              norm_g, w_in, shift_mu, w0, w_up, a0, a_up, vres_down, vres_up, vres_b,
              k_k, k_a, r_k, gn_g, gn_b, ln_v_g, ln_v_b, w_spatial, b_spatial,
              w_br_a, w_br_b, w_out, w_ple, w_ple_gate, b_ple_gate, final_g):
    W = dict(norm_g=norm_g, w_in=w_in, shift_mu=shift_mu, w0=w0, w_up=w_up, a0=a0, a_up=a_up,
             vres_down=vres_down, vres_up=vres_up, vres_b=vres_b, k_k=k_k, k_a=k_a, r_k=r_k,
             gn_g=gn_g, gn_b=gn_b, ln_v_g=ln_v_g, ln_v_b=ln_v_b, w_spatial=w_spatial,
             b_spatial=b_spatial, w_br_a=w_br_a, w_br_b=w_br_b, w_out=w_out, w_ple=w_ple,
             w_ple_gate=w_ple_gate, b_ple_gate=b_ple_gate, final_g=final_g)
    wkv0_p = jnp.zeros((DEPTH, x_prompt.shape[0], N_HEADS_A, HEAD_A, HEAD_A), state_rwkv_wkv.dtype)
    shift0_p = jnp.zeros((DEPTH, x_prompt.shape[0], N_SHIFT), state_rwkv_shift.dtype)
    y_prompt, wkv_prompt, shift_prompt, _ = _forward(x_prompt, p_prompt, wkv0_p, shift0_p, W)
    y_sample, wkv_sample, shift_sample, chunk_v_sample = _forward(
        x_sample, p_sample, state_rwkv_wkv, state_rwkv_shift, W)
    return (y_prompt, y_sample, wkv_prompt, shift_prompt, wkv_sample, shift_sample, chunk_v_sample)
```

```python
import functools

import jax
import jax.numpy as jnp
from jax import lax
from jax.experimental import pallas as pl
from jax.experimental.pallas import tpu as pltpu

D_MODEL = 1024
HEAD = 64
N_HEADS = 8
D_A = N_HEADS * HEAD
D_LORA = 64
D_B = 512
N_GROUPS_B = 4
GROUP_B = D_B // N_GROUPS_B
SPATIAL_CHUNK = 128
D_PLE = 256
N_SHIFT = 3 * D_A + 2 * D_LORA
D_IN = N_SHIFT + D_A + 3 * D_B + 2 * D_MODEL
EPS = 1e-6
GN_EPS = 64e-5
LN_EPS = 1e-5

O_GA = N_SHIFT
O_U = O_GA + D_A
O_VB = O_U + D_B
O_GB = O_VB + D_B
O_MA = O_GB + D_B
O_MB = O_MA + D_MODEL

WKV_CHUNK = 64
HEADS_PER_GROUP = 4
GROUP_W = HEADS_PER_GROUP * HEAD
N_HGROUPS = N_HEADS // HEADS_PER_GROUP
VMEM_LIMIT_BYTES = 56 * 1024 * 1024

F32 = jnp.float32
BF16 = jnp.bfloat16

NN = ((1,), (0,))
NT = ((1,), (1,))
TN = ((0,), (0,))


def _dg(a, b, dims):
    return lax.dot_general(a, b, (dims, ((), ())), preferred_element_type=F32)


def _split(x):
    hi = x.astype(BF16)
    lo = (x - hi.astype(F32)).astype(BF16)
    return hi, lo


def _dot3(a, b, dims):
    ah, al = a
    bh, bl = b
    return _dg(ah, bh, dims) + (_dg(al, bh, dims) + _dg(ah, bl, dims))


def _sigmoid(x):
    return 1.0 / (1.0 + jnp.exp(-x))


def _gelu_tanh(x):
    return 0.5 * x * (1.0 + jnp.tanh(0.7978845608028654 * (x + 0.044715 * (x * x * x))))


def _softplus(x):
    return jnp.maximum(x, 0.0) + jnp.log(1.0 + jnp.exp(-jnp.abs(x)))


def _head_ones(n):
    r = lax.broadcasted_iota(jnp.int32, (n, n), 0) // HEAD
    c = lax.broadcasted_iota(jnp.int32, (n, n), 1) // HEAD
    return jnp.where(r == c, 1.0, 0.0).astype(BF16)


def _headsum(x, ones_bd):
    hi, lo = _split(x)
    return _dg(hi, ones_bd, NN) + _dg(lo, ones_bd, NN)


def _proj_kernel(*refs, has_vres):
    if has_vres:
        (x_ref, g_ref, w_ref, lng_ref, lnb_ref, vd_ref, vu_ref, vb_ref,
         z_ref, ga_ref, u_ref, vn_ref, gb_ref, ma_ref, mb_ref, vg_ref) = refs
    else:
        (x_ref, g_ref, w_ref, lng_ref, lnb_ref,
         z_ref, ga_ref, u_ref, vn_ref, gb_ref, ma_ref, mb_ref) = refs
    x = x_ref[...]
    xn = x * lax.rsqrt(jnp.mean(x * x, axis=-1, keepdims=True) + EPS) * g_ref[...]
    xb = xn.astype(BF16)

    def seg(lo, hi):
        return jnp.dot(xb, w_ref[:, lo:hi], preferred_element_type=F32)

    z_ref[...] = seg(0, N_SHIFT)
    t = seg(O_GA, O_U)
    ga_ref[...] = t * _sigmoid(t)
    u_ref[...] = _gelu_tanh(seg(O_U, O_VB))
    vb = _gelu_tanh(seg(O_VB, O_GB))
    mu = jnp.mean(vb, axis=-1, keepdims=True)
    d = vb - mu
    var = jnp.mean(d * d, axis=-1, keepdims=True)
    vn_ref[...] = d * lax.rsqrt(var + LN_EPS) * lng_ref[...] + lnb_ref[...]
    t = seg(O_GB, O_MA)
    gb_ref[...] = t * _sigmoid(t)
    ma_ref[...] = _sigmoid(seg(O_MA, O_MB))
    mb_ref[...] = _sigmoid(seg(O_MB, D_IN))
    if has_vres:
        low = jnp.dot(xb, vd_ref[...], preferred_element_type=F32)
        up = jnp.dot(low.astype(BF16), vu_ref[...], preferred_element_type=F32)
        vg_ref[...] = _sigmoid(vb_ref[...] + up)


def _proj_call(x2d, norm_g, w_in_bf, ln_g, ln_b, vres, tm):
    n = x2d.shape[0]
    has_vres = vres is not None

    def full(shape):
        return pl.BlockSpec(shape, lambda i: (0,) * len(shape))

    def rows(w):
        return pl.BlockSpec((tm, w), lambda i: (i, 0))

    in_specs = [rows(D_MODEL), full((1, D_MODEL)), full((D_MODEL, D_IN)), full((1, D_B)), full((1, D_B))]
    args = [x2d, norm_g, w_in_bf, ln_g, ln_b]
    widths = [N_SHIFT, D_A, D_B, D_B, D_B, D_MODEL, D_MODEL]
    if has_vres:
        vd, vu, vb = vres
        in_specs += [full(vd.shape), full(vu.shape), full((1, D_A))]
        args += [vd, vu, vb]
        widths.append(D_A)
    return pl.pallas_call(
        functools.partial(_proj_kernel, has_vres=has_vres),
        grid=(n // tm,),
        in_specs=in_specs,
        out_specs=[rows(w) for w in widths],
        out_shape=[jax.ShapeDtypeStruct((n, w), F32) for w in widths],
        compiler_params=pltpu.CompilerParams(
            dimension_semantics=("arbitrary",), vmem_limit_bytes=VMEM_LIMIT_BYTES),
        name="proj_vres" if has_vres else "proj",
    )(*args)


def _timemix_pre(z, z_prev, mu, w0, wup_pad, a0, aup_pad, k_k, k_a, vres, ones_bd):
    zs = z + (z_prev - z) * mu
    r = zs[:, 0:D_A]
    k = zs[:, D_A:2 * D_A]
    v = zs[:, 2 * D_A:3 * D_A]
    lora_in = zs[:, 3 * D_A:N_SHIFT]
    wq = w0 + jnp.dot(jnp.tanh(lora_in).astype(BF16), wup_pad, preferred_element_type=F32)
    w = -_softplus(-wq) - 0.5
    lw = -jnp.exp(w)
    a = _sigmoid(a0 + jnp.dot(lora_in.astype(BF16), aup_pad, preferred_element_type=F32))
    if vres is not None:
        vgate, vfirst = vres
        v = v + (vfirst - v) * vgate
    kk = k * k_k
    nrm = jnp.sqrt(_headsum(kk * kk, ones_bd))
    kk = kk / jnp.maximum(nrm, 1e-12)
    k2 = k * (1.0 + (a - 1.0) * k_a)
    return r, k2, v, -kk, kk * a, lw


def _chunk_group(At, Rt, Bt, Kt, Bh, Kh, V, S, wcc_row, masks):
    strict, incl, eye, bdmask = masks
    C = At.shape[0]

    def bd(p):
        return tuple(jnp.where(bdmask, jnp.concatenate([q] * HEADS_PER_GROUP, axis=0), 0) for q in p)

    ar = _split(jnp.concatenate([At, Rt], axis=0))
    sc_b = _dot3(ar, bd(_split(Bt)), NT)
    sc_k = _dot3(ar, bd(_split(Kt)), NT)
    N = jnp.where(strict, sc_b[:C], 0.0)
    Srb = jnp.where(incl, sc_b[C:], 0.0)
    M = jnp.where(strict, sc_k[:C], 0.0)
    Srk = jnp.where(incl, sc_k[C:], 0.0)

    T = jnp.where(eye, 1.0, 0.0) + N
    np_s = _split(N)
    np_bd = bd(np_s)
    n_sq = C.bit_length() - 2
    for _ in range(n_sq):
        Np = _dot3(np_s, np_bd, NN)
        np_s = _split(Np)
        np_bd = bd(np_s)
        T = T + _dot3(_split(T), np_bd, NN)

    ar_s = _dot3(ar, _split(S), NT)
    mv = _dot3(_split(jnp.concatenate([M, Srk], axis=0)), bd(_split(V)), NN)
    X = ar_s[:C] + mv[:C]
    U = _dot3(_split(T), bd(_split(X)), NN)
    Y = ar_s[C:] + mv[C:] + _dot3(_split(Srb), bd(_split(U)), NN)
    upd = _dot3(_split(jnp.concatenate([U, V], axis=0)),
                _split(jnp.concatenate([Bh, Kh], axis=0)), TN)
    S_new = S * wcc_row + jnp.where(bdmask, upd, 0.0)
    return Y, S_new


def _wkv_chunk_kernel(*refs, bb, has_vres):
    if has_vres:
        (z_ref, vg_ref, vf_ref, s0_ref, sh0_ref, mu_ref, w0_ref, wup_ref, a0_ref, aup_ref,
         kk_ref, ka_ref, rk_ref, gng_ref, gnb_ref, y_ref, v_ref, s_ref, zp_scr) = refs
    else:
        (z_ref, s0_ref, sh0_ref, mu_ref, w0_ref, wup_ref, a0_ref, aup_ref,
         kk_ref, ka_ref, rk_ref, gng_ref, gnb_ref, y_ref, v_ref, s_ref, zp_scr) = refs
    C = z_ref.shape[1]
    c = pl.program_id(1)

    @pl.when(c == 0)
    def _():
        s_ref[...] = s0_ref[...]
        zp_scr[...] = sh0_ref[...]

    ones_bd = _head_ones(D_A)
    t_idx = lax.broadcasted_iota(jnp.int32, (C, GROUP_W), 0)
    s_idx = lax.broadcasted_iota(jnp.int32, (C, GROUP_W), 1) % C
    rr = lax.broadcasted_iota(jnp.int32, (GROUP_W, GROUP_W), 0) // HEAD
    cc = lax.broadcasted_iota(jnp.int32, (GROUP_W, GROUP_W), 1) // HEAD
    masks = (s_idx < t_idx, s_idx <= t_idx, s_idx == t_idx, rr == cc)
    tri = jnp.where(lax.broadcasted_iota(jnp.int32, (C, C), 1)
                    <= lax.broadcasted_iota(jnp.int32, (C, C), 0), 1.0, 0.0).astype(BF16)
    row0 = lax.broadcasted_iota(jnp.int32, (C, N_SHIFT), 0) == 0

    for bi in range(bb):
        z = z_ref[bi]
        z_prev = jnp.where(row0, zp_scr[bi], pltpu.roll(z, 1, 0))
        zp_scr[bi] = z[C - 1:C, :]
        vres = (vg_ref[bi], vf_ref[bi]) if has_vres else None
        r, k2, v, aa, bbv, lw = _timemix_pre(
            z, z_prev, mu_ref[...], w0_ref[...], wup_ref[...], a0_ref[...], aup_ref[...],
            kk_ref[...], ka_ref[...], vres, ones_bd)
        v_ref[bi] = v

        p1 = lw.astype(BF16)
        r1 = lw - p1.astype(F32)
        p2 = r1.astype(BF16)
        p3 = (r1 - p2.astype(F32)).astype(BF16)
        cum = _dg(tri, p1, NN) + (_dg(tri, p2, NN) + _dg(tri, p3, NN))
        cum_c = cum[C - 1:C, :]
        wc = jnp.exp(cum)
        wi = jnp.exp(-cum)
        wrel = jnp.exp(cum_c - cum)
        At = aa * jnp.exp(cum - lw)
        Rt = r * wc
        Bt = bbv * wi
        Kt = k2 * wi
        Bh = bbv * wrel
        Kh = k2 * wrel
        wcc = jnp.exp(cum_c)

        ys = []
        for g in range(N_HGROUPS):
            sl = slice(g * GROUP_W, (g + 1) * GROUP_W)
            Y, S_new = _chunk_group(At[:, sl], Rt[:, sl], Bt[:, sl], Kt[:, sl], Bh[:, sl], Kh[:, sl],
                                    v[:, sl], s_ref[bi, g], wcc[:, sl], masks)
            s_ref[bi, g] = S_new
            ys.append(Y)
        y = jnp.concatenate(ys, axis=1)

        m = _headsum(y, ones_bd) * (1.0 / HEAD)
        d = y - m
        var = _headsum(d * d, ones_bd) * (1.0 / HEAD)
        yn = d * lax.rsqrt(var + GN_EPS) * gng_ref[...] + gnb_ref[...]
        bonus = _headsum(r * k2 * rk_ref[...], ones_bd) * v
        y_ref[bi] = yn + bonus


def _wkv_chunk_call(z3, vres, s0_bd, shift0, p, bb):
    B, T, _ = z3.shape
    C = WKV_CHUNK
    has_vres = vres is not None

    def tok(w):
        return pl.BlockSpec((bb, C, w), lambda b, c: (b, c, 0))

    def full(shape):
        return pl.BlockSpec(shape, lambda b, c: (0,) * len(shape))

    s_spec = pl.BlockSpec((bb, N_HGROUPS, GROUP_W, GROUP_W), lambda b, c: (b, 0, 0, 0))
    in_specs = [tok(N_SHIFT)]
    args = [z3]
    if has_vres:
        in_specs += [tok(D_A), tok(D_A)]
        args += list(vres)
    in_specs += [s_spec, pl.BlockSpec((bb, 1, N_SHIFT), lambda b, c: (b, 0, 0))]
    args += [s0_bd, shift0]
    for name in ("mu", "w0", "wup", "a0", "aup", "k_k", "k_a", "r_k", "gn_g", "gn_b"):
        in_specs.append(full(p[name].shape))
        args.append(p[name])
    return pl.pallas_call(
        functools.partial(_wkv_chunk_kernel, bb=bb, has_vres=has_vres),
        grid=(B // bb, T // C),
        in_specs=in_specs,
        out_specs=[tok(D_A), tok(D_A), s_spec],
        out_shape=[jax.ShapeDtypeStruct((B, T, D_A), F32), jax.ShapeDtypeStruct((B, T, D_A), F32),
                   jax.ShapeDtypeStruct(s0_bd.shape, F32)],
        scratch_shapes=[pltpu.VMEM((bb, 1, N_SHIFT), F32)],
        compiler_params=pltpu.CompilerParams(
            dimension_semantics=("arbitrary", "arbitrary"), vmem_limit_bytes=VMEM_LIMIT_BYTES),
        name="wkv_chunk_vres" if has_vres else "wkv_chunk",
    )(*args)


def _wkv_pre_kernel(*refs, has_vres):
    if has_vres:
        (z_ref, zp_ref, vg_ref, vf_ref, mu_ref, w0_ref, wup_ref, a0_ref, aup_ref, kk_ref, ka_ref, rk_ref,
         r_o, w_o, k_o, v_o, a_o, b_o, bonus_o) = refs
        vres = (vg_ref[...], vf_ref[...])
    else:
        (z_ref, zp_ref, mu_ref, w0_ref, wup_ref, a0_ref, aup_ref, kk_ref, ka_ref, rk_ref,
         r_o, w_o, k_o, v_o, a_o, b_o, bonus_o) = refs
        vres = None
    ones_bd = _head_ones(D_A)
    r, k2, v, aa, bbv, lw = _timemix_pre(
        z_ref[...], zp_ref[...], mu_ref[...], w0_ref[...], wup_ref[...], a0_ref[...], aup_ref[...],
        kk_ref[...], ka_ref[...], vres, ones_bd)
    r_o[...] = r
    w_o[...] = jnp.exp(lw)
    k_o[...] = k2
    v_o[...] = v
    a_o[...] = aa
    b_o[...] = bbv
    bonus_o[...] = _headsum(r * k2 * rk_ref[...], ones_bd) * v


def _wkv_pre_call(z, z_prev, vres, p):
    n = z.shape[0]
    has_vres = vres is not None
    args = [z, z_prev] + (list(vres) if has_vres else [])
    args += [p[k] for k in ("mu", "w0", "wup", "a0", "aup", "k_k", "k_a", "r_k")]
    return pl.pallas_call(
        functools.partial(_wkv_pre_kernel, has_vres=has_vres),
        out_shape=[jax.ShapeDtypeStruct((n, D_A), F32)] * 7,
        compiler_params=pltpu.CompilerParams(vmem_limit_bytes=VMEM_LIMIT_BYTES),
        name="wkv_pre_vres" if has_vres else "wkv_pre",
    )(*args)


def _wkv_step_kernel(s_ref, r_ref, w_ref, k_ref, v_ref, a_ref, b_ref, bonus_ref, gng_ref, gnb_ref,
                     y_ref, so_ref):
    S = s_ref[...]
    eye = (lax.broadcasted_iota(jnp.int32, (HEAD, HEAD), 0)
           == lax.broadcasted_iota(jnp.int32, (HEAD, HEAD), 1))
    sa = jnp.sum(S * a_ref[...], axis=-1, keepdims=True)
    vcol = jnp.sum(jnp.where(eye, v_ref[...], 0.0), axis=-1, keepdims=True)
    S2 = S * w_ref[...] + sa * b_ref[...] + vcol * k_ref[...]
    so_ref[...] = S2
    ycol = jnp.sum(S2 * r_ref[...], axis=-1, keepdims=True)
    y = jnp.sum(jnp.where(eye, ycol, 0.0), axis=-2, keepdims=True)
    m = jnp.mean(y, axis=-1, keepdims=True)
    d = y - m
    var = jnp.mean(d * d, axis=-1, keepdims=True)
    y_ref[...] = d * lax.rsqrt(var + GN_EPS) * gng_ref[...] + gnb_ref[...] + bonus_ref[...]


def _wkv_step_call(S, vecs, gng_t, gnb_t, tb):
    n = S.shape[0]
    vec_spec = pl.BlockSpec((tb, 1, HEAD), lambda i: (i, 0, 0))
    s_spec = pl.BlockSpec((tb, HEAD, HEAD), lambda i: (i, 0, 0))
    full = pl.BlockSpec((tb, 1, HEAD), lambda i: (0, 0, 0))
    return pl.pallas_call(
        _wkv_step_kernel,
        grid=(n // tb,),
        in_specs=[s_spec] + [vec_spec] * 7 + [full, full],
        out_specs=[vec_spec, s_spec],
        out_shape=[jax.ShapeDtypeStruct((n, 1, HEAD), F32), jax.ShapeDtypeStruct(S.shape, F32)],
        compiler_params=pltpu.CompilerParams(
            dimension_semantics=("arbitrary",), vmem_limit_bytes=VMEM_LIMIT_BYTES),
        name="wkv_step",
    )(S, *vecs, gng_t, gnb_t)


def _merge_kernel(*refs, chunked, final):
    (h_ref, ya_ref, ga_ref, u_ref, vn_ref, gb_ref, ma_ref, mb_ref, p_ref,
     sp_a_ref, sp_b_ref, wba_ref, wbb_ref, wout_ref, wple_ref, wpg_ref, bpg_ref) = refs[:17]
    rest = refs[17:]
    if final:
        fg_ref, h_out, y_out = rest
    else:
        (h_out,) = rest
    vn = vn_ref[...]
    u = u_ref[...]
    if chunked:
        tm = vn.shape[0]
        tril = (lax.broadcasted_iota(jnp.int32, (SPATIAL_CHUNK, SPATIAL_CHUNK), 1)
                <= lax.broadcasted_iota(jnp.int32, (SPATIAL_CHUNK, SPATIAL_CHUNK), 0))
        vb16 = vn.astype(BF16)
        rows_out = []
        for ci in range(tm // SPATIAL_CHUNK):
            rs = slice(ci * SPATIAL_CHUNK, (ci + 1) * SPATIAL_CHUNK)
            cols = []
            for g in range(N_GROUPS_B):
                wm = jnp.where(tril, sp_a_ref[g], 0.0).astype(BF16)
                mixed = jnp.dot(wm, vb16[rs, g * GROUP_B:(g + 1) * GROUP_B], preferred_element_type=F32)
                cols.append(mixed + sp_b_ref[:, g:g + 1])
            rows_out.append(jnp.concatenate(cols, axis=1))
        mixed = jnp.concatenate(rows_out, axis=0)
    else:
        mixed = vn * sp_a_ref[...] + sp_b_ref[...]
    yb = u * mixed
    br_a = jnp.dot((ya_ref[...] * ga_ref[...]).astype(BF16), wba_ref[...], preferred_element_type=F32)
    br_b = jnp.dot((yb * gb_ref[...]).astype(BF16), wbb_ref[...], preferred_element_type=F32)
    merged = ma_ref[...] * br_a + mb_ref[...] * br_b
    h = h_ref[...] + jnp.dot(merged.astype(BF16), wout_ref[...], preferred_element_type=F32)
    ple = jnp.dot(p_ref[...].astype(BF16), wple_ref[...], preferred_element_type=F32)
    gate = _sigmoid(jnp.dot(h.astype(BF16), wpg_ref[...], preferred_element_type=F32) + bpg_ref[...])
    h = h + gate * ple
    h_out[...] = h
    if final:
        y_out[...] = h * lax.rsqrt(jnp.mean(h * h, axis=-1, keepdims=True) + EPS) * fg_ref[...]


def _merge_call(h, ya, ga, u, vn, gb, ma, mb, pe, sp_a, sp_b, wl, final_g, chunked, tm):
    n = h.shape[0]
    final = final_g is not None

    def rows(w):
        return pl.BlockSpec((tm, w), lambda i: (i, 0))

    def full(a):
        return pl.BlockSpec(a.shape, lambda i: (0,) * a.ndim)

    row_args = [h, ya, ga, u, vn, gb, ma, mb, pe]
    w_args = [sp_a, sp_b, wl["w_br_a"], wl["w_br_b"], wl["w_out"], wl["w_ple"], wl["w_pg"], wl["b_pg"]]
    if final:
        w_args.append(final_g)
    n_out = 2 if final else 1
    return pl.pallas_call(
        functools.partial(_merge_kernel, chunked=chunked, final=final),
        grid=(n // tm,),
        in_specs=[rows(a.shape[1]) for a in row_args] + [full(a) for a in w_args],
        out_specs=[rows(D_MODEL)] * n_out,
        out_shape=[jax.ShapeDtypeStruct((n, D_MODEL), F32)] * n_out,
        compiler_params=pltpu.CompilerParams(
            dimension_semantics=("arbitrary",), vmem_limit_bytes=VMEM_LIMIT_BYTES),
        name="merge_" + ("c" if chunked else "s") + ("f" if final else ""),
    )(*row_args, *w_args)


def _layer_params(W, l):
    row = lambda a: a.reshape(1, -1)
    zpad = jnp.zeros((D_LORA, D_A), F32)
    p = dict(
        norm_g=row(W["norm_g"][l]), w_in=W["w_in"][l].astype(BF16),
        ln_g=row(W["ln_v_g"][l]), ln_b=row(W["ln_v_b"][l]),
        mu=row(W["shift_mu"][l]), w0=row(W["w0"][l]), a0=row(W["a0"][l]),
        wup=jnp.concatenate([W["w_up"][l], zpad], axis=0).astype(BF16),
        aup=jnp.concatenate([zpad, W["a_up"][l]], axis=0).astype(BF16),
        k_k=row(W["k_k"][l]), k_a=row(W["k_a"][l]), r_k=row(W["r_k"][l]),
        gn_g=row(W["gn_g"][l]), gn_b=row(W["gn_b"][l]),
        w_br_a=W["w_br_a"][l].astype(BF16), w_br_b=W["w_br_b"][l].astype(BF16),
        w_out=W["w_out"][l].astype(BF16), w_ple=W["w_ple"][l].astype(BF16),
        w_pg=W["w_ple_gate"][l].astype(BF16), b_pg=row(W["b_ple_gate"][l]),
        sp_w=W["w_spatial"][l], sp_bT=W["b_spatial"][l].T,
        sp_w0=jnp.repeat(W["w_spatial"][l][:, 0, 0], GROUP_B).reshape(1, D_B),
        sp_b0=jnp.repeat(W["b_spatial"][l][:, 0], GROUP_B).reshape(1, D_B),
    )
    if l > 0:
        p["vres"] = (W["vres_down"][l - 1].astype(BF16), W["vres_up"][l - 1].astype(BF16),
                     row(W["vres_b"][l - 1]))
    else:
        p["vres"] = None
    return p


def _diag_blocks(s_bd):
    B = s_bd.shape[0]
    s = s_bd.reshape(B, N_HGROUPS, HEADS_PER_GROUP, HEAD, HEADS_PER_GROUP, HEAD)
    idx = jnp.arange(HEADS_PER_GROUP)
    s = s[:, :, idx, :, idx, :]
    return jnp.moveaxis(s, 0, 2).reshape(B, N_HEADS, HEAD, HEAD)


def _forward_prompt(x, pe, W, params):
    B, T, _ = x.shape
    n = B * T
    h = x.reshape(n, D_MODEL)
    v_first = None
    wkvs, shifts = [], []
    depth = len(params)
    for l, p in enumerate(params):
        outs = _proj_call(h, p["norm_g"], p["w_in"], p["ln_g"], p["ln_b"], p["vres"], tm=256)
        z, ga, u, vn, gb, ma, mb = outs[:7]
        z3 = z.reshape(B, T, N_SHIFT)
        vres = None if l == 0 else (outs[7].reshape(B, T, D_A), v_first)
        s0 = jnp.zeros((B, N_HGROUPS, GROUP_W, GROUP_W), F32)
        shift0 = jnp.zeros((B, 1, N_SHIFT), F32)
        ya, v, s_bd = _wkv_chunk_call(z3, vres, s0, shift0, p, bb=2)
        if l == 0:
            v_first = v
        wkvs.append(_diag_blocks(s_bd))
        shifts.append(z3[:, T - 1, :])
        fg = W["final_g"].reshape(1, -1) if l == depth - 1 else None
        res = _merge_call(h, ya.reshape(n, D_A), ga, u, vn, gb, ma, mb, pe[l].reshape(n, D_PLE),
                          p["sp_w"], p["sp_bT"], p, fg, chunked=True, tm=256)
        h = res[0]
    y = res[1].reshape(B, T, D_MODEL)
    return y, jnp.stack(wkvs), jnp.stack(shifts)


def _forward_sample(x, pe, wkv0, shift0, W, params):
    B = x.shape[0]
    h = x.reshape(B, D_MODEL)
    v_first = None
    wkvs, shifts, chunk_vs = [], [], []
    depth = len(params)
    tb = 64
    for l, p in enumerate(params):
        outs = _proj_call(h, p["norm_g"], p["w_in"], p["ln_g"], p["ln_b"], p["vres"], tm=B)
        z, ga, u, vn, gb, ma, mb = outs[:7]
        vres = None if l == 0 else (outs[7], v_first)
        r, w, k2, v, aa, bbv, bonus = _wkv_pre_call(z, shift0[l], vres, p)
        if l == 0:
            v_first = v
        vecs = [a.reshape(B * N_HEADS, 1, HEAD) for a in (r, w, k2, v, aa, bbv, bonus)]
        gng_t = jnp.tile(p["gn_g"].reshape(N_HEADS, 1, HEAD), (tb // N_HEADS, 1, 1))
        gnb_t = jnp.tile(p["gn_b"].reshape(N_HEADS, 1, HEAD), (tb // N_HEADS, 1, 1))
        ya, s_new = _wkv_step_call(wkv0[l].reshape(B * N_HEADS, HEAD, HEAD), vecs, gng_t, gnb_t, tb)
        wkvs.append(s_new.reshape(B, N_HEADS, HEAD, HEAD))
        shifts.append(z)
        chunk_vs.append(vn.reshape(B, 1, D_B))
        fg = W["final_g"].reshape(1, -1) if l == depth - 1 else None
        res = _merge_call(h, ya.reshape(B, D_A), ga, u, vn, gb, ma, mb, pe[l].reshape(B, D_PLE),
                          p["sp_w0"], p["sp_b0"], p, fg, chunked=False, tm=B)
        h = res[0]
    y = res[1].reshape(B, 1, D_MODEL)
    return y, jnp.stack(wkvs), jnp.stack(shifts), jnp.stack(chunk_vs)


def kernel(x_prompt, x_sample, state_rwkv_wkv, state_rwkv_shift, p_prompt, p_sample, norm_g, w_in, shift_mu, w0, w_up, a0, a_up, vres_down, vres_up, vres_b, k_k, k_a, r_k, gn_g, gn_b, ln_v_g, ln_v_b, w_spatial, b_spatial, w_br_a, w_br_b, w_out, w_ple, w_ple_gate, b_ple_gate, final_g):
    W = dict(norm_g=norm_g, w_in=w_in, shift_mu=shift_mu, w0=w0, w_up=w_up, a0=a0, a_up=a_up,
             vres_down=vres_down, vres_up=vres_up, vres_b=vres_b, k_k=k_k, k_a=k_a, r_k=r_k,
             gn_g=gn_g, gn_b=gn_b, ln_v_g=ln_v_g, ln_v_b=ln_v_b, w_spatial=w_spatial,
             b_spatial=b_spatial, w_br_a=w_br_a, w_br_b=w_br_b, w_out=w_out, w_ple=w_ple,
             w_ple_gate=w_ple_gate, b_ple_gate=b_ple_gate, final_g=final_g)
    params = [_layer_params(W, l) for l in range(w_in.shape[0])]
    y_p, wkv_p, shift_p = _forward_prompt(x_prompt, p_prompt, W, params)
    y_s, wkv_s, shift_s, chunk_v = _forward_sample(
        x_sample, p_sample, state_rwkv_wkv, state_rwkv_shift, W, params)
    return (y_p, y_s, wkv_p, shift_p, wkv_s, shift_s, chunk_v)
```

```python
import functools

import jax
import jax.numpy as jnp
from jax import lax
from jax.experimental import pallas as pl
from jax.experimental.pallas import tpu as pltpu

D_MODEL = 1024
HEAD = 64
N_HEADS = 8
D_A = N_HEADS * HEAD
D_LORA = 64
D_B = 512
N_GROUPS_B = 4
GROUP_B = D_B // N_GROUPS_B
SPATIAL_CHUNK = 128
D_PLE = 256
N_SHIFT = 3 * D_A + 2 * D_LORA
D_IN = N_SHIFT + D_A + 3 * D_B + 2 * D_MODEL
EPS = 1e-6
GN_EPS = 64e-5
LN_EPS = 1e-5

O_GA = N_SHIFT
O_U = O_GA + D_A
O_VB = O_U + D_B
O_GB = O_VB + D_B
O_MA = O_GB + D_B
O_MB = O_MA + D_MODEL

WKV_CHUNK = 64
HEADS_PER_GROUP = 4
GROUP_W = HEADS_PER_GROUP * HEAD
N_HGROUPS = N_HEADS // HEADS_PER_GROUP
VMEM_LIMIT_BYTES = 56 * 1024 * 1024

F32 = jnp.float32
BF16 = jnp.bfloat16

NN = ((1,), (0,))
NT = ((1,), (1,))
TN = ((0,), (0,))


def _dg(a, b, dims):
    return lax.dot_general(a, b, (dims, ((), ())), preferred_element_type=F32)


def _split(x):
    hi = x.astype(BF16)
    lo = (x - hi.astype(F32)).astype(BF16)
    return hi, lo


def _dot3(a, b, dims):
    ah, al = a
    bh, bl = b
    if dims == TN:
        both = _dg(jnp.concatenate([ah, al], axis=0), jnp.concatenate([bh, bh], axis=0), TN)
        return both + _dg(ah, bl, TN)
    m = ah.shape[0]
    both = _dg(jnp.concatenate([ah, al], axis=0), bh, dims)
    return both[:m] + both[m:] + _dg(ah, bl, dims)


def _sigmoid(x):
    return 1.0 / (1.0 + jnp.exp(-x))


def _gelu_tanh(x):
    return 0.5 * x * (1.0 + jnp.tanh(0.7978845608028654 * (x + 0.044715 * (x * x * x))))


def _softplus(x):
    return jnp.maximum(x, 0.0) + jnp.log(1.0 + jnp.exp(-jnp.abs(x)))


def _head_ones(n):
    r = lax.broadcasted_iota(jnp.int32, (n, n), 0) // HEAD
    c = lax.broadcasted_iota(jnp.int32, (n, n), 1) // HEAD
    return jnp.where(r == c, 1.0, 0.0).astype(BF16)


def _headsums(xs, ones_g):
    rows = xs[0].shape[0]
    pieces = []
    for x in xs:
        hi, lo = _split(x)
        pieces += [hi[:, :GROUP_W], lo[:, :GROUP_W], hi[:, GROUP_W:], lo[:, GROUP_W:]]
    o = _dg(jnp.concatenate(pieces, axis=0), ones_g, NN)
    outs = []
    for i in range(len(xs)):
        q = [o[(4 * i + j) * rows:(4 * i + j + 1) * rows] for j in range(4)]
        outs.append(jnp.concatenate([q[0] + q[1], q[2] + q[3]], axis=1))
    return outs


def _proj_kernel(*refs, has_vres):
    if has_vres:
        (x_ref, g_ref, w_ref, lng_ref, lnb_ref, vd_ref, vu_ref, vb_ref,
         z_ref, ga_ref, u_ref, vn_ref, gb_ref, ma_ref, mb_ref, vg_ref) = refs
    else:
        (x_ref, g_ref, w_ref, lng_ref, lnb_ref,
         z_ref, ga_ref, u_ref, vn_ref, gb_ref, ma_ref, mb_ref) = refs
    x = x_ref[...]
    xn = x * lax.rsqrt(jnp.mean(x * x, axis=-1, keepdims=True) + EPS) * g_ref[...]
    xb = xn.astype(BF16)

    def seg(lo, hi):
        return jnp.dot(xb, w_ref[:, lo:hi], preferred_element_type=F32)

    z_ref[...] = seg(0, N_SHIFT)
    t = seg(O_GA, O_U)
    ga_ref[...] = t * _sigmoid(t)
    u_ref[...] = _gelu_tanh(seg(O_U, O_VB))
    vb = _gelu_tanh(seg(O_VB, O_GB))
    mu = jnp.mean(vb, axis=-1, keepdims=True)
    d = vb - mu
    var = jnp.mean(d * d, axis=-1, keepdims=True)
    vn_ref[...] = d * lax.rsqrt(var + LN_EPS) * lng_ref[...] + lnb_ref[...]
    t = seg(O_GB, O_MA)
    gb_ref[...] = t * _sigmoid(t)
    ma_ref[...] = _sigmoid(seg(O_MA, O_MB))
    mb_ref[...] = _sigmoid(seg(O_MB, D_IN))
    if has_vres:
        low = jnp.dot(xb, vd_ref[...], preferred_element_type=F32)
        up = jnp.dot(low.astype(BF16), vu_ref[...], preferred_element_type=F32)
        vg_ref[...] = _sigmoid(vb_ref[...] + up)


def _proj_call(x2d, norm_g, w_in_bf, ln_g, ln_b, vres, tm):
    n = x2d.shape[0]
    has_vres = vres is not None

    def full(shape):
        return pl.BlockSpec(shape, lambda i: (0,) * len(shape))

    def rows(w):
        return pl.BlockSpec((tm, w), lambda i: (i, 0))

    in_specs = [rows(D_MODEL), full((1, D_MODEL)), full((D_MODEL, D_IN)), full((1, D_B)), full((1, D_B))]
    args = [x2d, norm_g, w_in_bf, ln_g, ln_b]
    widths = [N_SHIFT, D_A, D_B, D_B, D_B, D_MODEL, D_MODEL]
    if has_vres:
        vd, vu, vb = vres
        in_specs += [full(vd.shape), full(vu.shape), full((1, D_A))]
        args += [vd, vu, vb]
        widths.append(D_A)
    return pl.pallas_call(
        functools.partial(_proj_kernel, has_vres=has_vres),
        grid=(n // tm,),
        in_specs=in_specs,
        out_specs=[rows(w) for w in widths],
        out_shape=[jax.ShapeDtypeStruct((n, w), F32) for w in widths],
        compiler_params=pltpu.CompilerParams(
            dimension_semantics=("arbitrary",), vmem_limit_bytes=VMEM_LIMIT_BYTES),
        name="proj_vres" if has_vres else "proj",
    )(*args)


def _timemix_pre(z, z_prev, mu, w0, wup_pad, a0, aup_pad, k_k, k_a, r_k, vres, ones_g):
    zs = z + (z_prev - z) * mu
    r = zs[:, 0:D_A]
    k = zs[:, D_A:2 * D_A]
    v = zs[:, 2 * D_A:3 * D_A]
    lora_in = zs[:, 3 * D_A:N_SHIFT]
    wq = w0 + jnp.dot(jnp.tanh(lora_in).astype(BF16), wup_pad, preferred_element_type=F32)
    w = -_softplus(-wq) - 0.5
    lw = -jnp.exp(w)
    a = _sigmoid(a0 + jnp.dot(lora_in.astype(BF16), aup_pad, preferred_element_type=F32))
    if vres is not None:
        vgate, vfirst = vres
        v = v + (vfirst - v) * vgate
    kk = k * k_k
    k2 = k * (1.0 + (a - 1.0) * k_a)
    ss, rk_sum = _headsums([kk * kk, r * k2 * r_k], ones_g)
    kk = kk / jnp.maximum(jnp.sqrt(ss), 1e-12)
    return r, k2, v, -kk, kk * a, lw, rk_sum * v


def _chunk_groups(chains, masks):
    strict, incl, eye, bdmask = masks
    C = chains[0][0].shape[0]
    n = range(len(chains))
    At, Rt, Bt, Kt, Bh, Kh, V, S, wcc = (list(col) for col in zip(*chains))

    def bd(p):
        return tuple(jnp.where(bdmask, jnp.concatenate([q] * HEADS_PER_GROUP, axis=0), 0) for q in p)

    def stack(a, b):
        return tuple(jnp.concatenate([x, y], axis=0) for x, y in zip(a, b))

    ar = [_split(jnp.concatenate([At[i], Rt[i]], axis=0)) for i in n]
    sc_b = [_dot3(ar[i], bd(_split(Bt[i])), NT) for i in n]
    sc_k = [_dot3(ar[i], bd(_split(Kt[i])), NT) for i in n]
    N = [jnp.where(strict, sc_b[i][:C], 0.0) for i in n]
    Srb = [jnp.where(incl, sc_b[i][C:], 0.0) for i in n]
    M = [jnp.where(strict, sc_k[i][:C], 0.0) for i in n]
    Srk = [jnp.where(incl, sc_k[i][C:], 0.0) for i in n]
    ar_s = [_dot3(ar[i], _split(S[i]), NT) for i in n]
    mv = [_dot3(_split(jnp.concatenate([M[i], Srk[i]], axis=0)), bd(_split(V[i])), NN) for i in n]
    X = [ar_s[i][:C] + mv[i][:C] for i in n]

    T = [jnp.where(eye, 1.0, 0.0) + N[i] for i in n]
    p_s = [_split(N[i]) for i in n]
    P = [_dot3(p_s[i], bd(p_s[i]), NN) for i in n]
    n_stage = C.bit_length() - 2
    for k in range(1, n_stage + 1):
        p_s = [_split(P[i]) for i in n]
        p_bd = [bd(p_s[i]) for i in n]
        t_s = [_split(T[i]) for i in n]
        if k < n_stage:
            both = [_dot3(stack(t_s[i], p_s[i]), p_bd[i], NN) for i in n]
            T = [T[i] + both[i][:C] for i in n]
            P = [both[i][C:] for i in n]
        else:
            T = [T[i] + _dot3(t_s[i], p_bd[i], NN) for i in n]

    U = [_dot3(_split(T[i]), bd(_split(X[i])), NN) for i in n]
    Y = [ar_s[i][C:] + mv[i][C:] + _dot3(_split(Srb[i]), bd(_split(U[i])), NN) for i in n]
    upd = [_dot3(_split(jnp.concatenate([U[i], V[i]], axis=0)),
                 _split(jnp.concatenate([Bh[i], Kh[i]], axis=0)), TN) for i in n]
    return [(Y[i], S[i] * wcc[i] + jnp.where(bdmask, upd[i], 0.0)) for i in n]


def _wkv_chunk_kernel(*refs, bb, has_vres):
    if has_vres:
        (z_ref, vg_ref, vf_ref, s0_ref, sh0_ref, mu_ref, w0_ref, wup_ref, a0_ref, aup_ref,
         kk_ref, ka_ref, rk_ref, gng_ref, gnb_ref, y_ref, v_ref, s_ref, zp_scr) = refs
    else:
        (z_ref, s0_ref, sh0_ref, mu_ref, w0_ref, wup_ref, a0_ref, aup_ref,
         kk_ref, ka_ref, rk_ref, gng_ref, gnb_ref, y_ref, v_ref, s_ref, zp_scr) = refs
    C = z_ref.shape[1]
    c = pl.program_id(1)

    @pl.when(c == 0)
    def _():
        s_ref[...] = s0_ref[...]
        zp_scr[...] = sh0_ref[...]

    ones_g = _head_ones(GROUP_W)
    t_idx = lax.broadcasted_iota(jnp.int32, (C, GROUP_W), 0)
    s_idx = lax.broadcasted_iota(jnp.int32, (C, GROUP_W), 1) % C
    rr = lax.broadcasted_iota(jnp.int32, (GROUP_W, GROUP_W), 0) // HEAD
    cc = lax.broadcasted_iota(jnp.int32, (GROUP_W, GROUP_W), 1) // HEAD
    masks = (s_idx < t_idx, s_idx <= t_idx, s_idx == t_idx, rr == cc)
    tri = jnp.where(lax.broadcasted_iota(jnp.int32, (C, C), 1)
                    <= lax.broadcasted_iota(jnp.int32, (C, C), 0), 1.0, 0.0).astype(BF16)
    row0 = lax.broadcasted_iota(jnp.int32, (C, N_SHIFT), 0) == 0

    chains, bonuses = [], []
    for bi in range(bb):
        z = z_ref[bi]
        z_prev = jnp.where(row0, zp_scr[bi], pltpu.roll(z, 1, 0))
        zp_scr[bi] = z[C - 1:C, :]
        vres = (vg_ref[bi], vf_ref[bi]) if has_vres else None
        r, k2, v, aa, bbv, lw, bonus = _timemix_pre(
            z, z_prev, mu_ref[...], w0_ref[...], wup_ref[...], a0_ref[...], aup_ref[...],
            kk_ref[...], ka_ref[...], rk_ref[...], vres, ones_g)
        v_ref[bi] = v
        bonuses.append(bonus)

        p1 = lw.astype(BF16)
        r1 = lw - p1.astype(F32)
        p2 = r1.astype(BF16)
        p3 = (r1 - p2.astype(F32)).astype(BF16)
        cum = _dg(tri, p1, NN) + (_dg(tri, p2, NN) + _dg(tri, p3, NN))
        cum_c = cum[C - 1:C, :]
        wc = jnp.exp(cum)
        wi = jnp.exp(-cum)
        wrel = jnp.exp(cum_c - cum)
        At = aa * jnp.exp(cum - lw)
        Rt = r * wc
        Bt = bbv * wi
        Kt = k2 * wi
        Bh = bbv * wrel
        Kh = k2 * wrel
        wcc = jnp.exp(cum_c)

        for g in range(N_HGROUPS):
            sl = slice(g * GROUP_W, (g + 1) * GROUP_W)
            chains.append((At[:, sl], Rt[:, sl], Bt[:, sl], Kt[:, sl], Bh[:, sl], Kh[:, sl],
                           v[:, sl], s_ref[bi, g], wcc[:, sl]))

    res = _chunk_groups(chains, masks)

    for bi in range(bb):
        for g in range(N_HGROUPS):
            s_ref[bi, g] = res[bi * N_HGROUPS + g][1]
        y = jnp.concatenate([res[bi * N_HGROUPS + g][0] for g in range(N_HGROUPS)], axis=1)
        d = y - _headsums([y], ones_g)[0] * (1.0 / HEAD)
        var = _headsums([d * d], ones_g)[0] * (1.0 / HEAD)
        y_ref[bi] = d * lax.rsqrt(var + GN_EPS) * gng_ref[...] + gnb_ref[...] + bonuses[bi]


def _wkv_chunk_call(z3, vres, s0_bd, shift0, p, bb):
    B, T, _ = z3.shape
    C = WKV_CHUNK
    has_vres = vres is not None

    def tok(w):
        return pl.BlockSpec((bb, C, w), lambda b, c: (b, c, 0))

    def full(shape):
        return pl.BlockSpec(shape, lambda b, c: (0,) * len(shape))

    s_spec = pl.BlockSpec((bb, N_HGROUPS, GROUP_W, GROUP_W), lambda b, c: (b, 0, 0, 0))
    in_specs = [tok(N_SHIFT)]
    args = [z3]
    if has_vres:
        in_specs += [tok(D_A), tok(D_A)]
        args += list(vres)
    in_specs += [s_spec, pl.BlockSpec((bb, 1, N_SHIFT), lambda b, c: (b, 0, 0))]
    args += [s0_bd, shift0]
    for name in ("mu", "w0", "wup", "a0", "aup", "k_k", "k_a", "r_k", "gn_g", "gn_b"):
        in_specs.append(full(p[name].shape))
        args.append(p[name])
    return pl.pallas_call(
        functools.partial(_wkv_chunk_kernel, bb=bb, has_vres=has_vres),
        grid=(B // bb, T // C),
        in_specs=in_specs,
        out_specs=[tok(D_A), tok(D_A), s_spec],
        out_shape=[jax.ShapeDtypeStruct((B, T, D_A), F32), jax.ShapeDtypeStruct((B, T, D_A), F32),
                   jax.ShapeDtypeStruct(s0_bd.shape, F32)],
        scratch_shapes=[pltpu.VMEM((bb, 1, N_SHIFT), F32)],
        compiler_params=pltpu.CompilerParams(
            dimension_semantics=("arbitrary", "arbitrary"), vmem_limit_bytes=VMEM_LIMIT_BYTES),
        name="wkv_chunk_vres" if has_vres else "wkv_chunk",
    )(*args)


def _wkv_pre_kernel(*refs, has_vres):
    if has_vres:
        (z_ref, zp_ref, vg_ref, vf_ref, mu_ref, w0_ref, wup_ref, a0_ref, aup_ref, kk_ref, ka_ref, rk_ref,
         r_o, w_o, k_o, v_o, a_o, b_o, bonus_o) = refs
        vres = (vg_ref[...], vf_ref[...])
    else:
        (z_ref, zp_ref, mu_ref, w0_ref, wup_ref, a0_ref, aup_ref, kk_ref, ka_ref, rk_ref,
         r_o, w_o, k_o, v_o, a_o, b_o, bonus_o) = refs
        vres = None
    r, k2, v, aa, bbv, lw, bonus = _timemix_pre(
        z_ref[...], zp_ref[...], mu_ref[...], w0_ref[...], wup_ref[...], a0_ref[...], aup_ref[...],
        kk_ref[...], ka_ref[...], rk_ref[...], vres, _head_ones(GROUP_W))
    r_o[...] = r
    w_o[...] = jnp.exp(lw)
    k_o[...] = k2
    v_o[...] = v
    a_o[...] = aa
    b_o[...] = bbv
    bonus_o[...] = bonus


def _wkv_pre_call(z, z_prev, vres, p):
    n = z.shape[0]
    has_vres = vres is not None
    args = [z, z_prev] + (list(vres) if has_vres else [])
    args += [p[k] for k in ("mu", "w0", "wup", "a0", "aup", "k_k", "k_a", "r_k")]
    return pl.pallas_call(
        functools.partial(_wkv_pre_kernel, has_vres=has_vres),
        out_shape=[jax.ShapeDtypeStruct((n, D_A), F32)] * 7,
        compiler_params=pltpu.CompilerParams(vmem_limit_bytes=VMEM_LIMIT_BYTES),
        name="wkv_pre_vres" if has_vres else "wkv_pre",
    )(*args)


def _wkv_step_kernel(s_ref, r_ref, w_ref, k_ref, v_ref, a_ref, b_ref, bonus_ref, gng_ref, gnb_ref,
                     y_ref, so_ref):
    S = s_ref[...]
    eye = (lax.broadcasted_iota(jnp.int32, (HEAD, HEAD), 0)
           == lax.broadcasted_iota(jnp.int32, (HEAD, HEAD), 1))
    sa = jnp.sum(S * a_ref[...], axis=-1, keepdims=True)
    vcol = jnp.sum(jnp.where(eye, v_ref[...], 0.0), axis=-1, keepdims=True)
    S2 = S * w_ref[...] + sa * b_ref[...] + vcol * k_ref[...]
    so_ref[...] = S2
    ycol = jnp.sum(S2 * r_ref[...], axis=-1, keepdims=True)
    y = jnp.sum(jnp.where(eye, ycol, 0.0), axis=-2, keepdims=True)
    m = jnp.mean(y, axis=-1, keepdims=True)
    d = y - m
    var = jnp.mean(d * d, axis=-1, keepdims=True)
    y_ref[...] = d * lax.rsqrt(var + GN_EPS) * gng_ref[...] + gnb_ref[...] + bonus_ref[...]


def _wkv_step_call(S, vecs, gng_t, gnb_t, tb):
    n = S.shape[0]
    vec_spec = pl.BlockSpec((tb, 1, HEAD), lambda i: (i, 0, 0))
    s_spec = pl.BlockSpec((tb, HEAD, HEAD), lambda i: (i, 0, 0))
    full = pl.BlockSpec((tb, 1, HEAD), lambda i: (0, 0, 0))
    return pl.pallas_call(
        _wkv_step_kernel,
        grid=(n // tb,),
        in_specs=[s_spec] + [vec_spec] * 7 + [full, full],
        out_specs=[vec_spec, s_spec],
        out_shape=[jax.ShapeDtypeStruct((n, 1, HEAD), F32), jax.ShapeDtypeStruct(S.shape, F32)],
        compiler_params=pltpu.CompilerParams(
            dimension_semantics=("arbitrary",), vmem_limit_bytes=VMEM_LIMIT_BYTES),
        name="wkv_step",
    )(S, *vecs, gng_t, gnb_t)


def _merge_kernel(*refs, chunked, final):
    (h_ref, ya_ref, ga_ref, u_ref, vn_ref, gb_ref, ma_ref, mb_ref, p_ref,
     sp_a_ref, sp_b_ref, wba_ref, wbb_ref, wout_ref, wple_ref, wpg_ref, bpg_ref) = refs[:17]
    rest = refs[17:]
    if final:
        fg_ref, h_out, y_out = rest
    else:
        (h_out,) = rest
    vn = vn_ref[...]
    u = u_ref[...]
    if chunked:
        tm = vn.shape[0]
        tril = (lax.broadcasted_iota(jnp.int32, (SPATIAL_CHUNK, SPATIAL_CHUNK), 1)
                <= lax.broadcasted_iota(jnp.int32, (SPATIAL_CHUNK, SPATIAL_CHUNK), 0))
        vb16 = vn.astype(BF16)
        rows_out = []
        for ci in range(tm // SPATIAL_CHUNK):
            rs = slice(ci * SPATIAL_CHUNK, (ci + 1) * SPATIAL_CHUNK)
            cols = []
            for g in range(N_GROUPS_B):
                wm = jnp.where(tril, sp_a_ref[g], 0.0).astype(BF16)
                mixed = jnp.dot(wm, vb16[rs, g * GROUP_B:(g + 1) * GROUP_B], preferred_element_type=F32)
                cols.append(mixed + sp_b_ref[:, g:g + 1])
            rows_out.append(jnp.concatenate(cols, axis=1))
        mixed = jnp.concatenate(rows_out, axis=0)
    else:
        mixed = vn * sp_a_ref[...] + sp_b_ref[...]
    yb = u * mixed
    br_a = jnp.dot((ya_ref[...] * ga_ref[...]).astype(BF16), wba_ref[...], preferred_element_type=F32)
    br_b = jnp.dot((yb * gb_ref[...]).astype(BF16), wbb_ref[...], preferred_element_type=F32)
    merged = ma_ref[...] * br_a + mb_ref[...] * br_b
    h = h_ref[...] + jnp.dot(merged.astype(BF16), wout_ref[...], preferred_element_type=F32)
    ple = jnp.dot(p_ref[...].astype(BF16), wple_ref[...], preferred_element_type=F32)
    gate = _sigmoid(jnp.dot(h.astype(BF16), wpg_ref[...], preferred_element_type=F32) + bpg_ref[...])
    h = h + gate * ple
    h_out[...] = h
    if final:
        y_out[...] = h * lax.rsqrt(jnp.mean(h * h, axis=-1, keepdims=True) + EPS) * fg_ref[...]


def _merge_call(h, ya, ga, u, vn, gb, ma, mb, pe, sp_a, sp_b, wl, final_g, chunked, tm):
    n = h.shape[0]
    final = final_g is not None

    def rows(w):
        return pl.BlockSpec((tm, w), lambda i: (i, 0))

    def full(a):
        return pl.BlockSpec(a.shape, lambda i: (0,) * a.ndim)

    row_args = [h, ya, ga, u, vn, gb, ma, mb, pe]
    w_args = [sp_a, sp_b, wl["w_br_a"], wl["w_br_b"], wl["w_out"], wl["w_ple"], wl["w_pg"], wl["b_pg"]]
    if final:
        w_args.append(final_g)
    n_out = 2 if final else 1
    return pl.pallas_call(
        functools.partial(_merge_kernel, chunked=chunked, final=final),
        grid=(n // tm,),
        in_specs=[rows(a.shape[1]) for a in row_args] + [full(a) for a in w_args],
        out_specs=[rows(D_MODEL)] * n_out,
        out_shape=[jax.ShapeDtypeStruct((n, D_MODEL), F32)] * n_out,
        compiler_params=pltpu.CompilerParams(
            dimension_semantics=("arbitrary",), vmem_limit_bytes=VMEM_LIMIT_BYTES),
        name="merge_" + ("c" if chunked else "s") + ("f" if final else ""),
    )(*row_args, *w_args)


def _layer_params(W, l):
    row = lambda a: a.reshape(1, -1)
    zpad = jnp.zeros((D_LORA, D_A), F32)
    p = dict(
        norm_g=row(W["norm_g"][l]), w_in=W["w_in"][l].astype(BF16),
        ln_g=row(W["ln_v_g"][l]), ln_b=row(W["ln_v_b"][l]),
        mu=row(W["shift_mu"][l]), w0=row(W["w0"][l]), a0=row(W["a0"][l]),
        wup=jnp.concatenate([W["w_up"][l], zpad], axis=0).astype(BF16),
        aup=jnp.concatenate([zpad, W["a_up"][l]], axis=0).astype(BF16),
        k_k=row(W["k_k"][l]), k_a=row(W["k_a"][l]), r_k=row(W["r_k"][l]),
        gn_g=row(W["gn_g"][l]), gn_b=row(W["gn_b"][l]),
        w_br_a=W["w_br_a"][l].astype(BF16), w_br_b=W["w_br_b"][l].astype(BF16),
        w_out=W["w_out"][l].astype(BF16), w_ple=W["w_ple"][l].astype(BF16),
        w_pg=W["w_ple_gate"][l].astype(BF16), b_pg=row(W["b_ple_gate"][l]),
        sp_w=W["w_spatial"][l], sp_bT=W["b_spatial"][l].T,
        sp_w0=jnp.repeat(W["w_spatial"][l][:, 0, 0], GROUP_B).reshape(1, D_B),
        sp_b0=jnp.repeat(W["b_spatial"][l][:, 0], GROUP_B).reshape(1, D_B),
    )
    if l > 0:
        p["vres"] = (W["vres_down"][l - 1].astype(BF16), W["vres_up"][l - 1].astype(BF16),
                     row(W["vres_b"][l - 1]))
    else:
        p["vres"] = None
    return p


def _diag_blocks(s_bd):
    B = s_bd.shape[0]
    s = s_bd.reshape(B, N_HGROUPS, HEADS_PER_GROUP, HEAD, HEADS_PER_GROUP, HEAD)
    idx = jnp.arange(HEADS_PER_GROUP)
    s = s[:, :, idx, :, idx, :]
    return jnp.moveaxis(s, 0, 2).reshape(B, N_HEADS, HEAD, HEAD)


def _forward_prompt(x, pe, W, params):
    B, T, _ = x.shape
    n = B * T
    h = x.reshape(n, D_MODEL)
    v_first = None
    wkvs, shifts = [], []
    depth = len(params)
    for l, p in enumerate(params):
        outs = _proj_call(h, p["norm_g"], p["w_in"], p["ln_g"], p["ln_b"], p["vres"], tm=256)
        z, ga, u, vn, gb, ma, mb = outs[:7]
        z3 = z.reshape(B, T, N_SHIFT)
        vres = None if l == 0 else (outs[7].reshape(B, T, D_A), v_first)
        s0 = jnp.zeros((B, N_HGROUPS, GROUP_W, GROUP_W), F32)
        shift0 = jnp.zeros((B, 1, N_SHIFT), F32)
        ya, v, s_bd = _wkv_chunk_call(z3, vres, s0, shift0, p, bb=4)
        if l == 0:
            v_first = v
        wkvs.append(_diag_blocks(s_bd))
        shifts.append(z3[:, T - 1, :])
        fg = W["final_g"].reshape(1, -1) if l == depth - 1 else None
        res = _merge_call(h, ya.reshape(n, D_A), ga, u, vn, gb, ma, mb, pe[l].reshape(n, D_PLE),
                          p["sp_w"], p["sp_bT"], p, fg, chunked=True, tm=256)
        h = res[0]
    y = res[1].reshape(B, T, D_MODEL)
    return y, jnp.stack(wkvs), jnp.stack(shifts)


def _forward_sample(x, pe, wkv0, shift0, W, params):
    B = x.shape[0]
    h = x.reshape(B, D_MODEL)
    v_first = None
    wkvs, shifts, chunk_vs = [], [], []
    depth = len(params)
    tb = 64
    for l, p in enumerate(params):
        outs = _proj_call(h, p["norm_g"], p["w_in"], p["ln_g"], p["ln_b"], p["vres"], tm=B)
        z, ga, u, vn, gb, ma, mb = outs[:7]
        vres = None if l == 0 else (outs[7], v_first)
        r, w, k2, v, aa, bbv, bonus = _wkv_pre_call(z, shift0[l], vres, p)
        if l == 0:
            v_first = v
        vecs = [a.reshape(B * N_HEADS, 1, HEAD) for a in (r, w, k2, v, aa, bbv, bonus)]
        gng_t = jnp.tile(p["gn_g"].reshape(N_HEADS, 1, HEAD), (tb // N_HEADS, 1, 1))
        gnb_t = jnp.tile(p["gn_b"].reshape(N_HEADS, 1, HEAD), (tb // N_HEADS, 1, 1))
        ya, s_new = _wkv_step_call(wkv0[l].reshape(B * N_HEADS, HEAD, HEAD), vecs, gng_t, gnb_t, tb)
        wkvs.append(s_new.reshape(B, N_HEADS, HEAD, HEAD))
        shifts.append(z)
        chunk_vs.append(vn.reshape(B, 1, D_B))
        fg = W["final_g"].reshape(1, -1) if l == depth - 1 else None
        res = _merge_call(h, ya.reshape(B, D_A), ga, u, vn, gb, ma, mb, pe[l].reshape(B, D_PLE),
                          p["sp_w0"], p["sp_b0"], p, fg, chunked=False, tm=B)
        h = res[0]
    y = res[1].reshape(B, 1, D_MODEL)
    return y, jnp.stack(wkvs), jnp.stack(shifts), jnp.stack(chunk_vs)


def kernel(x_prompt, x_sample, state_rwkv_wkv, state_rwkv_shift, p_prompt, p_sample, norm_g, w_in, shift_mu, w0, w_up, a0, a_up, vres_down, vres_up, vres_b, k_k, k_a, r_k, gn_g, gn_b, ln_v_g, ln_v_b, w_spatial, b_spatial, w_br_a, w_br_b, w_out, w_ple, w_ple_gate, b_ple_gate, final_g):
    W = dict(norm_g=norm_g, w_in=w_in, shift_mu=shift_mu, w0=w0, w_up=w_up, a0=a0, a_up=a_up,
             vres_down=vres_down, vres_up=vres_up, vres_b=vres_b, k_k=k_k, k_a=k_a, r_k=r_k,
             gn_g=gn_g, gn_b=gn_b, ln_v_g=ln_v_g, ln_v_b=ln_v_b, w_spatial=w_spatial,
             b_spatial=b_spatial, w_br_a=w_br_a, w_br_b=w_br_b, w_out=w_out, w_ple=w_ple,
             w_ple_gate=w_ple_gate, b_ple_gate=b_ple_gate, final_g=final_g)
    params = [_layer_params(W, l) for l in range(w_in.shape[0])]
    y_p, wkv_p, shift_p = _forward_prompt(x_prompt, p_prompt, W, params)
    y_s, wkv_s, shift_s, chunk_v = _forward_sample(
        x_sample, p_sample, state_rwkv_wkv, state_rwkv_shift, W, params)
    return (y_p, y_s, wkv_p, shift_p, wkv_s, shift_s, chunk_v)
```

```python
import functools

import jax
import jax.numpy as jnp
from jax import lax
from jax.experimental import pallas as pl
from jax.experimental.pallas import tpu as pltpu

D_MODEL = 1024
HEAD = 64
N_HEADS = 8
D_A = N_HEADS * HEAD
D_LORA = 64
D_B = 512
N_GROUPS_B = 4
GROUP_B = D_B // N_GROUPS_B
SPATIAL_CHUNK = 128
D_PLE = 256
N_SHIFT = 3 * D_A + 2 * D_LORA
D_IN = N_SHIFT + D_A + 3 * D_B + 2 * D_MODEL
EPS = 1e-6
GN_EPS = 64e-5
LN_EPS = 1e-5

O_GA = N_SHIFT
O_U = O_GA + D_A
O_VB = O_U + D_B
O_GB = O_VB + D_B
O_MA = O_GB + D_B
O_MB = O_MA + D_MODEL

WKV_CHUNK = 64
HEADS_PER_GROUP = 4
GROUP_W = HEADS_PER_GROUP * HEAD
N_HGROUPS = N_HEADS // HEADS_PER_GROUP
VMEM_LIMIT_BYTES = 56 * 1024 * 1024

F32 = jnp.float32
BF16 = jnp.bfloat16

NN = ((1,), (0,))
NT = ((1,), (1,))
TN = ((0,), (0,))


def _dg(a, b, dims):
    return lax.dot_general(a, b, (dims, ((), ())), preferred_element_type=F32)


def _split(x):
    hi = x.astype(BF16)
    lo = (x - hi.astype(F32)).astype(BF16)
    return hi, lo


def _sigmoid(x):
    return 1.0 / (1.0 + jnp.exp(-x))


def _gelu_tanh(x):
    return 0.5 * x * (1.0 + jnp.tanh(0.7978845608028654 * (x + 0.044715 * (x * x * x))))


def _softplus(x):
    return jnp.maximum(x, 0.0) + jnp.log(1.0 + jnp.exp(-jnp.abs(x)))


def _head_ones(n):
    r = lax.broadcasted_iota(jnp.int32, (n, n), 0) // HEAD
    c = lax.broadcasted_iota(jnp.int32, (n, n), 1) // HEAD
    return jnp.where(r == c, 1.0, 0.0).astype(BF16)


def _headsums(xs, ones_g):
    rows = xs[0].shape[0]
    pieces = []
    for x in xs:
        hi, lo = _split(x)
        pieces += [hi[:, :GROUP_W], lo[:, :GROUP_W], hi[:, GROUP_W:], lo[:, GROUP_W:]]
    o = _dg(jnp.concatenate(pieces, axis=0), ones_g, NN)
    outs = []
    for i in range(len(xs)):
        q = [o[(4 * i + j) * rows:(4 * i + j + 1) * rows] for j in range(4)]
        outs.append(jnp.concatenate([q[0] + q[1], q[2] + q[3]], axis=1))
    return outs


def _proj_kernel(*refs, has_vres):
    if has_vres:
        (x_ref, g_ref, w_ref, lng_ref, lnb_ref, vd_ref, vu_ref, vb_ref,
         z_ref, ga_ref, u_ref, vn_ref, gb_ref, ma_ref, mb_ref, vg_ref) = refs
    else:
        (x_ref, g_ref, w_ref, lng_ref, lnb_ref,
         z_ref, ga_ref, u_ref, vn_ref, gb_ref, ma_ref, mb_ref) = refs
    x = x_ref[...]
    xn = x * lax.rsqrt(jnp.mean(x * x, axis=-1, keepdims=True) + EPS) * g_ref[...]
    xb = xn.astype(BF16)

    def seg(lo, hi):
        return jnp.dot(xb, w_ref[:, lo:hi], preferred_element_type=F32)

    z_ref[...] = seg(0, N_SHIFT)
    t = seg(O_GA, O_U)
    ga_ref[...] = t * _sigmoid(t)
    u_ref[...] = _gelu_tanh(seg(O_U, O_VB))
    vb = _gelu_tanh(seg(O_VB, O_GB))
    mu = jnp.mean(vb, axis=-1, keepdims=True)
    d = vb - mu
    var = jnp.mean(d * d, axis=-1, keepdims=True)
    vn_ref[...] = d * lax.rsqrt(var + LN_EPS) * lng_ref[...] + lnb_ref[...]
    t = seg(O_GB, O_MA)
    gb_ref[...] = t * _sigmoid(t)
    ma_ref[...] = _sigmoid(seg(O_MA, O_MB))
    mb_ref[...] = _sigmoid(seg(O_MB, D_IN))
    if has_vres:
        low = jnp.dot(xb, vd_ref[...], preferred_element_type=F32)
        up = jnp.dot(low.astype(BF16), vu_ref[...], preferred_element_type=F32)
        vg_ref[...] = _sigmoid(vb_ref[...] + up)


def _proj_call(x2d, norm_g, w_in_bf, ln_g, ln_b, vres, tm):
    n = x2d.shape[0]
    has_vres = vres is not None

    def full(shape):
        return pl.BlockSpec(shape, lambda i: (0,) * len(shape))

    def rows(w):
        return pl.BlockSpec((tm, w), lambda i: (i, 0))

    in_specs = [rows(D_MODEL), full((1, D_MODEL)), full((D_MODEL, D_IN)), full((1, D_B)), full((1, D_B))]
    args = [x2d, norm_g, w_in_bf, ln_g, ln_b]
    widths = [N_SHIFT, D_A, D_B, D_B, D_B, D_MODEL, D_MODEL]
    if has_vres:
        vd, vu, vb = vres
        in_specs += [full(vd.shape), full(vu.shape), full((1, D_A))]
        args += [vd, vu, vb]
        widths.append(D_A)
    return pl.pallas_call(
        functools.partial(_proj_kernel, has_vres=has_vres),
        grid=(n // tm,),
        in_specs=in_specs,
        out_specs=[rows(w) for w in widths],
        out_shape=[jax.ShapeDtypeStruct((n, w), F32) for w in widths],
        compiler_params=pltpu.CompilerParams(
            dimension_semantics=("arbitrary",), vmem_limit_bytes=VMEM_LIMIT_BYTES),
        name="proj_vres" if has_vres else "proj",
    )(*args)


def _timemix_pre(z, z_prev, mu, w0, wup_pad, a0, aup_pad, k_k, k_a, r_k, vres, ones_g):
    zs = z + (z_prev - z) * mu
    r = zs[:, 0:D_A]
    k = zs[:, D_A:2 * D_A]
    v = zs[:, 2 * D_A:3 * D_A]
    lora_in = zs[:, 3 * D_A:N_SHIFT]
    wq = w0 + jnp.dot(jnp.tanh(lora_in).astype(BF16), wup_pad, preferred_element_type=F32)
    w = -_softplus(-wq) - 0.5
    lw = -jnp.exp(w)
    a = _sigmoid(a0 + jnp.dot(lora_in.astype(BF16), aup_pad, preferred_element_type=F32))
    if vres is not None:
        vgate, vfirst = vres
        v = v + (vfirst - v) * vgate
    kk = k * k_k
    k2 = k * (1.0 + (a - 1.0) * k_a)
    ss, rk_sum = _headsums([kk * kk, r * k2 * r_k], ones_g)
    kk = kk / jnp.maximum(jnp.sqrt(ss), 1e-12)
    return r, k2, v, -kk, kk * a, lw, rk_sum * v


def _chunk_groups(chains, masks):
    t_idx, s_idx, bdmask = masks
    C = chains[0][0].shape[0]
    n = range(len(chains))
    At, Rt, Bt, Kt, Bh, Kh, V, S, wcc = (list(col) for col in zip(*chains))
    strict = s_idx < t_idx
    incl = s_idx <= t_idx

    def b16(x):
        return x.astype(BF16)

    def bd(x):
        return jnp.where(bdmask, jnp.concatenate([x] * HEADS_PER_GROUP, axis=0), 0)

    ar = [b16(jnp.concatenate([At[i], Rt[i]], axis=0)) for i in n]
    sc_b = [_dg(ar[i], bd(b16(Bt[i])), NT) for i in n]
    sc_k = [_dg(ar[i], bd(b16(Kt[i])), NT) for i in n]
    N = [jnp.where(strict, sc_b[i][:C], 0.0) for i in n]
    Srb = [jnp.where(incl, sc_b[i][C:], 0.0) for i in n]
    M = [jnp.where(strict, sc_k[i][:C], 0.0) for i in n]
    Srk = [jnp.where(incl, sc_k[i][C:], 0.0) for i in n]
    ar_s = [_dg(ar[i], b16(S[i]), NT) for i in n]
    mv = [_dg(b16(jnp.concatenate([M[i], Srk[i]], axis=0)), bd(b16(V[i])), NN) for i in n]
    X = [ar_s[i][:C] + mv[i][:C] for i in n]

    def level_mask(b):
        return ((t_idx // (2 * b)) == (s_idx // (2 * b))) & ((t_idx % (2 * b)) >= b) & ((s_idx % (2 * b)) < b)

    T = [jnp.where(s_idx == t_idx, 1.0, 0.0) + jnp.where(level_mask(1), N[i], 0.0) for i in n]
    b = 2
    while b < C:
        lm = level_mask(b)
        d = [b16(T[i]) for i in n]
        E = [_dg(b16(jnp.where(lm, N[i], 0.0)), bd(d[i]), NN) for i in n]
        F = [_dg(d[i], bd(b16(E[i])), NN) for i in n]
        T = [T[i] + F[i] for i in n]
        b *= 2

    U = [_dg(b16(T[i]), bd(b16(X[i])), NN) for i in n]
    Y = [ar_s[i][C:] + mv[i][C:] + _dg(b16(Srb[i]), bd(b16(U[i])), NN) for i in n]
    upd = [_dg(b16(jnp.concatenate([U[i], V[i]], axis=0)),
               b16(jnp.concatenate([Bh[i], Kh[i]], axis=0)), TN) for i in n]
    return [(Y[i], S[i] * wcc[i] + jnp.where(bdmask, upd[i], 0.0)) for i in n]


def _wkv_chunk_kernel(*refs, bb, has_vres):
    if has_vres:
        (z_ref, vg_ref, vf_ref, s0_ref, sh0_ref, mu_ref, w0_ref, wup_ref, a0_ref, aup_ref,
         kk_ref, ka_ref, rk_ref, gng_ref, gnb_ref, y_ref, v_ref, s_ref, zp_scr) = refs
    else:
        (z_ref, s0_ref, sh0_ref, mu_ref, w0_ref, wup_ref, a0_ref, aup_ref,
         kk_ref, ka_ref, rk_ref, gng_ref, gnb_ref, y_ref, v_ref, s_ref, zp_scr) = refs
    C = z_ref.shape[1]
    c = pl.program_id(1)

    @pl.when(c == 0)
    def _():
        s_ref[...] = s0_ref[...]
        zp_scr[...] = sh0_ref[...]

    ones_g = _head_ones(GROUP_W)
    t_idx = lax.broadcasted_iota(jnp.int32, (C, GROUP_W), 0)
    s_idx = lax.broadcasted_iota(jnp.int32, (C, GROUP_W), 1) % C
    rr = lax.broadcasted_iota(jnp.int32, (GROUP_W, GROUP_W), 0) // HEAD
    cc = lax.broadcasted_iota(jnp.int32, (GROUP_W, GROUP_W), 1) // HEAD
    masks = (t_idx, s_idx, rr == cc)
    tri = jnp.where(lax.broadcasted_iota(jnp.int32, (C, C), 1)
                    <= lax.broadcasted_iota(jnp.int32, (C, C), 0), 1.0, 0.0).astype(BF16)
    row0 = lax.broadcasted_iota(jnp.int32, (C, N_SHIFT), 0) == 0

    chains, bonuses = [], []
    for bi in range(bb):
        z = z_ref[bi]
        z_prev = jnp.where(row0, zp_scr[bi], pltpu.roll(z, 1, 0))
        zp_scr[bi] = z[C - 1:C, :]
        vres = (vg_ref[bi], vf_ref[bi]) if has_vres else None
        r, k2, v, aa, bbv, lw, bonus = _timemix_pre(
            z, z_prev, mu_ref[...], w0_ref[...], wup_ref[...], a0_ref[...], aup_ref[...],
            kk_ref[...], ka_ref[...], rk_ref[...], vres, ones_g)
        v_ref[bi] = v
        bonuses.append(bonus)

        p1 = lw.astype(BF16)
        r1 = lw - p1.astype(F32)
        p2 = r1.astype(BF16)
        p3 = (r1 - p2.astype(F32)).astype(BF16)
        cum = _dg(tri, p1, NN) + (_dg(tri, p2, NN) + _dg(tri, p3, NN))
        cum_c = cum[C - 1:C, :]
        wc = jnp.exp(cum)
        wi = jnp.exp(-cum)
        wrel = jnp.exp(cum_c - cum)
        At = aa * jnp.exp(cum - lw)
        Rt = r * wc
        Bt = bbv * wi
        Kt = k2 * wi
        Bh = bbv * wrel
        Kh = k2 * wrel
        wcc = jnp.exp(cum_c)

        for g in range(N_HGROUPS):
            sl = slice(g * GROUP_W, (g + 1) * GROUP_W)
            chains.append((At[:, sl], Rt[:, sl], Bt[:, sl], Kt[:, sl], Bh[:, sl], Kh[:, sl],
                           v[:, sl], s_ref[bi, g], wcc[:, sl]))

    res = _chunk_groups(chains, masks)

    for bi in range(bb):
        for g in range(N_HGROUPS):
            s_ref[bi, g] = res[bi * N_HGROUPS + g][1]
        y = jnp.concatenate([res[bi * N_HGROUPS + g][0] for g in range(N_HGROUPS)], axis=1)
        d = y - _headsums([y], ones_g)[0] * (1.0 / HEAD)
        var = _headsums([d * d], ones_g)[0] * (1.0 / HEAD)
        y_ref[bi] = d * lax.rsqrt(var + GN_EPS) * gng_ref[...] + gnb_ref[...] + bonuses[bi]


def _wkv_chunk_call(z3, vres, s0_bd, shift0, p, bb):
    B, T, _ = z3.shape
    C = WKV_CHUNK
    has_vres = vres is not None

    def tok(w):
        return pl.BlockSpec((bb, C, w), lambda b, c: (b, c, 0))

    def full(shape):
        return pl.BlockSpec(shape, lambda b, c: (0,) * len(shape))

    s_spec = pl.BlockSpec((bb, N_HGROUPS, GROUP_W, GROUP_W), lambda b, c: (b, 0, 0, 0))
    in_specs = [tok(N_SHIFT)]
    args = [z3]
    if has_vres:
        in_specs += [tok(D_A), tok(D_A)]
        args += list(vres)
    in_specs += [s_spec, pl.BlockSpec((bb, 1, N_SHIFT), lambda b, c: (b, 0, 0))]
    args += [s0_bd, shift0]
    for name in ("mu", "w0", "wup", "a0", "aup", "k_k", "k_a", "r_k", "gn_g", "gn_b"):
        in_specs.append(full(p[name].shape))
        args.append(p[name])
    return pl.pallas_call(
        functools.partial(_wkv_chunk_kernel, bb=bb, has_vres=has_vres),
        grid=(B // bb, T // C),
        in_specs=in_specs,
        out_specs=[tok(D_A), tok(D_A), s_spec],
        out_shape=[jax.ShapeDtypeStruct((B, T, D_A), F32), jax.ShapeDtypeStruct((B, T, D_A), F32),
                   jax.ShapeDtypeStruct(s0_bd.shape, F32)],
        scratch_shapes=[pltpu.VMEM((bb, 1, N_SHIFT), F32)],
        compiler_params=pltpu.CompilerParams(
            dimension_semantics=("arbitrary", "arbitrary"), vmem_limit_bytes=VMEM_LIMIT_BYTES),
        name="wkv_chunk_vres" if has_vres else "wkv_chunk",
    )(*args)


def _wkv_pre_kernel(*refs, has_vres):
    if has_vres:
        (z_ref, zp_ref, vg_ref, vf_ref, mu_ref, w0_ref, wup_ref, a0_ref, aup_ref, kk_ref, ka_ref, rk_ref,
         r_o, w_o, k_o, v_o, a_o, b_o, bonus_o) = refs
        vres = (vg_ref[...], vf_ref[...])
    else:
        (z_ref, zp_ref, mu_ref, w0_ref, wup_ref, a0_ref, aup_ref, kk_ref, ka_ref, rk_ref,
         r_o, w_o, k_o, v_o, a_o, b_o, bonus_o) = refs
        vres = None
    r, k2, v, aa, bbv, lw, bonus = _timemix_pre(
        z_ref[...], zp_ref[...], mu_ref[...], w0_ref[...], wup_ref[...], a0_ref[...], aup_ref[...],
        kk_ref[...], ka_ref[...], rk_ref[...], vres, _head_ones(GROUP_W))
    r_o[...] = r
    w_o[...] = jnp.exp(lw)
    k_o[...] = k2
    v_o[...] = v
    a_o[...] = aa
    b_o[...] = bbv
    bonus_o[...] = bonus


def _wkv_pre_call(z, z_prev, vres, p):
    n = z.shape[0]
    has_vres = vres is not None
    args = [z, z_prev] + (list(vres) if has_vres else [])
    args += [p[k] for k in ("mu", "w0", "wup", "a0", "aup", "k_k", "k_a", "r_k")]
    return pl.pallas_call(
        functools.partial(_wkv_pre_kernel, has_vres=has_vres),
        out_shape=[jax.ShapeDtypeStruct((n, D_A), F32)] * 7,
        compiler_params=pltpu.CompilerParams(vmem_limit_bytes=VMEM_LIMIT_BYTES),
        name="wkv_pre_vres" if has_vres else "wkv_pre",
    )(*args)


def _wkv_step_kernel(s_ref, r_ref, w_ref, k_ref, v_ref, a_ref, b_ref, bonus_ref, gng_ref, gnb_ref,
                     y_ref, so_ref):
    S = s_ref[...]
    eye = (lax.broadcasted_iota(jnp.int32, (HEAD, HEAD), 0)
           == lax.broadcasted_iota(jnp.int32, (HEAD, HEAD), 1))
    sa = jnp.sum(S * a_ref[...], axis=-1, keepdims=True)
    vcol = jnp.sum(jnp.where(eye, v_ref[...], 0.0), axis=-1, keepdims=True)
    S2 = S * w_ref[...] + sa * b_ref[...] + vcol * k_ref[...]
    so_ref[...] = S2
    ycol = jnp.sum(S2 * r_ref[...], axis=-1, keepdims=True)
    y = jnp.sum(jnp.where(eye, ycol, 0.0), axis=-2, keepdims=True)
    m = jnp.mean(y, axis=-1, keepdims=True)
    d = y - m
    var = jnp.mean(d * d, axis=-1, keepdims=True)
    y_ref[...] = d * lax.rsqrt(var + GN_EPS) * gng_ref[...] + gnb_ref[...] + bonus_ref[...]


def _wkv_step_call(S, vecs, gng_t, gnb_t, tb):
    n = S.shape[0]
    vec_spec = pl.BlockSpec((tb, 1, HEAD), lambda i: (i, 0, 0))
    s_spec = pl.BlockSpec((tb, HEAD, HEAD), lambda i: (i, 0, 0))
    full = pl.BlockSpec((tb, 1, HEAD), lambda i: (0, 0, 0))
    return pl.pallas_call(
        _wkv_step_kernel,
        grid=(n // tb,),
        in_specs=[s_spec] + [vec_spec] * 7 + [full, full],
        out_specs=[vec_spec, s_spec],
        out_shape=[jax.ShapeDtypeStruct((n, 1, HEAD), F32), jax.ShapeDtypeStruct(S.shape, F32)],
        compiler_params=pltpu.CompilerParams(
            dimension_semantics=("arbitrary",), vmem_limit_bytes=VMEM_LIMIT_BYTES),
        name="wkv_step",
    )(S, *vecs, gng_t, gnb_t)


def _merge_kernel(*refs, chunked, final):
    (h_ref, ya_ref, ga_ref, u_ref, vn_ref, gb_ref, ma_ref, mb_ref, p_ref,
     sp_a_ref, sp_b_ref, wba_ref, wbb_ref, wout_ref, wple_ref, wpg_ref, bpg_ref) = refs[:17]
    rest = refs[17:]
    if final:
        fg_ref, h_out, y_out = rest
    else:
        (h_out,) = rest
    vn = vn_ref[...]
    u = u_ref[...]
    if chunked:
        tm = vn.shape[0]
        tril = (lax.broadcasted_iota(jnp.int32, (SPATIAL_CHUNK, SPATIAL_CHUNK), 1)
                <= lax.broadcasted_iota(jnp.int32, (SPATIAL_CHUNK, SPATIAL_CHUNK), 0))
        vb16 = vn.astype(BF16)
        rows_out = []
        for ci in range(tm // SPATIAL_CHUNK):
            rs = slice(ci * SPATIAL_CHUNK, (ci + 1) * SPATIAL_CHUNK)
            cols = []
            for g in range(N_GROUPS_B):
                wm = jnp.where(tril, sp_a_ref[g], 0.0).astype(BF16)
                mixed = jnp.dot(wm, vb16[rs, g * GROUP_B:(g + 1) * GROUP_B], preferred_element_type=F32)
                cols.append(mixed + sp_b_ref[:, g:g + 1])
            rows_out.append(jnp.concatenate(cols, axis=1))
        mixed = jnp.concatenate(rows_out, axis=0)
    else:
        mixed = vn * sp_a_ref[...] + sp_b_ref[...]
    yb = u * mixed
    br_a = jnp.dot((ya_ref[...] * ga_ref[...]).astype(BF16), wba_ref[...], preferred_element_type=F32)
    br_b = jnp.dot((yb * gb_ref[...]).astype(BF16), wbb_ref[...], preferred_element_type=F32)
    merged = ma_ref[...] * br_a + mb_ref[...] * br_b
    h = h_ref[...] + jnp.dot(merged.astype(BF16), wout_ref[...], preferred_element_type=F32)
    ple = jnp.dot(p_ref[...].astype(BF16), wple_ref[...], preferred_element_type=F32)
    gate = _sigmoid(jnp.dot(h.astype(BF16), wpg_ref[...], preferred_element_type=F32) + bpg_ref[...])
    h = h + gate * ple
    h_out[...] = h
    if final:
        y_out[...] = h * lax.rsqrt(jnp.mean(h * h, axis=-1, keepdims=True) + EPS) * fg_ref[...]


def _merge_call(h, ya, ga, u, vn, gb, ma, mb, pe, sp_a, sp_b, wl, final_g, chunked, tm):
    n = h.shape[0]
    final = final_g is not None

    def rows(w):
        return pl.BlockSpec((tm, w), lambda i: (i, 0))

    def full(a):
        return pl.BlockSpec(a.shape, lambda i: (0,) * a.ndim)

    row_args = [h, ya, ga, u, vn, gb, ma, mb, pe]
    w_args = [sp_a, sp_b, wl["w_br_a"], wl["w_br_b"], wl["w_out"], wl["w_ple"], wl["w_pg"], wl["b_pg"]]
    if final:
        w_args.append(final_g)
    n_out = 2 if final else 1
    return pl.pallas_call(
        functools.partial(_merge_kernel, chunked=chunked, final=final),
        grid=(n // tm,),
        in_specs=[rows(a.shape[1]) for a in row_args] + [full(a) for a in w_args],
        out_specs=[rows(D_MODEL)] * n_out,
        out_shape=[jax.ShapeDtypeStruct((n, D_MODEL), F32)] * n_out,
        compiler_params=pltpu.CompilerParams(
            dimension_semantics=("arbitrary",), vmem_limit_bytes=VMEM_LIMIT_BYTES),
        name="merge_" + ("c" if chunked else "s") + ("f" if final else ""),
    )(*row_args, *w_args)


def _layer_params(W, l):
    row = lambda a: a.reshape(1, -1)
    zpad = jnp.zeros((D_LORA, D_A), F32)
    p = dict(
        norm_g=row(W["norm_g"][l]), w_in=W["w_in"][l].astype(BF16),
        ln_g=row(W["ln_v_g"][l]), ln_b=row(W["ln_v_b"][l]),
        mu=row(W["shift_mu"][l]), w0=row(W["w0"][l]), a0=row(W["a0"][l]),
        wup=jnp.concatenate([W["w_up"][l], zpad], axis=0).astype(BF16),
        aup=jnp.concatenate([zpad, W["a_up"][l]], axis=0).astype(BF16),
        k_k=row(W["k_k"][l]), k_a=row(W["k_a"][l]), r_k=row(W["r_k"][l]),
        gn_g=row(W["gn_g"][l]), gn_b=row(W["gn_b"][l]),
        w_br_a=W["w_br_a"][l].astype(BF16), w_br_b=W["w_br_b"][l].astype(BF16),
        w_out=W["w_out"][l].astype(BF16), w_ple=W["w_ple"][l].astype(BF16),
        w_pg=W["w_ple_gate"][l].astype(BF16), b_pg=row(W["b_ple_gate"][l]),
        sp_w=W["w_spatial"][l], sp_bT=W["b_spatial"][l].T,
        sp_w0=jnp.repeat(W["w_spatial"][l][:, 0, 0], GROUP_B).reshape(1, D_B),
        sp_b0=jnp.repeat(W["b_spatial"][l][:, 0], GROUP_B).reshape(1, D_B),
    )
    if l > 0:
        p["vres"] = (W["vres_down"][l - 1].astype(BF16), W["vres_up"][l - 1].astype(BF16),
                     row(W["vres_b"][l - 1]))
    else:
        p["vres"] = None
    return p


def _diag_blocks(s_bd):
    B = s_bd.shape[0]
    s = s_bd.reshape(B, N_HGROUPS, HEADS_PER_GROUP, HEAD, HEADS_PER_GROUP, HEAD)
    idx = jnp.arange(HEADS_PER_GROUP)
    s = s[:, :, idx, :, idx, :]
    return jnp.moveaxis(s, 0, 2).reshape(B, N_HEADS, HEAD, HEAD)


def _forward_prompt(x, pe, W, params):
    B, T, _ = x.shape
    n = B * T
    h = x.reshape(n, D_MODEL)
    v_first = None
    wkvs, shifts = [], []
    depth = len(params)
    for l, p in enumerate(params):
        outs = _proj_call(h, p["norm_g"], p["w_in"], p["ln_g"], p["ln_b"], p["vres"], tm=256)
        z, ga, u, vn, gb, ma, mb = outs[:7]
        z3 = z.reshape(B, T, N_SHIFT)
        vres = None if l == 0 else (outs[7].reshape(B, T, D_A), v_first)
        s0 = jnp.zeros((B, N_HGROUPS, GROUP_W, GROUP_W), F32)
        shift0 = jnp.zeros((B, 1, N_SHIFT), F32)
        ya, v, s_bd = _wkv_chunk_call(z3, vres, s0, shift0, p, bb=4)
        if l == 0:
            v_first = v
        wkvs.append(_diag_blocks(s_bd))
        shifts.append(z3[:, T - 1, :])
        fg = W["final_g"].reshape(1, -1) if l == depth - 1 else None
        res = _merge_call(h, ya.reshape(n, D_A), ga, u, vn, gb, ma, mb, pe[l].reshape(n, D_PLE),
                          p["sp_w"], p["sp_bT"], p, fg, chunked=True, tm=256)
        h = res[0]
    y = res[1].reshape(B, T, D_MODEL)
    return y, jnp.stack(wkvs), jnp.stack(shifts)


def _forward_sample(x, pe, wkv0, shift0, W, params):
    B = x.shape[0]
    h = x.reshape(B, D_MODEL)
    v_first = None
    wkvs, shifts, chunk_vs = [], [], []
    depth = len(params)
    tb = 64
    for l, p in enumerate(params):
        outs = _proj_call(h, p["norm_g"], p["w_in"], p["ln_g"], p["ln_b"], p["vres"], tm=B)
        z, ga, u, vn, gb, ma, mb = outs[:7]
        vres = None if l == 0 else (outs[7], v_first)
        r, w, k2, v, aa, bbv, bonus = _wkv_pre_call(z, shift0[l], vres, p)
        if l == 0:
            v_first = v
        vecs = [a.reshape(B * N_HEADS, 1, HEAD) for a in (r, w, k2, v, aa, bbv, bonus)]
        gng_t = jnp.tile(p["gn_g"].reshape(N_HEADS, 1, HEAD), (tb // N_HEADS, 1, 1))
        gnb_t = jnp.tile(p["gn_b"].reshape(N_HEADS, 1, HEAD), (tb // N_HEADS, 1, 1))
        ya, s_new = _wkv_step_call(wkv0[l].reshape(B * N_HEADS, HEAD, HEAD), vecs, gng_t, gnb_t, tb)
        wkvs.append(s_new.reshape(B, N_HEADS, HEAD, HEAD))
        shifts.append(z)
        chunk_vs.append(vn.reshape(B, 1, D_B))
        fg = W["final_g"].reshape(1, -1) if l == depth - 1 else None
        res = _merge_call(h, ya.reshape(B, D_A), ga, u, vn, gb, ma, mb, pe[l].reshape(B, D_PLE),
                          p["sp_w0"], p["sp_b0"], p, fg, chunked=False, tm=B)
        h = res[0]
    y = res[1].reshape(B, 1, D_MODEL)
    return y, jnp.stack(wkvs), jnp.stack(shifts), jnp.stack(chunk_vs)


def kernel(x_prompt, x_sample, state_rwkv_wkv, state_rwkv_shift, p_prompt, p_sample, norm_g, w_in, shift_mu, w0, w_up, a0, a_up, vres_down, vres_up, vres_b, k_k, k_a, r_k, gn_g, gn_b, ln_v_g, ln_v_b, w_spatial, b_spatial, w_br_a, w_br_b, w_out, w_ple, w_ple_gate, b_ple_gate, final_g):
    W = dict(norm_g=norm_g, w_in=w_in, shift_mu=shift_mu, w0=w0, w_up=w_up, a0=a0, a_up=a_up,
             vres_down=vres_down, vres_up=vres_up, vres_b=vres_b, k_k=k_k, k_a=k_a, r_k=r_k,
             gn_g=gn_g, gn_b=gn_b, ln_v_g=ln_v_g, ln_v_b=ln_v_b, w_spatial=w_spatial,
             b_spatial=b_spatial, w_br_a=w_br_a, w_br_b=w_br_b, w_out=w_out, w_ple=w_ple,
             w_ple_gate=w_ple_gate, b_ple_gate=b_ple_gate, final_g=final_g)
    params = [_layer_params(W, l) for l in range(w_in.shape[0])]
    y_p, wkv_p, shift_p = _forward_prompt(x_prompt, p_prompt, W, params)
    y_s, wkv_s, shift_s, chunk_v = _forward_sample(
        x_sample, p_sample, state_rwkv_wkv, state_rwkv_shift, W, params)
    return (y_p, y_s, wkv_p, shift_p, wkv_s, shift_s, chunk_v)
```

```python
import functools

import jax
import jax.numpy as jnp
from jax import lax
from jax.experimental import pallas as pl
from jax.experimental.pallas import tpu as pltpu

D_MODEL = 1024
HEAD = 64
N_HEADS = 8
D_A = N_HEADS * HEAD
D_LORA = 64
D_B = 512
N_GROUPS_B = 4
GROUP_B = D_B // N_GROUPS_B
SPATIAL_CHUNK = 128
D_PLE = 256
N_SHIFT = 3 * D_A + 2 * D_LORA
D_IN = N_SHIFT + D_A + 3 * D_B + 2 * D_MODEL
EPS = 1e-6
GN_EPS = 64e-5
LN_EPS = 1e-5

O_GA = N_SHIFT
O_U = O_GA + D_A
O_VB = O_U + D_B
O_GB = O_VB + D_B
O_MA = O_GB + D_B
O_MB = O_MA + D_MODEL

WKV_CHUNK = 64
HEADS_PER_GROUP = 4
GROUP_W = HEADS_PER_GROUP * HEAD
N_HGROUPS = N_HEADS // HEADS_PER_GROUP
VMEM_LIMIT_BYTES = 56 * 1024 * 1024

F32 = jnp.float32
BF16 = jnp.bfloat16

NN = ((1,), (0,))
NT = ((1,), (1,))
TN = ((0,), (0,))


def _dg(a, b, dims):
    return lax.dot_general(a, b, (dims, ((), ())), preferred_element_type=F32)


def _split(x):
    hi = x.astype(BF16)
    lo = (x - hi.astype(F32)).astype(BF16)
    return hi, lo


def _sigmoid(x):
    return 1.0 / (1.0 + jnp.exp(-x))


def _gelu_tanh(x):
    return 0.5 * x * (1.0 + jnp.tanh(0.7978845608028654 * (x + 0.044715 * (x * x * x))))


def _softplus(x):
    return jnp.maximum(x, 0.0) + jnp.log(1.0 + jnp.exp(-jnp.abs(x)))


def _head_ones(n):
    r = lax.broadcasted_iota(jnp.int32, (n, n), 0) // HEAD
    c = lax.broadcasted_iota(jnp.int32, (n, n), 1) // HEAD
    return jnp.where(r == c, 1.0, 0.0).astype(BF16)


def _headsums(xs, ones_g):
    rows = xs[0].shape[0]
    pieces = []
    for x in xs:
        hi, lo = _split(x)
        pieces += [hi[:, :GROUP_W], lo[:, :GROUP_W], hi[:, GROUP_W:], lo[:, GROUP_W:]]
    o = _dg(jnp.concatenate(pieces, axis=0), ones_g, NN)
    outs = []
    for i in range(len(xs)):
        q = [o[(4 * i + j) * rows:(4 * i + j + 1) * rows] for j in range(4)]
        outs.append(jnp.concatenate([q[0] + q[1], q[2] + q[3]], axis=1))
    return outs


def _proj_kernel(*refs, has_vres):
    if has_vres:
        (x_ref, g_ref, w_ref, lng_ref, lnb_ref, vd_ref, vu_ref, vb_ref,
         z_ref, ga_ref, u_ref, vn_ref, gb_ref, ma_ref, mb_ref, vg_ref) = refs
    else:
        (x_ref, g_ref, w_ref, lng_ref, lnb_ref,
         z_ref, ga_ref, u_ref, vn_ref, gb_ref, ma_ref, mb_ref) = refs
    x = x_ref[...]
    xn = x * lax.rsqrt(jnp.mean(x * x, axis=-1, keepdims=True) + EPS) * g_ref[...]
    xb = xn.astype(BF16)

    def seg(lo, hi):
        return jnp.dot(xb, w_ref[:, lo:hi], preferred_element_type=F32)

    z_ref[...] = seg(0, N_SHIFT)
    t = seg(O_GA, O_U)
    ga_ref[...] = t * _sigmoid(t)
    u_ref[...] = _gelu_tanh(seg(O_U, O_VB))
    vb = _gelu_tanh(seg(O_VB, O_GB))
    mu = jnp.mean(vb, axis=-1, keepdims=True)
    d = vb - mu
    var = jnp.mean(d * d, axis=-1, keepdims=True)
    vn_ref[...] = d * lax.rsqrt(var + LN_EPS) * lng_ref[...] + lnb_ref[...]
    t = seg(O_GB, O_MA)
    gb_ref[...] = t * _sigmoid(t)
    ma_ref[...] = _sigmoid(seg(O_MA, O_MB))
    mb_ref[...] = _sigmoid(seg(O_MB, D_IN))
    if has_vres:
        low = jnp.dot(xb, vd_ref[...], preferred_element_type=F32)
        up = jnp.dot(low.astype(BF16), vu_ref[...], preferred_element_type=F32)
        vg_ref[...] = _sigmoid(vb_ref[...] + up)


def _proj_call(x2d, norm_g, w_in_all, layer, ln_g, ln_b, vres, tm):
    n = x2d.shape[0]
    has_vres = vres is not None

    def full(shape):
        return pl.BlockSpec(shape, lambda i: (0,) * len(shape), pipeline_mode=pl.Buffered(1))

    def rows(w):
        return pl.BlockSpec((tm, w), lambda i: (i, 0))

    w_spec = pl.BlockSpec((None, D_MODEL, D_IN), lambda i: (layer, 0, 0), pipeline_mode=pl.Buffered(1))
    in_specs = [rows(D_MODEL), full((1, D_MODEL)), w_spec, full((1, D_B)), full((1, D_B))]
    args = [x2d, norm_g, w_in_all, ln_g, ln_b]
    widths = [N_SHIFT, D_A, D_B, D_B, D_B, D_MODEL, D_MODEL]
    if has_vres:
        vd, vu, vb = vres
        in_specs += [full(vd.shape), full(vu.shape), full((1, D_A))]
        args += [vd, vu, vb]
        widths.append(D_A)
    return pl.pallas_call(
        functools.partial(_proj_kernel, has_vres=has_vres),
        grid=(n // tm,),
        in_specs=in_specs,
        out_specs=[rows(w) for w in widths],
        out_shape=[jax.ShapeDtypeStruct((n, w), F32) for w in widths],
        compiler_params=pltpu.CompilerParams(
            dimension_semantics=("arbitrary",), vmem_limit_bytes=VMEM_LIMIT_BYTES),
        name="proj_vres" if has_vres else "proj",
    )(*args)


def _timemix_pre(z, z_prev, mu, w0, wup_pad, a0, aup_pad, k_k, k_a, r_k, vres, ones_g):
    zs = z + (z_prev - z) * mu
    r = zs[:, 0:D_A]
    k = zs[:, D_A:2 * D_A]
    v = zs[:, 2 * D_A:3 * D_A]
    lora_in = zs[:, 3 * D_A:N_SHIFT]
    wq = w0 + jnp.dot(jnp.tanh(lora_in).astype(BF16), wup_pad, preferred_element_type=F32)
    w = -_softplus(-wq) - 0.5
    lw = -jnp.exp(w)
    a = _sigmoid(a0 + jnp.dot(lora_in.astype(BF16), aup_pad, preferred_element_type=F32))
    if vres is not None:
        vgate, vfirst = vres
        v = v + (vfirst - v) * vgate
    kk = k * k_k
    k2 = k * (1.0 + (a - 1.0) * k_a)
    ss, rk_sum = _headsums([kk * kk, r * k2 * r_k], ones_g)
    kk = kk / jnp.maximum(jnp.sqrt(ss), 1e-12)
    return r, k2, v, -kk, kk * a, lw, rk_sum * v


def _chunk_groups(chains, masks):
    t_idx, s_idx, bdmask = masks
    C = chains[0][0].shape[0]
    n = range(len(chains))
    At, Rt, Bt, Kt, Bh, Kh, V, S, wcc = (list(col) for col in zip(*chains))
    strict = s_idx < t_idx
    incl = s_idx <= t_idx

    def b16(x):
        return x.astype(BF16)

    def bd(x):
        return jnp.where(bdmask, jnp.concatenate([x] * HEADS_PER_GROUP, axis=0), 0)

    ar = [b16(jnp.concatenate([At[i], Rt[i]], axis=0)) for i in n]
    sc_b = [_dg(ar[i], bd(b16(Bt[i])), NT) for i in n]
    sc_k = [_dg(ar[i], bd(b16(Kt[i])), NT) for i in n]
    N = [jnp.where(strict, sc_b[i][:C], 0.0) for i in n]
    Srb = [jnp.where(incl, sc_b[i][C:], 0.0) for i in n]
    M = [jnp.where(strict, sc_k[i][:C], 0.0) for i in n]
    Srk = [jnp.where(incl, sc_k[i][C:], 0.0) for i in n]
    ar_s = [_dg(ar[i], b16(S[i]), NT) for i in n]
    mv = [_dg(b16(jnp.concatenate([M[i], Srk[i]], axis=0)), bd(b16(V[i])), NN) for i in n]
    X = [ar_s[i][:C] + mv[i][:C] for i in n]

    def level_mask(b):
        return ((t_idx // (2 * b)) == (s_idx // (2 * b))) & ((t_idx % (2 * b)) >= b) & ((s_idx % (2 * b)) < b)

    T = [jnp.where(s_idx == t_idx, 1.0, 0.0) + jnp.where(level_mask(1), N[i], 0.0) for i in n]
    b = 2
    while b < C:
        lm = level_mask(b)
        d = [b16(T[i]) for i in n]
        E = [_dg(b16(jnp.where(lm, N[i], 0.0)), bd(d[i]), NN) for i in n]
        F = [_dg(d[i], bd(b16(E[i])), NN) for i in n]
        T = [T[i] + F[i] for i in n]
        b *= 2

    U = [_dg(b16(T[i]), bd(b16(X[i])), NN) for i in n]
    Y = [ar_s[i][C:] + mv[i][C:] + _dg(b16(Srb[i]), bd(b16(U[i])), NN) for i in n]
    upd = [_dg(b16(jnp.concatenate([U[i], V[i]], axis=0)),
               b16(jnp.concatenate([Bh[i], Kh[i]], axis=0)), TN) for i in n]
    return [(Y[i], S[i] * wcc[i] + jnp.where(bdmask, upd[i], 0.0)) for i in n]


def _wkv_chunk_kernel(*refs, bb, has_vres):
    if has_vres:
        (z_ref, vg_ref, vf_ref, s0_ref, sh0_ref, mu_ref, w0_ref, wup_ref, a0_ref, aup_ref,
         kk_ref, ka_ref, rk_ref, gng_ref, gnb_ref, y_ref, v_ref, s_ref, zp_scr) = refs
    else:
        (z_ref, s0_ref, sh0_ref, mu_ref, w0_ref, wup_ref, a0_ref, aup_ref,
         kk_ref, ka_ref, rk_ref, gng_ref, gnb_ref, y_ref, v_ref, s_ref, zp_scr) = refs
    C = z_ref.shape[1]
    c = pl.program_id(1)

    @pl.when(c == 0)
    def _():
        s_ref[...] = s0_ref[...]
        zp_scr[...] = sh0_ref[...]

    ones_g = _head_ones(GROUP_W)
    t_idx = lax.broadcasted_iota(jnp.int32, (C, GROUP_W), 0)
    s_idx = lax.broadcasted_iota(jnp.int32, (C, GROUP_W), 1) % C
    rr = lax.broadcasted_iota(jnp.int32, (GROUP_W, GROUP_W), 0) // HEAD
    cc = lax.broadcasted_iota(jnp.int32, (GROUP_W, GROUP_W), 1) // HEAD
    masks = (t_idx, s_idx, rr == cc)
    tri = jnp.where(lax.broadcasted_iota(jnp.int32, (C, C), 1)
                    <= lax.broadcasted_iota(jnp.int32, (C, C), 0), 1.0, 0.0).astype(BF16)
    row0 = lax.broadcasted_iota(jnp.int32, (C, N_SHIFT), 0) == 0

    chains, bonuses = [], []
    for bi in range(bb):
        z = z_ref[bi]
        z_prev = jnp.where(row0, zp_scr[bi], pltpu.roll(z, 1, 0))
        zp_scr[bi] = z[C - 1:C, :]
        vres = (vg_ref[bi], vf_ref[bi]) if has_vres else None
        r, k2, v, aa, bbv, lw, bonus = _timemix_pre(
            z, z_prev, mu_ref[...], w0_ref[...], wup_ref[...], a0_ref[...], aup_ref[...],
            kk_ref[...], ka_ref[...], rk_ref[...], vres, ones_g)
        v_ref[bi] = v
        bonuses.append(bonus)

        p1 = lw.astype(BF16)
        r1 = lw - p1.astype(F32)
        p2 = r1.astype(BF16)
        p3 = (r1 - p2.astype(F32)).astype(BF16)
        cum = _dg(tri, p1, NN) + (_dg(tri, p2, NN) + _dg(tri, p3, NN))
        cum_c = cum[C - 1:C, :]
        wc = jnp.exp(cum)
        wi = jnp.exp(-cum)
        wrel = jnp.exp(cum_c - cum)
        At = aa * jnp.exp(cum - lw)
        Rt = r * wc
        Bt = bbv * wi
        Kt = k2 * wi
        Bh = bbv * wrel
        Kh = k2 * wrel
        wcc = jnp.exp(cum_c)

        for g in range(N_HGROUPS):
            sl = slice(g * GROUP_W, (g + 1) * GROUP_W)
            chains.append((At[:, sl], Rt[:, sl], Bt[:, sl], Kt[:, sl], Bh[:, sl], Kh[:, sl],
                           v[:, sl], s_ref[bi, g], wcc[:, sl]))

    res = _chunk_groups(chains, masks)

    for bi in range(bb):
        for g in range(N_HGROUPS):
            s_ref[bi, g] = res[bi * N_HGROUPS + g][1]
        y = jnp.concatenate([res[bi * N_HGROUPS + g][0] for g in range(N_HGROUPS)], axis=1)
        d = y - _headsums([y], ones_g)[0] * (1.0 / HEAD)
        var = _headsums([d * d], ones_g)[0] * (1.0 / HEAD)
        y_ref[bi] = d * lax.rsqrt(var + GN_EPS) * gng_ref[...] + gnb_ref[...] + bonuses[bi]


def _wkv_chunk_call(z3, vres, s0_bd, shift0, p, bb):
    B, T, _ = z3.shape
    C = WKV_CHUNK
    has_vres = vres is not None

    def tok(w):
        return pl.BlockSpec((bb, C, w), lambda b, c: (b, c, 0))

    def full(shape):
        return pl.BlockSpec(shape, lambda b, c: (0,) * len(shape))

    s_spec = pl.BlockSpec((bb, N_HGROUPS, GROUP_W, GROUP_W), lambda b, c: (b, 0, 0, 0))
    in_specs = [tok(N_SHIFT)]
    args = [z3]
    if has_vres:
        in_specs += [tok(D_A), tok(D_A)]
        args += list(vres)
    in_specs += [s_spec, pl.BlockSpec((bb, 1, N_SHIFT), lambda b, c: (b, 0, 0))]
    args += [s0_bd, shift0]
    for name in ("mu", "w0", "wup", "a0", "aup", "k_k", "k_a", "r_k", "gn_g", "gn_b"):
        in_specs.append(full(p[name].shape))
        args.append(p[name])
    return pl.pallas_call(
        functools.partial(_wkv_chunk_kernel, bb=bb, has_vres=has_vres),
        grid=(B // bb, T // C),
        in_specs=in_specs,
        out_specs=[tok(D_A), tok(D_A), s_spec],
        out_shape=[jax.ShapeDtypeStruct((B, T, D_A), F32), jax.ShapeDtypeStruct((B, T, D_A), F32),
                   jax.ShapeDtypeStruct(s0_bd.shape, F32)],
        scratch_shapes=[pltpu.VMEM((bb, 1, N_SHIFT), F32)],
        compiler_params=pltpu.CompilerParams(
            dimension_semantics=("arbitrary", "arbitrary"), vmem_limit_bytes=VMEM_LIMIT_BYTES),
        name="wkv_chunk_vres" if has_vres else "wkv_chunk",
    )(*args)


def _wkv_pre_kernel(*refs, has_vres):
    if has_vres:
        (z_ref, zp_ref, vg_ref, vf_ref, mu_ref, w0_ref, wup_ref, a0_ref, aup_ref, kk_ref, ka_ref, rk_ref,
         r_o, w_o, k_o, v_o, a_o, b_o, bonus_o) = refs
        vres = (vg_ref[...], vf_ref[...])
    else:
        (z_ref, zp_ref, mu_ref, w0_ref, wup_ref, a0_ref, aup_ref, kk_ref, ka_ref, rk_ref,
         r_o, w_o, k_o, v_o, a_o, b_o, bonus_o) = refs
        vres = None
    r, k2, v, aa, bbv, lw, bonus = _timemix_pre(
        z_ref[...], zp_ref[...], mu_ref[...], w0_ref[...], wup_ref[...], a0_ref[...], aup_ref[...],
        kk_ref[...], ka_ref[...], rk_ref[...], vres, _head_ones(GROUP_W))
    r_o[...] = r
    w_o[...] = jnp.exp(lw)
    k_o[...] = k2
    v_o[...] = v
    a_o[...] = aa
    b_o[...] = bbv
    bonus_o[...] = bonus


def _wkv_pre_call(z, z_prev, vres, p):
    n = z.shape[0]
    has_vres = vres is not None
    args = [z, z_prev] + (list(vres) if has_vres else [])
    args += [p[k] for k in ("mu", "w0", "wup", "a0", "aup", "k_k", "k_a", "r_k")]
    return pl.pallas_call(
        functools.partial(_wkv_pre_kernel, has_vres=has_vres),
        out_shape=[jax.ShapeDtypeStruct((n, D_A), F32)] * 7,
        compiler_params=pltpu.CompilerParams(vmem_limit_bytes=VMEM_LIMIT_BYTES),
        name="wkv_pre_vres" if has_vres else "wkv_pre",
    )(*args)


def _wkv_step_kernel(s_ref, r_ref, w_ref, k_ref, v_ref, a_ref, b_ref, bonus_ref, gng_ref, gnb_ref,
                     y_ref, so_ref):
    S = s_ref[...]
    eye = (lax.broadcasted_iota(jnp.int32, (HEAD, HEAD), 0)
           == lax.broadcasted_iota(jnp.int32, (HEAD, HEAD), 1))
    sa = jnp.sum(S * a_ref[...], axis=-1, keepdims=True)
    vcol = jnp.sum(jnp.where(eye, v_ref[...], 0.0), axis=-1, keepdims=True)
    S2 = S * w_ref[...] + sa * b_ref[...] + vcol * k_ref[...]
    so_ref[...] = S2
    ycol = jnp.sum(S2 * r_ref[...], axis=-1, keepdims=True)
    y = jnp.sum(jnp.where(eye, ycol, 0.0), axis=-2, keepdims=True)
    m = jnp.mean(y, axis=-1, keepdims=True)
    d = y - m
    var = jnp.mean(d * d, axis=-1, keepdims=True)
    y_ref[...] = d * lax.rsqrt(var + GN_EPS) * gng_ref[...] + gnb_ref[...] + bonus_ref[...]


def _wkv_step_call(S, vecs, gng_t, gnb_t, tb):
    n = S.shape[0]
    vec_spec = pl.BlockSpec((tb, 1, HEAD), lambda i: (i, 0, 0))
    s_spec = pl.BlockSpec((tb, HEAD, HEAD), lambda i: (i, 0, 0))
    full = pl.BlockSpec((tb, 1, HEAD), lambda i: (0, 0, 0))
    return pl.pallas_call(
        _wkv_step_kernel,
        grid=(n // tb,),
        in_specs=[s_spec] + [vec_spec] * 7 + [full, full],
        out_specs=[vec_spec, s_spec],
        out_shape=[jax.ShapeDtypeStruct((n, 1, HEAD), F32), jax.ShapeDtypeStruct(S.shape, F32)],
        compiler_params=pltpu.CompilerParams(
            dimension_semantics=("arbitrary",), vmem_limit_bytes=VMEM_LIMIT_BYTES),
        name="wkv_step",
    )(S, *vecs, gng_t, gnb_t)


def _merge_kernel(*refs, chunked, final):
    (h_ref, ya_ref, ga_ref, u_ref, vn_ref, gb_ref, ma_ref, mb_ref, p_ref,
     sp_a_ref, sp_b_ref, wba_ref, wbb_ref, wout_ref, wple_ref, wpg_ref, bpg_ref) = refs[:17]
    rest = refs[17:]
    if final:
        fg_ref, h_out, y_out = rest
    else:
        (h_out,) = rest
    vn = vn_ref[...]
    u = u_ref[...]
    if chunked:
        tm = vn.shape[0]
        tril = (lax.broadcasted_iota(jnp.int32, (SPATIAL_CHUNK, SPATIAL_CHUNK), 1)
                <= lax.broadcasted_iota(jnp.int32, (SPATIAL_CHUNK, SPATIAL_CHUNK), 0))
        vb16 = vn.astype(BF16)
        rows_out = []
        for ci in range(tm // SPATIAL_CHUNK):
            rs = slice(ci * SPATIAL_CHUNK, (ci + 1) * SPATIAL_CHUNK)
            cols = []
            for g in range(N_GROUPS_B):
                wm = jnp.where(tril, sp_a_ref[g], 0.0).astype(BF16)
                mixed = jnp.dot(wm, vb16[rs, g * GROUP_B:(g + 1) * GROUP_B], preferred_element_type=F32)
                cols.append(mixed + sp_b_ref[:, g:g + 1])
            rows_out.append(jnp.concatenate(cols, axis=1))
        mixed = jnp.concatenate(rows_out, axis=0)
    else:
        mixed = vn * sp_a_ref[...] + sp_b_ref[...]
    yb = u * mixed
    br_a = jnp.dot((ya_ref[...] * ga_ref[...]).astype(BF16), wba_ref[...], preferred_element_type=F32)
    br_b = jnp.dot((yb * gb_ref[...]).astype(BF16), wbb_ref[...], preferred_element_type=F32)
    merged = ma_ref[...] * br_a + mb_ref[...] * br_b
    h = h_ref[...] + jnp.dot(merged.astype(BF16), wout_ref[...], preferred_element_type=F32)
    ple = jnp.dot(p_ref[...].astype(BF16), wple_ref[...], preferred_element_type=F32)
    gate = _sigmoid(jnp.dot(h.astype(BF16), wpg_ref[...], preferred_element_type=F32) + bpg_ref[...])
    h = h + gate * ple
    h_out[...] = h
    if final:
        y_out[...] = h * lax.rsqrt(jnp.mean(h * h, axis=-1, keepdims=True) + EPS) * fg_ref[...]


def _merge_call(h, ya, ga, u, vn, gb, ma, mb, pe_all, layer, sp_a, sp_b, wl, final_g, chunked, tm):
    n = h.shape[0]
    final = final_g is not None

    def rows(w):
        return pl.BlockSpec((tm, w), lambda i: (i, 0))

    def full(a):
        return pl.BlockSpec(a.shape, lambda i: (0,) * a.ndim, pipeline_mode=pl.Buffered(1))

    row_args = [h, ya, ga, u, vn, gb, ma, mb]
    pe_spec = pl.BlockSpec((tm, D_PLE), lambda i: (layer * (n // tm) + i, 0))
    w_args = [sp_a, sp_b, wl["w_br_a"], wl["w_br_b"], wl["w_out"], wl["w_ple"], wl["w_pg"], wl["b_pg"]]
    if final:
        w_args.append(final_g)
    n_out = 2 if final else 1
    return pl.pallas_call(
        functools.partial(_merge_kernel, chunked=chunked, final=final),
        grid=(n // tm,),
        in_specs=[rows(a.shape[1]) for a in row_args] + [pe_spec] + [full(a) for a in w_args],
        out_specs=[rows(D_MODEL)] * n_out,
        out_shape=[jax.ShapeDtypeStruct((n, D_MODEL), F32)] * n_out,
        compiler_params=pltpu.CompilerParams(
            dimension_semantics=("arbitrary",), vmem_limit_bytes=VMEM_LIMIT_BYTES),
        name="merge_" + ("c" if chunked else "s") + ("f" if final else ""),
    )(*row_args, pe_all, *w_args)


def _layer_params(W, l):
    row = lambda a: a.reshape(1, -1)
    zpad = jnp.zeros((D_LORA, D_A), F32)
    p = dict(
        norm_g=row(W["norm_g"][l]),
        ln_g=row(W["ln_v_g"][l]), ln_b=row(W["ln_v_b"][l]),
        mu=row(W["shift_mu"][l]), w0=row(W["w0"][l]), a0=row(W["a0"][l]),
        wup=jnp.concatenate([W["w_up"][l], zpad], axis=0).astype(BF16),
        aup=jnp.concatenate([zpad, W["a_up"][l]], axis=0).astype(BF16),
        k_k=row(W["k_k"][l]), k_a=row(W["k_a"][l]), r_k=row(W["r_k"][l]),
        gn_g=row(W["gn_g"][l]), gn_b=row(W["gn_b"][l]),
        w_br_a=W["w_br_a"][l].astype(BF16), w_br_b=W["w_br_b"][l].astype(BF16),
        w_out=W["w_out"][l].astype(BF16), w_ple=W["w_ple"][l].astype(BF16),
        w_pg=W["w_ple_gate"][l].astype(BF16), b_pg=row(W["b_ple_gate"][l]),
        sp_w=W["w_spatial"][l], sp_bT=W["b_spatial"][l].T,
        sp_w0=jnp.repeat(W["w_spatial"][l][:, 0, 0], GROUP_B).reshape(1, D_B),
        sp_b0=jnp.repeat(W["b_spatial"][l][:, 0], GROUP_B).reshape(1, D_B),
    )
    if l > 0:
        p["vres"] = (W["vres_down"][l - 1].astype(BF16), W["vres_up"][l - 1].astype(BF16),
                     row(W["vres_b"][l - 1]))
    else:
        p["vres"] = None
    return p


def _diag_blocks(s_bd):
    B = s_bd.shape[0]
    blocks = [s_bd[:, :, h * HEAD:(h + 1) * HEAD, h * HEAD:(h + 1) * HEAD] for h in range(HEADS_PER_GROUP)]
    return jnp.stack(blocks, axis=2).reshape(B, N_HEADS, HEAD, HEAD)


def _forward_prompt(x, pe, W, params, w_in_all):
    B, T, _ = x.shape
    n = B * T
    h = x.reshape(n, D_MODEL)
    pe_all = pe.reshape(-1, D_PLE)
    v_first = None
    wkvs, shifts = [], []
    depth = len(params)
    for l, p in enumerate(params):
        outs = _proj_call(h, p["norm_g"], w_in_all, l, p["ln_g"], p["ln_b"], p["vres"], tm=512)
        z, ga, u, vn, gb, ma, mb = outs[:7]
        z3 = z.reshape(B, T, N_SHIFT)
        vres = None if l == 0 else (outs[7].reshape(B, T, D_A), v_first)
        s0 = jnp.zeros((B, N_HGROUPS, GROUP_W, GROUP_W), F32)
        shift0 = jnp.zeros((B, 1, N_SHIFT), F32)
        ya, v, s_bd = _wkv_chunk_call(z3, vres, s0, shift0, p, bb=8)
        if l == 0:
            v_first = v
        wkvs.append(_diag_blocks(s_bd))
        shifts.append(z3[:, T - 1, :])
        fg = W["final_g"].reshape(1, -1) if l == depth - 1 else None
        res = _merge_call(h, ya.reshape(n, D_A), ga, u, vn, gb, ma, mb, pe_all, l,
                          p["sp_w"], p["sp_bT"], p, fg, chunked=True, tm=512)
        h = res[0]
    y = res[1].reshape(B, T, D_MODEL)
    return y, jnp.stack(wkvs), jnp.stack(shifts)


def _forward_sample(x, pe, wkv0, shift0, W, params, w_in_all):
    B = x.shape[0]
    h = x.reshape(B, D_MODEL)
    pe_all = pe.reshape(-1, D_PLE)
    v_first = None
    wkvs, shifts, chunk_vs = [], [], []
    depth = len(params)
    tb = 64
    for l, p in enumerate(params):
        outs = _proj_call(h, p["norm_g"], w_in_all, l, p["ln_g"], p["ln_b"], p["vres"], tm=B)
        z, ga, u, vn, gb, ma, mb = outs[:7]
        vres = None if l == 0 else (outs[7], v_first)
        r, w, k2, v, aa, bbv, bonus = _wkv_pre_call(z, shift0[l], vres, p)
        if l == 0:
            v_first = v
        vecs = [a.reshape(B * N_HEADS, 1, HEAD) for a in (r, w, k2, v, aa, bbv, bonus)]
        gng_t = jnp.tile(p["gn_g"].reshape(N_HEADS, 1, HEAD), (tb // N_HEADS, 1, 1))
        gnb_t = jnp.tile(p["gn_b"].reshape(N_HEADS, 1, HEAD), (tb // N_HEADS, 1, 1))
        ya, s_new = _wkv_step_call(wkv0[l].reshape(B * N_HEADS, HEAD, HEAD), vecs, gng_t, gnb_t, tb)
        wkvs.append(s_new.reshape(B, N_HEADS, HEAD, HEAD))
        shifts.append(z)
        chunk_vs.append(vn.reshape(B, 1, D_B))
        fg = W["final_g"].reshape(1, -1) if l == depth - 1 else None
        res = _merge_call(h, ya.reshape(B, D_A), ga, u, vn, gb, ma, mb, pe_all, l,
                          p["sp_w0"], p["sp_b0"], p, fg, chunked=False, tm=B)
        h = res[0]
    y = res[1].reshape(B, 1, D_MODEL)
    return y, jnp.stack(wkvs), jnp.stack(shifts), jnp.stack(chunk_vs)


def kernel(x_prompt, x_sample, state_rwkv_wkv, state_rwkv_shift, p_prompt, p_sample, norm_g, w_in, shift_mu, w0, w_up, a0, a_up, vres_down, vres_up, vres_b, k_k, k_a, r_k, gn_g, gn_b, ln_v_g, ln_v_b, w_spatial, b_spatial, w_br_a, w_br_b, w_out, w_ple, w_ple_gate, b_ple_gate, final_g):
    W = dict(norm_g=norm_g, w_in=w_in, shift_mu=shift_mu, w0=w0, w_up=w_up, a0=a0, a_up=a_up,
             vres_down=vres_down, vres_up=vres_up, vres_b=vres_b, k_k=k_k, k_a=k_a, r_k=r_k,
             gn_g=gn_g, gn_b=gn_b, ln_v_g=ln_v_g, ln_v_b=ln_v_b, w_spatial=w_spatial,
             b_spatial=b_spatial, w_br_a=w_br_a, w_br_b=w_br_b, w_out=w_out, w_ple=w_ple,
             w_ple_gate=w_ple_gate, b_ple_gate=b_ple_gate, final_g=final_g)
    params = [_layer_params(W, l) for l in range(w_in.shape[0])]
    w_in_all = w_in.astype(BF16)
    y_p, wkv_p, shift_p = _forward_prompt(x_prompt, p_prompt, W, params, w_in_all)
    y_s, wkv_s, shift_s, chunk_v = _forward_sample(
        x_sample, p_sample, state_rwkv_wkv, state_rwkv_shift, W, params, w_in_all)
    return (y_p, y_s, wkv_p, shift_p, wkv_s, shift_s, chunk_v)
```

```python
import functools

import jax
import jax.numpy as jnp
from jax import lax
from jax.experimental import pallas as pl
from jax.experimental.pallas import tpu as pltpu

D_MODEL = 1024
HEAD = 64
N_HEADS = 8
D_A = N_HEADS * HEAD
D_LORA = 64
D_B = 512
N_GROUPS_B = 4
GROUP_B = D_B // N_GROUPS_B
SPATIAL_CHUNK = 128
D_PLE = 256
N_SHIFT = 3 * D_A + 2 * D_LORA
D_IN = N_SHIFT + D_A + 3 * D_B + 2 * D_MODEL
EPS = 1e-6
GN_EPS = 64e-5
LN_EPS = 1e-5

O_GA = N_SHIFT
O_U = O_GA + D_A
O_VB = O_U + D_B
O_GB = O_VB + D_B
O_MA = O_GB + D_B
O_MB = O_MA + D_MODEL

WKV_CHUNK = 64
HEADS_PER_GROUP = 4
GROUP_W = HEADS_PER_GROUP * HEAD
N_HGROUPS = N_HEADS // HEADS_PER_GROUP
VMEM_LIMIT_BYTES = 56 * 1024 * 1024

F32 = jnp.float32
BF16 = jnp.bfloat16

NN = ((1,), (0,))
NT = ((1,), (1,))
TN = ((0,), (0,))


def _dg(a, b, dims):
    return lax.dot_general(a, b, (dims, ((), ())), preferred_element_type=F32)


def _split(x):
    hi = x.astype(BF16)
    lo = (x - hi.astype(F32)).astype(BF16)
    return hi, lo


def _sigmoid(x):
    return 1.0 / (1.0 + jnp.exp(-x))


def _gelu_tanh(x):
    return 0.5 * x * (1.0 + jnp.tanh(0.7978845608028654 * (x + 0.044715 * (x * x * x))))


def _softplus(x):
    return jnp.maximum(x, 0.0) + jnp.log(1.0 + jnp.exp(-jnp.abs(x)))


def _head_ones(n):
    r = lax.broadcasted_iota(jnp.int32, (n, n), 0) // HEAD
    c = lax.broadcasted_iota(jnp.int32, (n, n), 1) // HEAD
    return jnp.where(r == c, 1.0, 0.0).astype(BF16)


def _headsums(xs, ones_g):
    rows = xs[0].shape[0]
    pieces = []
    for x in xs:
        hi, lo = _split(x)
        pieces += [hi[:, :GROUP_W], lo[:, :GROUP_W], hi[:, GROUP_W:], lo[:, GROUP_W:]]
    o = _dg(jnp.concatenate(pieces, axis=0), ones_g, NN)
    outs = []
    for i in range(len(xs)):
        q = [o[(4 * i + j) * rows:(4 * i + j + 1) * rows] for j in range(4)]
        outs.append(jnp.concatenate([q[0] + q[1], q[2] + q[3]], axis=1))
    return outs


def _spatial_mix(vn, sp_a_ref, sp_b_ref, chunked):
    if not chunked:
        return vn * sp_a_ref[...] + sp_b_ref[...]
    tril = (lax.broadcasted_iota(jnp.int32, (SPATIAL_CHUNK, SPATIAL_CHUNK), 1)
            <= lax.broadcasted_iota(jnp.int32, (SPATIAL_CHUNK, SPATIAL_CHUNK), 0))
    vb16 = vn.astype(BF16)
    wm = [jnp.where(tril, sp_a_ref[g], 0.0).astype(BF16) for g in range(N_GROUPS_B)]
    rows_out = []
    for ci in range(vn.shape[0] // SPATIAL_CHUNK):
        rs = slice(ci * SPATIAL_CHUNK, (ci + 1) * SPATIAL_CHUNK)
        cols = [jnp.dot(wm[g], vb16[rs, g * GROUP_B:(g + 1) * GROUP_B], preferred_element_type=F32)
                + sp_b_ref[:, g:g + 1] for g in range(N_GROUPS_B)]
        rows_out.append(jnp.concatenate(cols, axis=1))
    return jnp.concatenate(rows_out, axis=0)


def _proj_kernel(*refs, has_vres, chunked):
    (x_ref, g_ref, w_ref, lng_ref, lnb_ref, sp_a_ref, sp_b_ref, wbb_ref) = refs[:8]
    rest = refs[8:]
    if has_vres:
        vd_ref, vu_ref, vb_ref = rest[:3]
        rest = rest[3:]
    z_ref, ga_ref, ma_ref, pb_ref = rest[:4]
    rest = rest[4:]
    if not chunked:
        vn_ref = rest[0]
        rest = rest[1:]
    x = x_ref[...]
    xn = x * lax.rsqrt(jnp.mean(x * x, axis=-1, keepdims=True) + EPS) * g_ref[...]
    xb = xn.astype(BF16)

    def seg(lo, hi):
        return jnp.dot(xb, w_ref[:, lo:hi], preferred_element_type=F32)

    z_ref[...] = seg(0, N_SHIFT)
    t = seg(O_GA, O_U)
    ga_ref[...] = t * _sigmoid(t)
    ma_ref[...] = _sigmoid(seg(O_MA, O_MB))
    u = _gelu_tanh(seg(O_U, O_VB))
    vb = _gelu_tanh(seg(O_VB, O_GB))
    mu = jnp.mean(vb, axis=-1, keepdims=True)
    d = vb - mu
    var = jnp.mean(d * d, axis=-1, keepdims=True)
    vn = d * lax.rsqrt(var + LN_EPS) * lng_ref[...] + lnb_ref[...]
    if not chunked:
        vn_ref[...] = vn
    t = seg(O_GB, O_MA)
    yb = u * _spatial_mix(vn, sp_a_ref, sp_b_ref, chunked) * (t * _sigmoid(t))
    br_b = jnp.dot(yb.astype(BF16), wbb_ref[...], preferred_element_type=F32)
    pb_ref[...] = _sigmoid(seg(O_MB, D_IN)) * br_b
    if has_vres:
        (vg_ref,) = rest
        low = jnp.dot(xb, vd_ref[...], preferred_element_type=F32)
        up = jnp.dot(low.astype(BF16), vu_ref[...], preferred_element_type=F32)
        vg_ref[...] = _sigmoid(vb_ref[...] + up)


def _proj_call(x2d, w_in_all, layer, p, chunked, tm):
    n = x2d.shape[0]
    vres = p["vres"]
    has_vres = vres is not None

    def full(a):
        return pl.BlockSpec(a.shape, lambda i: (0,) * a.ndim, pipeline_mode=pl.Buffered(1))

    def rows(w):
        return pl.BlockSpec((tm, w), lambda i: (i, 0))

    w_spec = pl.BlockSpec((None, D_MODEL, D_IN), lambda i: (layer, 0, 0), pipeline_mode=pl.Buffered(1))
    sp_a, sp_b = (p["sp_w"], p["sp_bT"]) if chunked else (p["sp_w0"], p["sp_b0"])
    consts = [p["ln_g"], p["ln_b"], sp_a, sp_b, p["w_br_b"]] + (list(vres) if has_vres else [])
    widths = [N_SHIFT, D_A, D_MODEL, D_MODEL] + ([] if chunked else [D_B]) + ([D_A] if has_vres else [])
    return pl.pallas_call(
        functools.partial(_proj_kernel, has_vres=has_vres, chunked=chunked),
        grid=(n // tm,),
        in_specs=[rows(D_MODEL), full(p["norm_g"]), w_spec] + [full(a) for a in consts],
        out_specs=[rows(w) for w in widths],
        out_shape=[jax.ShapeDtypeStruct((n, w), F32) for w in widths],
        compiler_params=pltpu.CompilerParams(
            dimension_semantics=("arbitrary",), vmem_limit_bytes=VMEM_LIMIT_BYTES),
        name="proj" + ("_c" if chunked else "_s") + ("_vres" if has_vres else ""),
    )(x2d, p["norm_g"], w_in_all, *consts)


def _timemix_pre(z, z_prev, mu, w0, wup_pad, a0, aup_pad, k_k, k_a, r_k, vres, ones_g):
    zs = z + (z_prev - z) * mu
    r = zs[:, 0:D_A]
    k = zs[:, D_A:2 * D_A]
    v = zs[:, 2 * D_A:3 * D_A]
    lora_in = zs[:, 3 * D_A:N_SHIFT]
    wq = w0 + jnp.dot(jnp.tanh(lora_in).astype(BF16), wup_pad, preferred_element_type=F32)
    w = -_softplus(-wq) - 0.5
    lw = -jnp.exp(w)
    a = _sigmoid(a0 + jnp.dot(lora_in.astype(BF16), aup_pad, preferred_element_type=F32))
    if vres is not None:
        vgate, vfirst = vres
        v = v + (vfirst - v) * vgate
    kk = k * k_k
    k2 = k * (1.0 + (a - 1.0) * k_a)
    ss, rk_sum = _headsums([kk * kk, r * k2 * r_k], ones_g)
    kk = kk / jnp.maximum(jnp.sqrt(ss), 1e-12)
    return r, k2, v, -kk, kk * a, lw, rk_sum * v


def _chunk_groups(chains, masks):
    t_idx, s_idx, bdmask = masks
    C = chains[0][0].shape[0]
    n = range(len(chains))
    At, Rt, Bt, Kt, Bh, Kh, V, S, wcc = (list(col) for col in zip(*chains))
    strict = s_idx < t_idx
    incl = s_idx <= t_idx

    def b16(x):
        return x.astype(BF16)

    def bd(x):
        return jnp.where(bdmask, jnp.concatenate([x] * HEADS_PER_GROUP, axis=0), 0)

    ar = [b16(jnp.concatenate([At[i], Rt[i]], axis=0)) for i in n]
    sc_b = [_dg(ar[i], bd(b16(Bt[i])), NT) for i in n]
    sc_k = [_dg(ar[i], bd(b16(Kt[i])), NT) for i in n]
    N = [jnp.where(strict, sc_b[i][:C], 0.0) for i in n]
    Srb = [jnp.where(incl, sc_b[i][C:], 0.0) for i in n]
    M = [jnp.where(strict, sc_k[i][:C], 0.0) for i in n]
    Srk = [jnp.where(incl, sc_k[i][C:], 0.0) for i in n]
    ar_s = [_dg(ar[i], b16(S[i]), NT) for i in n]
    mv = [_dg(b16(jnp.concatenate([M[i], Srk[i]], axis=0)), bd(b16(V[i])), NN) for i in n]
    X = [ar_s[i][:C] + mv[i][:C] for i in n]

    def level_mask(b):
        return ((t_idx // (2 * b)) == (s_idx // (2 * b))) & ((t_idx % (2 * b)) >= b) & ((s_idx % (2 * b)) < b)

    T = [jnp.where(s_idx == t_idx, 1.0, 0.0) + jnp.where(level_mask(1), N[i], 0.0) for i in n]
    b = 2
    while b < C:
        lm = level_mask(b)
        d = [b16(T[i]) for i in n]
        E = [_dg(b16(jnp.where(lm, N[i], 0.0)), bd(d[i]), NN) for i in n]
        F = [_dg(d[i], bd(b16(E[i])), NN) for i in n]
        T = [T[i] + F[i] for i in n]
        b *= 2

    U = [_dg(b16(T[i]), bd(b16(X[i])), NN) for i in n]
    Y = [ar_s[i][C:] + mv[i][C:] + _dg(b16(Srb[i]), bd(b16(U[i])), NN) for i in n]
    upd = [_dg(b16(jnp.concatenate([U[i], V[i]], axis=0)),
               b16(jnp.concatenate([Bh[i], Kh[i]], axis=0)), TN) for i in n]
    return [(Y[i], S[i] * wcc[i] + jnp.where(bdmask, upd[i], 0.0)) for i in n]


def _wkv_chunk_kernel(*refs, bb, has_vres):
    if has_vres:
        (z_ref, vg_ref, vf_ref, s0_ref, sh0_ref, mu_ref, w0_ref, wup_ref, a0_ref, aup_ref,
         kk_ref, ka_ref, rk_ref, gng_ref, gnb_ref, y_ref, v_ref, s_ref, zp_scr) = refs
    else:
        (z_ref, s0_ref, sh0_ref, mu_ref, w0_ref, wup_ref, a0_ref, aup_ref,
         kk_ref, ka_ref, rk_ref, gng_ref, gnb_ref, y_ref, v_ref, s_ref, zp_scr) = refs
    C = z_ref.shape[1]
    c = pl.program_id(1)

    @pl.when(c == 0)
    def _():
        s_ref[...] = s0_ref[...]
        zp_scr[...] = sh0_ref[...]

    ones_g = _head_ones(GROUP_W)
    t_idx = lax.broadcasted_iota(jnp.int32, (C, GROUP_W), 0)
    s_idx = lax.broadcasted_iota(jnp.int32, (C, GROUP_W), 1) % C
    rr = lax.broadcasted_iota(jnp.int32, (GROUP_W, GROUP_W), 0) // HEAD
    cc = lax.broadcasted_iota(jnp.int32, (GROUP_W, GROUP_W), 1) // HEAD
    masks = (t_idx, s_idx, rr == cc)
    tri = jnp.where(lax.broadcasted_iota(jnp.int32, (C, C), 1)
                    <= lax.broadcasted_iota(jnp.int32, (C, C), 0), 1.0, 0.0).astype(BF16)
    row0 = lax.broadcasted_iota(jnp.int32, (C, N_SHIFT), 0) == 0

    chains, bonuses = [], []
    for bi in range(bb):
        z = z_ref[bi]
        z_prev = jnp.where(row0, zp_scr[bi], pltpu.roll(z, 1, 0))
        zp_scr[bi] = z[C - 1:C, :]
        vres = (vg_ref[bi], vf_ref[bi]) if has_vres else None
        r, k2, v, aa, bbv, lw, bonus = _timemix_pre(
            z, z_prev, mu_ref[...], w0_ref[...], wup_ref[...], a0_ref[...], aup_ref[...],
            kk_ref[...], ka_ref[...], rk_ref[...], vres, ones_g)
        v_ref[bi] = v
        bonuses.append(bonus)

        p1 = lw.astype(BF16)
        r1 = lw - p1.astype(F32)
        p2 = r1.astype(BF16)
        p3 = (r1 - p2.astype(F32)).astype(BF16)
        cum = _dg(tri, p1, NN) + (_dg(tri, p2, NN) + _dg(tri, p3, NN))
        cum_c = cum[C - 1:C, :]
        wc = jnp.exp(cum)
        wi = jnp.exp(-cum)
        wrel = jnp.exp(cum_c - cum)
        At = aa * jnp.exp(cum - lw)
        Rt = r * wc
        Bt = bbv * wi
        Kt = k2 * wi
        Bh = bbv * wrel
        Kh = k2 * wrel
        wcc = jnp.exp(cum_c)

        for g in range(N_HGROUPS):
            sl = slice(g * GROUP_W, (g + 1) * GROUP_W)
            chains.append((At[:, sl], Rt[:, sl], Bt[:, sl], Kt[:, sl], Bh[:, sl], Kh[:, sl],
                           v[:, sl], s_ref[bi, g], wcc[:, sl]))

    res = _chunk_groups(chains, masks)

    for bi in range(bb):
        for g in range(N_HGROUPS):
            s_ref[bi, g] = res[bi * N_HGROUPS + g][1]
        y = jnp.concatenate([res[bi * N_HGROUPS + g][0] for g in range(N_HGROUPS)], axis=1)
        d = y - _headsums([y], ones_g)[0] * (1.0 / HEAD)
        var = _headsums([d * d], ones_g)[0] * (1.0 / HEAD)
        y_ref[bi] = d * lax.rsqrt(var + GN_EPS) * gng_ref[...] + gnb_ref[...] + bonuses[bi]


def _wkv_chunk_call(z3, vres, s0_bd, shift0, p, bb):
    B, T, _ = z3.shape
    C = WKV_CHUNK
    has_vres = vres is not None

    def tok(w):
        return pl.BlockSpec((bb, C, w), lambda b, c: (b, c, 0))

    def full(shape):
        return pl.BlockSpec(shape, lambda b, c: (0,) * len(shape))

    s_spec = pl.BlockSpec((bb, N_HGROUPS, GROUP_W, GROUP_W), lambda b, c: (b, 0, 0, 0))
    in_specs = [tok(N_SHIFT)]
    args = [z3]
    if has_vres:
        in_specs += [tok(D_A), tok(D_A)]
        args += list(vres)
    in_specs += [s_spec, pl.BlockSpec((bb, 1, N_SHIFT), lambda b, c: (b, 0, 0))]
    args += [s0_bd, shift0]
    for name in ("mu", "w0", "wup", "a0", "aup", "k_k", "k_a", "r_k", "gn_g", "gn_b"):
        in_specs.append(full(p[name].shape))
        args.append(p[name])
    return pl.pallas_call(
        functools.partial(_wkv_chunk_kernel, bb=bb, has_vres=has_vres),
        grid=(B // bb, T // C),
        in_specs=in_specs,
        out_specs=[tok(D_A), tok(D_A), s_spec],
        out_shape=[jax.ShapeDtypeStruct((B, T, D_A), F32), jax.ShapeDtypeStruct((B, T, D_A), F32),
                   jax.ShapeDtypeStruct(s0_bd.shape, F32)],
        scratch_shapes=[pltpu.VMEM((bb, 1, N_SHIFT), F32)],
        compiler_params=pltpu.CompilerParams(
            dimension_semantics=("arbitrary", "arbitrary"), vmem_limit_bytes=VMEM_LIMIT_BYTES),
        name="wkv_chunk_vres" if has_vres else "wkv_chunk",
    )(*args)


def _wkv_pre_kernel(*refs, has_vres):
    if has_vres:
        (z_ref, zp_ref, vg_ref, vf_ref, mu_ref, w0_ref, wup_ref, a0_ref, aup_ref, kk_ref, ka_ref, rk_ref,
         r_o, w_o, k_o, v_o, a_o, b_o, bonus_o) = refs
        vres = (vg_ref[...], vf_ref[...])
    else:
        (z_ref, zp_ref, mu_ref, w0_ref, wup_ref, a0_ref, aup_ref, kk_ref, ka_ref, rk_ref,
         r_o, w_o, k_o, v_o, a_o, b_o, bonus_o) = refs
        vres = None
    r, k2, v, aa, bbv, lw, bonus = _timemix_pre(
        z_ref[...], zp_ref[...], mu_ref[...], w0_ref[...], wup_ref[...], a0_ref[...], aup_ref[...],
        kk_ref[...], ka_ref[...], rk_ref[...], vres, _head_ones(GROUP_W))
    r_o[...] = r
    w_o[...] = jnp.exp(lw)
    k_o[...] = k2
    v_o[...] = v
    a_o[...] = aa
    b_o[...] = bbv
    bonus_o[...] = bonus


def _wkv_pre_call(z, z_prev, vres, p):
    n = z.shape[0]
    has_vres = vres is not None
    args = [z, z_prev] + (list(vres) if has_vres else [])
    args += [p[k] for k in ("mu", "w0", "wup", "a0", "aup", "k_k", "k_a", "r_k")]
    return pl.pallas_call(
        functools.partial(_wkv_pre_kernel, has_vres=has_vres),
        out_shape=[jax.ShapeDtypeStruct((n, D_A), F32)] * 7,
        compiler_params=pltpu.CompilerParams(vmem_limit_bytes=VMEM_LIMIT_BYTES),
        name="wkv_pre_vres" if has_vres else "wkv_pre",
    )(*args)


def _wkv_step_kernel(s_ref, r_ref, w_ref, k_ref, v_ref, a_ref, b_ref, bonus_ref, gng_ref, gnb_ref,
                     y_ref, so_ref):
    S = s_ref[...]
    eye = (lax.broadcasted_iota(jnp.int32, (HEAD, HEAD), 0)
           == lax.broadcasted_iota(jnp.int32, (HEAD, HEAD), 1))
    sa = jnp.sum(S * a_ref[...], axis=-1, keepdims=True)
    vcol = jnp.sum(jnp.where(eye, v_ref[...], 0.0), axis=-1, keepdims=True)
    S2 = S * w_ref[...] + sa * b_ref[...] + vcol * k_ref[...]
    so_ref[...] = S2
    ycol = jnp.sum(S2 * r_ref[...], axis=-1, keepdims=True)
    y = jnp.sum(jnp.where(eye, ycol, 0.0), axis=-2, keepdims=True)
    m = jnp.mean(y, axis=-1, keepdims=True)
    d = y - m
    var = jnp.mean(d * d, axis=-1, keepdims=True)
    y_ref[...] = d * lax.rsqrt(var + GN_EPS) * gng_ref[...] + gnb_ref[...] + bonus_ref[...]


def _wkv_step_call(S, vecs, gng_t, gnb_t, tb):
    n = S.shape[0]
    vec_spec = pl.BlockSpec((tb, 1, HEAD), lambda i: (i, 0, 0))
    s_spec = pl.BlockSpec((tb, HEAD, HEAD), lambda i: (i, 0, 0))
    full = pl.BlockSpec((tb, 1, HEAD), lambda i: (0, 0, 0))
    return pl.pallas_call(
        _wkv_step_kernel,
        grid=(n // tb,),
        in_specs=[s_spec] + [vec_spec] * 7 + [full, full],
        out_specs=[vec_spec, s_spec],
        out_shape=[jax.ShapeDtypeStruct((n, 1, HEAD), F32), jax.ShapeDtypeStruct(S.shape, F32)],
        compiler_params=pltpu.CompilerParams(
            dimension_semantics=("arbitrary",), vmem_limit_bytes=VMEM_LIMIT_BYTES),
        name="wkv_step",
    )(S, *vecs, gng_t, gnb_t)


def _merge_kernel(*refs, final):
    (h_ref, ya_ref, ga_ref, ma_ref, pb_ref, p_ref, wba_ref, wout_ref, wple_ref, wpg_ref, bpg_ref) = refs[:11]
    rest = refs[11:]
    if final:
        fg_ref, h_out, y_out = rest
    else:
        (h_out,) = rest
    br_a = jnp.dot((ya_ref[...] * ga_ref[...]).astype(BF16), wba_ref[...], preferred_element_type=F32)
    merged = ma_ref[...] * br_a + pb_ref[...]
    h = h_ref[...] + jnp.dot(merged.astype(BF16), wout_ref[...], preferred_element_type=F32)
    ple = jnp.dot(p_ref[...].astype(BF16), wple_ref[...], preferred_element_type=F32)
    gate = _sigmoid(jnp.dot(h.astype(BF16), wpg_ref[...], preferred_element_type=F32) + bpg_ref[...])
    h = h + gate * ple
    h_out[...] = h
    if final:
        y_out[...] = h * lax.rsqrt(jnp.mean(h * h, axis=-1, keepdims=True) + EPS) * fg_ref[...]


def _merge_call(h, ya, ga, ma, pb, pe_all, layer, wl, final_g, tm):
    n = h.shape[0]
    final = final_g is not None

    def rows(w):
        return pl.BlockSpec((tm, w), lambda i: (i, 0))

    def full(a):
        return pl.BlockSpec(a.shape, lambda i: (0,) * a.ndim, pipeline_mode=pl.Buffered(1))

    row_args = [h, ya, ga, ma, pb]
    pe_spec = pl.BlockSpec((tm, D_PLE), lambda i: (layer * (n // tm) + i, 0))
    w_args = [wl["w_br_a"], wl["w_out"], wl["w_ple"], wl["w_pg"], wl["b_pg"]]
    if final:
        w_args.append(final_g)
    n_out = 2 if final else 1
    return pl.pallas_call(
        functools.partial(_merge_kernel, final=final),
        grid=(n // tm,),
        in_specs=[rows(a.shape[1]) for a in row_args] + [pe_spec] + [full(a) for a in w_args],
        out_specs=[rows(D_MODEL)] * n_out,
        out_shape=[jax.ShapeDtypeStruct((n, D_MODEL), F32)] * n_out,
        compiler_params=pltpu.CompilerParams(
            dimension_semantics=("arbitrary",), vmem_limit_bytes=VMEM_LIMIT_BYTES),
        name="merge" + ("_f" if final else ""),
    )(*row_args, pe_all, *w_args)


def _layer_params(W, l):
    row = lambda a: a.reshape(1, -1)
    zpad = jnp.zeros((D_LORA, D_A), F32)
    p = dict(
        norm_g=row(W["norm_g"][l]),
        ln_g=row(W["ln_v_g"][l]), ln_b=row(W["ln_v_b"][l]),
        mu=row(W["shift_mu"][l]), w0=row(W["w0"][l]), a0=row(W["a0"][l]),
        wup=jnp.concatenate([W["w_up"][l], zpad], axis=0).astype(BF16),
        aup=jnp.concatenate([zpad, W["a_up"][l]], axis=0).astype(BF16),
        k_k=row(W["k_k"][l]), k_a=row(W["k_a"][l]), r_k=row(W["r_k"][l]),
        gn_g=row(W["gn_g"][l]), gn_b=row(W["gn_b"][l]),
        w_br_a=W["w_br_a"][l].astype(BF16), w_br_b=W["w_br_b"][l].astype(BF16),
        w_out=W["w_out"][l].astype(BF16), w_ple=W["w_ple"][l].astype(BF16),
        w_pg=W["w_ple_gate"][l].astype(BF16), b_pg=row(W["b_ple_gate"][l]),
        sp_w=W["w_spatial"][l], sp_bT=W["b_spatial"][l].T,
        sp_w0=jnp.repeat(W["w_spatial"][l][:, 0, 0], GROUP_B).reshape(1, D_B),
        sp_b0=jnp.repeat(W["b_spatial"][l][:, 0], GROUP_B).reshape(1, D_B),
    )
    if l > 0:
        p["vres"] = (W["vres_down"][l - 1].astype(BF16), W["vres_up"][l - 1].astype(BF16),
                     row(W["vres_b"][l - 1]))
    else:
        p["vres"] = None
    return p


def _diag_blocks(s_bd):
    B = s_bd.shape[0]
    blocks = [s_bd[:, :, h * HEAD:(h + 1) * HEAD, h * HEAD:(h + 1) * HEAD] for h in range(HEADS_PER_GROUP)]
    return jnp.stack(blocks, axis=2).reshape(B, N_HEADS, HEAD, HEAD)


def _forward_prompt(x, pe, W, params, w_in_all):
    B, T, _ = x.shape
    n = B * T
    h = x.reshape(n, D_MODEL)
    pe_all = pe.reshape(-1, D_PLE)
    v_first = None
    wkvs, shifts = [], []
    depth = len(params)
    for l, p in enumerate(params):
        outs = _proj_call(h, w_in_all, l, p, chunked=True, tm=512)
        z, ga, ma, pb = outs[:4]
        z3 = z.reshape(B, T, N_SHIFT)
        vres = None if l == 0 else (outs[4].reshape(B, T, D_A), v_first)
        s0 = jnp.zeros((B, N_HGROUPS, GROUP_W, GROUP_W), F32)
        shift0 = jnp.zeros((B, 1, N_SHIFT), F32)
        ya, v, s_bd = _wkv_chunk_call(z3, vres, s0, shift0, p, bb=8)
        if l == 0:
            v_first = v
        wkvs.append(_diag_blocks(s_bd))
        shifts.append(z3[:, T - 1, :])
        fg = W["final_g"].reshape(1, -1) if l == depth - 1 else None
        res = _merge_call(h, ya.reshape(n, D_A), ga, ma, pb, pe_all, l, p, fg, tm=512)
        h = res[0]
    y = res[1].reshape(B, T, D_MODEL)
    return y, jnp.stack(wkvs), jnp.stack(shifts)


def _forward_sample(x, pe, wkv0, shift0, W, params, w_in_all):
    B = x.shape[0]
    h = x.reshape(B, D_MODEL)
    pe_all = pe.reshape(-1, D_PLE)
    v_first = None
    wkvs, shifts, chunk_vs = [], [], []
    depth = len(params)
    tb = 64
    for l, p in enumerate(params):
        outs = _proj_call(h, w_in_all, l, p, chunked=False, tm=B)
        z, ga, ma, pb, vn = outs[:5]
        vres = None if l == 0 else (outs[5], v_first)
        r, w, k2, v, aa, bbv, bonus = _wkv_pre_call(z, shift0[l], vres, p)
        if l == 0:
            v_first = v
        vecs = [a.reshape(B * N_HEADS, 1, HEAD) for a in (r, w, k2, v, aa, bbv, bonus)]
        gng_t = jnp.tile(p["gn_g"].reshape(N_HEADS, 1, HEAD), (tb // N_HEADS, 1, 1))
        gnb_t = jnp.tile(p["gn_b"].reshape(N_HEADS, 1, HEAD), (tb // N_HEADS, 1, 1))
        ya, s_new = _wkv_step_call(wkv0[l].reshape(B * N_HEADS, HEAD, HEAD), vecs, gng_t, gnb_t, tb)
        wkvs.append(s_new.reshape(B, N_HEADS, HEAD, HEAD))
        shifts.append(z)
        chunk_vs.append(vn.reshape(B, 1, D_B))
        fg = W["final_g"].reshape(1, -1) if l == depth - 1 else None
        res = _merge_call(h, ya.reshape(B, D_A), ga, ma, pb, pe_all, l, p, fg, tm=B)
        h = res[0]
    y = res[1].reshape(B, 1, D_MODEL)
    return y, jnp.stack(wkvs), jnp.stack(shifts), jnp.stack(chunk_vs)


def kernel(x_prompt, x_sample, state_rwkv_wkv, state_rwkv_shift, p_prompt, p_sample, norm_g, w_in, shift_mu, w0, w_up, a0, a_up, vres_down, vres_up, vres_b, k_k, k_a, r_k, gn_g, gn_b, ln_v_g, ln_v_b, w_spatial, b_spatial, w_br_a, w_br_b, w_out, w_ple, w_ple_gate, b_ple_gate, final_g):
    W = dict(norm_g=norm_g, w_in=w_in, shift_mu=shift_mu, w0=w0, w_up=w_up, a0=a0, a_up=a_up,
             vres_down=vres_down, vres_up=vres_up, vres_b=vres_b, k_k=k_k, k_a=k_a, r_k=r_k,
             gn_g=gn_g, gn_b=gn_b, ln_v_g=ln_v_g, ln_v_b=ln_v_b, w_spatial=w_spatial,
             b_spatial=b_spatial, w_br_a=w_br_a, w_br_b=w_br_b, w_out=w_out, w_ple=w_ple,
             w_ple_gate=w_ple_gate, b_ple_gate=b_ple_gate, final_g=final_g)
    params = [_layer_params(W, l) for l in range(w_in.shape[0])]
    w_in_all = w_in.astype(BF16)
    y_p, wkv_p, shift_p = _forward_prompt(x_prompt, p_prompt, W, params, w_in_all)
    y_s, wkv_s, shift_s, chunk_v = _forward_sample(
        x_sample, p_sample, state_rwkv_wkv, state_rwkv_shift, W, params, w_in_all)
    return (y_p, y_s, wkv_p, shift_p, wkv_s, shift_s, chunk_v)
```

```python
import functools

import jax
import jax.numpy as jnp
from jax import lax
from jax.experimental import pallas as pl
from jax.experimental.pallas import tpu as pltpu

D_MODEL = 1024
HEAD = 64
N_HEADS = 8
D_A = N_HEADS * HEAD
D_LORA = 64
D_B = 512
N_GROUPS_B = 4
GROUP_B = D_B // N_GROUPS_B
SPATIAL_CHUNK = 128
D_PLE = 256
N_SHIFT = 3 * D_A + 2 * D_LORA
D_IN = N_SHIFT + D_A + 3 * D_B + 2 * D_MODEL
EPS = 1e-6
GN_EPS = 64e-5
LN_EPS = 1e-5

O_GA = N_SHIFT
O_U = O_GA + D_A
O_VB = O_U + D_B
O_GB = O_VB + D_B
O_MA = O_GB + D_B
O_MB = O_MA + D_MODEL

WKV_CHUNK = 64
HEADS_PER_GROUP = 4
GROUP_W = HEADS_PER_GROUP * HEAD
N_HGROUPS = N_HEADS // HEADS_PER_GROUP
VMEM_LIMIT_BYTES = 56 * 1024 * 1024

F32 = jnp.float32
BF16 = jnp.bfloat16

NN = ((1,), (0,))
NT = ((1,), (1,))
TN = ((0,), (0,))


def _dg(a, b, dims):
    return lax.dot_general(a, b, (dims, ((), ())), preferred_element_type=F32)


def _split(x):
    hi = x.astype(BF16)
    lo = (x - hi.astype(F32)).astype(BF16)
    return hi, lo


def _sigmoid(x):
    return 1.0 / (1.0 + jnp.exp(-x))


def _gelu_tanh(x):
    return 0.5 * x * (1.0 + jnp.tanh(0.7978845608028654 * (x + 0.044715 * (x * x * x))))


def _softplus(x):
    return jnp.maximum(x, 0.0) + jnp.log(1.0 + jnp.exp(-jnp.abs(x)))


def _head_ones(n):
    r = lax.broadcasted_iota(jnp.int32, (n, n), 0) // HEAD
    c = lax.broadcasted_iota(jnp.int32, (n, n), 1) // HEAD
    return jnp.where(r == c, 1.0, 0.0).astype(BF16)


def _headsums(xs, ones_g):
    rows = xs[0].shape[0]
    pieces = []
    for x in xs:
        hi, lo = _split(x)
        pieces += [hi[:, :GROUP_W], lo[:, :GROUP_W], hi[:, GROUP_W:], lo[:, GROUP_W:]]
    o = _dg(jnp.concatenate(pieces, axis=0), ones_g, NN)
    outs = []
    for i in range(len(xs)):
        q = [o[(4 * i + j) * rows:(4 * i + j + 1) * rows] for j in range(4)]
        outs.append(jnp.concatenate([q[0] + q[1], q[2] + q[3]], axis=1))
    return outs


def _spatial_mix(vn, sp_a_ref, sp_b_ref, chunked):
    if not chunked:
        return vn * sp_a_ref[...] + sp_b_ref[...]
    tril = (lax.broadcasted_iota(jnp.int32, (SPATIAL_CHUNK, SPATIAL_CHUNK), 1)
            <= lax.broadcasted_iota(jnp.int32, (SPATIAL_CHUNK, SPATIAL_CHUNK), 0))
    vb16 = vn.astype(BF16)
    wm = [jnp.where(tril, sp_a_ref[g], 0.0).astype(BF16) for g in range(N_GROUPS_B)]
    rows_out = []
    for ci in range(vn.shape[0] // SPATIAL_CHUNK):
        rs = slice(ci * SPATIAL_CHUNK, (ci + 1) * SPATIAL_CHUNK)
        cols = [jnp.dot(wm[g], vb16[rs, g * GROUP_B:(g + 1) * GROUP_B], preferred_element_type=F32)
                + sp_b_ref[:, g:g + 1] for g in range(N_GROUPS_B)]
        rows_out.append(jnp.concatenate(cols, axis=1))
    return jnp.concatenate(rows_out, axis=0)


def _proj_kernel(*refs, has_vres, chunked):
    (x_ref, g_ref, w_ref, lng_ref, lnb_ref, sp_a_ref, sp_b_ref, wbb_ref) = refs[:8]
    rest = refs[8:]
    if has_vres:
        vd_ref, vu_ref, vb_ref = rest[:3]
        rest = rest[3:]
    z_ref, ga_ref, ma_ref, pb_ref = rest[:4]
    rest = rest[4:]
    if not chunked:
        vn_ref = rest[0]
        rest = rest[1:]
    x = x_ref[...]
    xn = x * lax.rsqrt(jnp.mean(x * x, axis=-1, keepdims=True) + EPS) * g_ref[...]
    xb = xn.astype(BF16)

    def seg(lo, hi):
        return jnp.dot(xb, w_ref[:, lo:hi], preferred_element_type=F32)

    z_ref[...] = seg(0, N_SHIFT)
    t = seg(O_GA, O_U)
    ga_ref[...] = t * _sigmoid(t)
    ma_ref[...] = _sigmoid(seg(O_MA, O_MB))
    u = _gelu_tanh(seg(O_U, O_VB))
    vb = _gelu_tanh(seg(O_VB, O_GB))
    mu = jnp.mean(vb, axis=-1, keepdims=True)
    d = vb - mu
    var = jnp.mean(d * d, axis=-1, keepdims=True)
    vn = d * lax.rsqrt(var + LN_EPS) * lng_ref[...] + lnb_ref[...]
    if not chunked:
        vn_ref[...] = vn
    t = seg(O_GB, O_MA)
    yb = u * _spatial_mix(vn, sp_a_ref, sp_b_ref, chunked) * (t * _sigmoid(t))
    br_b = jnp.dot(yb.astype(BF16), wbb_ref[...], preferred_element_type=F32)
    pb_ref[...] = _sigmoid(seg(O_MB, D_IN)) * br_b
    if has_vres:
        (vg_ref,) = rest
        low = jnp.dot(xb, vd_ref[...], preferred_element_type=F32)
        up = jnp.dot(low.astype(BF16), vu_ref[...], preferred_element_type=F32)
        vg_ref[...] = _sigmoid(vb_ref[...] + up)


def _proj_call(x2d, w_in_all, layer, p, chunked, tm):
    n = x2d.shape[0]
    vres = p["vres"]
    has_vres = vres is not None

    def full(a):
        return pl.BlockSpec(a.shape, lambda i: (0,) * a.ndim, pipeline_mode=pl.Buffered(1))

    def rows(w):
        return pl.BlockSpec((tm, w), lambda i: (i, 0))

    w_spec = pl.BlockSpec((None, D_MODEL, D_IN), lambda i: (layer, 0, 0), pipeline_mode=pl.Buffered(1))
    sp_a, sp_b = (p["sp_w"], p["sp_bT"]) if chunked else (p["sp_w0"], p["sp_b0"])
    consts = [p["ln_g"], p["ln_b"], sp_a, sp_b, p["w_br_b"]] + (list(vres) if has_vres else [])
    widths = [N_SHIFT, D_A, D_MODEL, D_MODEL] + ([] if chunked else [D_B]) + ([D_A] if has_vres else [])
    return pl.pallas_call(
        functools.partial(_proj_kernel, has_vres=has_vres, chunked=chunked),
        grid=(n // tm,),
        in_specs=[rows(D_MODEL), full(p["norm_g"]), w_spec] + [full(a) for a in consts],
        out_specs=[rows(w) for w in widths],
        out_shape=[jax.ShapeDtypeStruct((n, w), F32) for w in widths],
        compiler_params=pltpu.CompilerParams(
            dimension_semantics=("arbitrary",), vmem_limit_bytes=VMEM_LIMIT_BYTES),
        name="proj" + ("_c" if chunked else "_s") + ("_vres" if has_vres else ""),
    )(x2d, p["norm_g"], w_in_all, *consts)


def _timemix_pre(z, z_prev, mu, w0, wup_pad, a0, aup_pad, k_k, k_a, r_k, vres, ones_g):
    zs = z + (z_prev - z) * mu
    r = zs[:, 0:D_A]
    k = zs[:, D_A:2 * D_A]
    v = zs[:, 2 * D_A:3 * D_A]
    lora_in = zs[:, 3 * D_A:N_SHIFT]
    wq = w0 + jnp.dot(jnp.tanh(lora_in).astype(BF16), wup_pad, preferred_element_type=F32)
    w = -_softplus(-wq) - 0.5
    lw = -jnp.exp(w)
    a = _sigmoid(a0 + jnp.dot(lora_in.astype(BF16), aup_pad, preferred_element_type=F32))
    if vres is not None:
        vgate, vfirst = vres
        v = v + (vfirst - v) * vgate
    kk = k * k_k
    k2 = k * (1.0 + (a - 1.0) * k_a)
    ss, rk_sum = _headsums([kk * kk, r * k2 * r_k], ones_g)
    kk = kk / jnp.maximum(jnp.sqrt(ss), 1e-12)
    return r, k2, v, -kk, kk * a, lw, rk_sum * v


def _chunk_groups(chains, masks):
    t_idx, s_idx, bdmask = masks
    C = chains[0][0].shape[0]
    n = range(len(chains))
    At, Rt, Bt, Kt, Bh, Kh, V, S, wcc = (list(col) for col in zip(*chains))
    strict = s_idx < t_idx
    incl = s_idx <= t_idx

    def b16(x):
        return x.astype(BF16)

    def bd(x):
        return jnp.where(bdmask, jnp.concatenate([x] * HEADS_PER_GROUP, axis=0), 0)

    ar = [b16(jnp.concatenate([At[i], Rt[i]], axis=0)) for i in n]
    sc_b = [_dg(ar[i], bd(b16(Bt[i])), NT) for i in n]
    sc_k = [_dg(ar[i], bd(b16(Kt[i])), NT) for i in n]
    N = [jnp.where(strict, sc_b[i][:C], 0.0) for i in n]
    Srb = [jnp.where(incl, sc_b[i][C:], 0.0) for i in n]
    M = [jnp.where(strict, sc_k[i][:C], 0.0) for i in n]
    Srk = [jnp.where(incl, sc_k[i][C:], 0.0) for i in n]
    ar_s = [_dg(ar[i], b16(S[i]), NT) for i in n]
    mv = [_dg(b16(jnp.concatenate([M[i], Srk[i]], axis=0)), bd(b16(V[i])), NN) for i in n]
    X = [ar_s[i][:C] + mv[i][:C] for i in n]

    def level_mask(b):
        return ((t_idx // (2 * b)) == (s_idx // (2 * b))) & ((t_idx % (2 * b)) >= b) & ((s_idx % (2 * b)) < b)

    T = [jnp.where(s_idx == t_idx, 1.0, 0.0) + jnp.where(level_mask(1), N[i], 0.0) for i in n]
    b = 2
    while b < C:
        lm = level_mask(b)
        d = [b16(T[i]) for i in n]
        E = [_dg(b16(jnp.where(lm, N[i], 0.0)), bd(d[i]), NN) for i in n]
        F = [_dg(d[i], bd(b16(E[i])), NN) for i in n]
        T = [T[i] + F[i] for i in n]
        b *= 2

    U = [_dg(b16(T[i]), bd(b16(X[i])), NN) for i in n]
    Y = [ar_s[i][C:] + mv[i][C:] + _dg(b16(Srb[i]), bd(b16(U[i])), NN) for i in n]
    upd = [_dg(b16(jnp.concatenate([U[i], V[i]], axis=0)),
               b16(jnp.concatenate([Bh[i], Kh[i]], axis=0)), TN) for i in n]
    return [(Y[i], S[i] * wcc[i] + jnp.where(bdmask, upd[i], 0.0)) for i in n]


def _wkv_chunk_kernel(*refs, bb, has_vres):
    if has_vres:
        (z_ref, vg_ref, vf_ref, s0_ref, sh0_ref, mu_ref, w0_ref, wup_ref, a0_ref, aup_ref,
         kk_ref, ka_ref, rk_ref, gng_ref, gnb_ref, y_ref, v_ref, s_ref, zp_scr) = refs
    else:
        (z_ref, s0_ref, sh0_ref, mu_ref, w0_ref, wup_ref, a0_ref, aup_ref,
         kk_ref, ka_ref, rk_ref, gng_ref, gnb_ref, y_ref, v_ref, s_ref, zp_scr) = refs
    C = z_ref.shape[1]
    c = pl.program_id(1)

    @pl.when(c == 0)
    def _():
        s_ref[...] = s0_ref[...]
        zp_scr[...] = sh0_ref[...]

    ones_g = _head_ones(GROUP_W)
    t_idx = lax.broadcasted_iota(jnp.int32, (C, GROUP_W), 0)
    s_idx = lax.broadcasted_iota(jnp.int32, (C, GROUP_W), 1) % C
    rr = lax.broadcasted_iota(jnp.int32, (GROUP_W, GROUP_W), 0) // HEAD
    cc = lax.broadcasted_iota(jnp.int32, (GROUP_W, GROUP_W), 1) // HEAD
    masks = (t_idx, s_idx, rr == cc)
    tri = jnp.where(lax.broadcasted_iota(jnp.int32, (C, C), 1)
                    <= lax.broadcasted_iota(jnp.int32, (C, C), 0), 1.0, 0.0).astype(BF16)
    row0 = lax.broadcasted_iota(jnp.int32, (C, N_SHIFT), 0) == 0

    chains, bonuses = [], []
    for bi in range(bb):
        z = z_ref[bi]
        z_prev = jnp.where(row0, zp_scr[bi], pltpu.roll(z, 1, 0))
        zp_scr[bi] = z[C - 1:C, :]
        vres = (vg_ref[bi], vf_ref[bi]) if has_vres else None
        r, k2, v, aa, bbv, lw, bonus = _timemix_pre(
            z, z_prev, mu_ref[...], w0_ref[...], wup_ref[...], a0_ref[...], aup_ref[...],
            kk_ref[...], ka_ref[...], rk_ref[...], vres, ones_g)
        v_ref[bi] = v
        bonuses.append(bonus)

        p1 = lw.astype(BF16)
        r1 = lw - p1.astype(F32)
        p2 = r1.astype(BF16)
        p3 = (r1 - p2.astype(F32)).astype(BF16)
        cum = _dg(tri, p1, NN) + (_dg(tri, p2, NN) + _dg(tri, p3, NN))
        cum_c = cum[C - 1:C, :]
        wc = jnp.exp(cum)
        wi = jnp.exp(-cum)
        wrel = jnp.exp(cum_c - cum)
        At = aa * jnp.exp(cum - lw)
        Rt = r * wc
        Bt = bbv * wi
        Kt = k2 * wi
        Bh = bbv * wrel
        Kh = k2 * wrel
        wcc = jnp.exp(cum_c)

        for g in range(N_HGROUPS):
            sl = slice(g * GROUP_W, (g + 1) * GROUP_W)
            chains.append((At[:, sl], Rt[:, sl], Bt[:, sl], Kt[:, sl], Bh[:, sl], Kh[:, sl],
                           v[:, sl], s_ref[bi, g], wcc[:, sl]))

    res = _chunk_groups(chains, masks)

    for bi in range(bb):
        for g in range(N_HGROUPS):
            s_ref[bi, g] = res[bi * N_HGROUPS + g][1]
        y = jnp.concatenate([res[bi * N_HGROUPS + g][0] for g in range(N_HGROUPS)], axis=1)
        d = y - _headsums([y], ones_g)[0] * (1.0 / HEAD)
        var = _headsums([d * d], ones_g)[0] * (1.0 / HEAD)
        y_ref[bi] = d * lax.rsqrt(var + GN_EPS) * gng_ref[...] + gnb_ref[...] + bonuses[bi]


def _wkv_chunk_call(z3, vres, s0_bd, shift0, p, bb):
    B, T, _ = z3.shape
    C = WKV_CHUNK
    has_vres = vres is not None

    def tok(w):
        return pl.BlockSpec((bb, C, w), lambda b, c: (b, c, 0))

    def full(shape):
        return pl.BlockSpec(shape, lambda b, c: (0,) * len(shape))

    s_spec = pl.BlockSpec((bb, N_HGROUPS, GROUP_W, GROUP_W), lambda b, c: (b, 0, 0, 0))
    in_specs = [tok(N_SHIFT)]
    args = [z3]
    if has_vres:
        in_specs += [tok(D_A), tok(D_A)]
        args += list(vres)
    in_specs += [s_spec, pl.BlockSpec((bb, 1, N_SHIFT), lambda b, c: (b, 0, 0))]
    args += [s0_bd, shift0]
    for name in ("mu", "w0", "wup", "a0", "aup", "k_k", "k_a", "r_k", "gn_g", "gn_b"):
        in_specs.append(full(p[name].shape))
        args.append(p[name])
    return pl.pallas_call(
        functools.partial(_wkv_chunk_kernel, bb=bb, has_vres=has_vres),
        grid=(B // bb, T // C),
        in_specs=in_specs,
        out_specs=[tok(D_A), tok(D_A), s_spec],
        out_shape=[jax.ShapeDtypeStruct((B, T, D_A), F32), jax.ShapeDtypeStruct((B, T, D_A), F32),
                   jax.ShapeDtypeStruct(s0_bd.shape, F32)],
        scratch_shapes=[pltpu.VMEM((bb, 1, N_SHIFT), F32)],
        compiler_params=pltpu.CompilerParams(
            dimension_semantics=("arbitrary", "arbitrary"), vmem_limit_bytes=VMEM_LIMIT_BYTES),
        name="wkv_chunk_vres" if has_vres else "wkv_chunk",
    )(*args)


def _wkv_pre_kernel(*refs, has_vres):
    if has_vres:
        (z_ref, zp_ref, vg_ref, vf_ref, mu_ref, w0_ref, wup_ref, a0_ref, aup_ref, kk_ref, ka_ref, rk_ref,
         r_o, w_o, k_o, vt_o, a_o, b_o, bonus_o, v_o) = refs
        vres = (vg_ref[...], vf_ref[...])
    else:
        (z_ref, zp_ref, mu_ref, w0_ref, wup_ref, a0_ref, aup_ref, kk_ref, ka_ref, rk_ref,
         r_o, w_o, k_o, vt_o, a_o, b_o, bonus_o, v_o) = refs
        vres = None
    r, k2, v, aa, bbv, lw, bonus = _timemix_pre(
        z_ref[...], zp_ref[...], mu_ref[...], w0_ref[...], wup_ref[...], a0_ref[...], aup_ref[...],
        kk_ref[...], ka_ref[...], rk_ref[...], vres, _head_ones(GROUP_W))
    r_o[...] = r.T
    w_o[...] = jnp.exp(lw).T
    k_o[...] = k2.T
    vt_o[...] = v.T
    a_o[...] = aa.T
    b_o[...] = bbv.T
    bonus_o[...] = bonus.T
    v_o[...] = v


def _wkv_pre_call(z, z_prev, vres, p):
    n = z.shape[0]
    has_vres = vres is not None
    args = [z, z_prev] + (list(vres) if has_vres else [])
    args += [p[k] for k in ("mu", "w0", "wup", "a0", "aup", "k_k", "k_a", "r_k")]
    return pl.pallas_call(
        functools.partial(_wkv_pre_kernel, has_vres=has_vres),
        out_shape=[jax.ShapeDtypeStruct((D_A, n), F32)] * 7 + [jax.ShapeDtypeStruct((n, D_A), F32)],
        compiler_params=pltpu.CompilerParams(vmem_limit_bytes=VMEM_LIMIT_BYTES),
        name="wkv_pre_vres" if has_vres else "wkv_pre",
    )(*args)


def _wkv_step_kernel(s_ref, r_ref, w_ref, k_ref, v_ref, a_ref, b_ref, bonus_ref, gng_ref, gnb_ref,
                     y_ref, so_ref, y_scr):
    aT = a_ref[...]
    wT = w_ref[...]
    bT = b_ref[...]
    kT = k_ref[...]
    rT = r_ref[...]

    def body(i, carry):
        si = s_ref[i]
        sa = jnp.sum(si * aT, axis=0, keepdims=True)
        s2 = si * wT + sa * bT + v_ref[pl.ds(i, 1), :] * kT
        so_ref[i] = s2
        y_scr[pl.ds(i, 1), :] = jnp.sum(s2 * rT, axis=0, keepdims=True)
        return carry

    lax.fori_loop(0, HEAD, body, 0, unroll=4)
    y = y_scr[...]
    d = y - jnp.mean(y, axis=0, keepdims=True)
    var = jnp.mean(d * d, axis=0, keepdims=True)
    y_ref[...] = d * lax.rsqrt(var + GN_EPS) * gng_ref[...] + gnb_ref[...] + bonus_ref[...]


def _wkv_step_call(s_all, layer, vecs, gng_col, gnb_col):
    n = s_all.shape[-1]
    vec_spec = pl.BlockSpec((HEAD, n), lambda h: (h, 0))
    col_spec = pl.BlockSpec((HEAD, 1), lambda h: (h, 0))
    s_in = pl.BlockSpec((None, None, HEAD, HEAD, n), lambda h: (layer, h, 0, 0, 0))
    s_out = pl.BlockSpec((None, HEAD, HEAD, n), lambda h: (h, 0, 0, 0))
    return pl.pallas_call(
        _wkv_step_kernel,
        grid=(N_HEADS,),
        in_specs=[s_in] + [vec_spec] * 7 + [col_spec, col_spec],
        out_specs=[vec_spec, s_out],
        out_shape=[jax.ShapeDtypeStruct((D_A, n), F32), jax.ShapeDtypeStruct(s_all.shape[1:], F32)],
        scratch_shapes=[pltpu.VMEM((HEAD, n), F32)],
        compiler_params=pltpu.CompilerParams(
            dimension_semantics=("arbitrary",), vmem_limit_bytes=VMEM_LIMIT_BYTES),
        name="wkv_step",
    )(s_all, *vecs, gng_col, gnb_col)


def _merge_kernel(*refs, final):
    (h_ref, ya_ref, ga_ref, ma_ref, pb_ref, p_ref, wba_ref, wout_ref, wple_ref, wpg_ref, bpg_ref) = refs[:11]
    rest = refs[11:]
    if final:
        fg_ref, h_out, y_out = rest
    else:
        (h_out,) = rest
    br_a = jnp.dot((ya_ref[...] * ga_ref[...]).astype(BF16), wba_ref[...], preferred_element_type=F32)
    merged = ma_ref[...] * br_a + pb_ref[...]
    h = h_ref[...] + jnp.dot(merged.astype(BF16), wout_ref[...], preferred_element_type=F32)
    ple = jnp.dot(p_ref[...].astype(BF16), wple_ref[...], preferred_element_type=F32)
    gate = _sigmoid(jnp.dot(h.astype(BF16), wpg_ref[...], preferred_element_type=F32) + bpg_ref[...])
    h = h + gate * ple
    h_out[...] = h
    if final:
        y_out[...] = h * lax.rsqrt(jnp.mean(h * h, axis=-1, keepdims=True) + EPS) * fg_ref[...]


def _merge_call(h, ya, ga, ma, pb, pe_all, layer, wl, final_g, tm):
    n = h.shape[0]
    final = final_g is not None

    def rows(w):
        return pl.BlockSpec((tm, w), lambda i: (i, 0))

    def full(a):
        return pl.BlockSpec(a.shape, lambda i: (0,) * a.ndim, pipeline_mode=pl.Buffered(1))

    row_args = [h, ya, ga, ma, pb]
    pe_spec = pl.BlockSpec((tm, D_PLE), lambda i: (layer * (n // tm) + i, 0))
    w_args = [wl["w_br_a"], wl["w_out"], wl["w_ple"], wl["w_pg"], wl["b_pg"]]
    if final:
        w_args.append(final_g)
    n_out = 2 if final else 1
    return pl.pallas_call(
        functools.partial(_merge_kernel, final=final),
        grid=(n // tm,),
        in_specs=[rows(a.shape[1]) for a in row_args] + [pe_spec] + [full(a) for a in w_args],
        out_specs=[rows(D_MODEL)] * n_out,
        out_shape=[jax.ShapeDtypeStruct((n, D_MODEL), F32)] * n_out,
        compiler_params=pltpu.CompilerParams(
            dimension_semantics=("arbitrary",), vmem_limit_bytes=VMEM_LIMIT_BYTES),
        name="merge" + ("_f" if final else ""),
    )(*row_args, pe_all, *w_args)


def _layer_params(W, l):
    row = lambda a: a.reshape(1, -1)
    zpad = jnp.zeros((D_LORA, D_A), F32)
    p = dict(
        norm_g=row(W["norm_g"][l]),
        ln_g=row(W["ln_v_g"][l]), ln_b=row(W["ln_v_b"][l]),
        mu=row(W["shift_mu"][l]), w0=row(W["w0"][l]), a0=row(W["a0"][l]),
        wup=jnp.concatenate([W["w_up"][l], zpad], axis=0).astype(BF16),
        aup=jnp.concatenate([zpad, W["a_up"][l]], axis=0).astype(BF16),
        k_k=row(W["k_k"][l]), k_a=row(W["k_a"][l]), r_k=row(W["r_k"][l]),
        gn_g=row(W["gn_g"][l]), gn_b=row(W["gn_b"][l]),
        w_br_a=W["w_br_a"][l].astype(BF16), w_br_b=W["w_br_b"][l].astype(BF16),
        w_out=W["w_out"][l].astype(BF16), w_ple=W["w_ple"][l].astype(BF16),
        w_pg=W["w_ple_gate"][l].astype(BF16), b_pg=row(W["b_ple_gate"][l]),
        sp_w=W["w_spatial"][l], sp_bT=W["b_spatial"][l].T,
        sp_w0=jnp.repeat(W["w_spatial"][l][:, 0, 0], GROUP_B).reshape(1, D_B),
        sp_b0=jnp.repeat(W["b_spatial"][l][:, 0], GROUP_B).reshape(1, D_B),
    )
    if l > 0:
        p["vres"] = (W["vres_down"][l - 1].astype(BF16), W["vres_up"][l - 1].astype(BF16),
                     row(W["vres_b"][l - 1]))
    else:
        p["vres"] = None
    return p


def _diag_blocks(s_bd):
    B = s_bd.shape[0]
    blocks = [s_bd[:, :, h * HEAD:(h + 1) * HEAD, h * HEAD:(h + 1) * HEAD] for h in range(HEADS_PER_GROUP)]
    return jnp.stack(blocks, axis=2).reshape(B, N_HEADS, HEAD, HEAD)


def _forward_prompt(x, pe, W, params, w_in_all):
    B, T, _ = x.shape
    n = B * T
    h = x.reshape(n, D_MODEL)
    pe_all = pe.reshape(-1, D_PLE)
    v_first = None
    wkvs, shifts = [], []
    depth = len(params)
    for l, p in enumerate(params):
        outs = _proj_call(h, w_in_all, l, p, chunked=True, tm=512)
        z, ga, ma, pb = outs[:4]
        z3 = z.reshape(B, T, N_SHIFT)
        vres = None if l == 0 else (outs[4].reshape(B, T, D_A), v_first)
        s0 = jnp.zeros((B, N_HGROUPS, GROUP_W, GROUP_W), F32)
        shift0 = jnp.zeros((B, 1, N_SHIFT), F32)
        ya, v, s_bd = _wkv_chunk_call(z3, vres, s0, shift0, p, bb=8)
        if l == 0:
            v_first = v
        wkvs.append(_diag_blocks(s_bd))
        shifts.append(z3[:, T - 1, :])
        fg = W["final_g"].reshape(1, -1) if l == depth - 1 else None
        res = _merge_call(h, ya.reshape(n, D_A), ga, ma, pb, pe_all, l, p, fg, tm=512)
        h = res[0]
    y = res[1].reshape(B, T, D_MODEL)
    return y, jnp.stack(wkvs), jnp.stack(shifts)


def _forward_sample(x, pe, wkv0, shift0, W, params, w_in_all):
    B = x.shape[0]
    h = x.reshape(B, D_MODEL)
    pe_all = pe.reshape(-1, D_PLE)
    s_all = jnp.transpose(wkv0, (0, 2, 3, 4, 1))
    v_first = None
    wkvs, shifts, chunk_vs = [], [], []
    depth = len(params)
    for l, p in enumerate(params):
        outs = _proj_call(h, w_in_all, l, p, chunked=False, tm=B)
        z, ga, ma, pb, vn = outs[:5]
        vres = None if l == 0 else (outs[5], v_first)
        pre = _wkv_pre_call(z, shift0[l], vres, p)
        if l == 0:
            v_first = pre[7]
        ya_t, s_new = _wkv_step_call(s_all, l, pre[:7], p["gn_g"].reshape(D_A, 1), p["gn_b"].reshape(D_A, 1))
        wkvs.append(s_new)
        shifts.append(z)
        chunk_vs.append(vn.reshape(B, 1, D_B))
        fg = W["final_g"].reshape(1, -1) if l == depth - 1 else None
        res = _merge_call(h, ya_t.T, ga, ma, pb, pe_all, l, p, fg, tm=B)
        h = res[0]
    y = res[1].reshape(B, 1, D_MODEL)
    wkv_new = jnp.transpose(jnp.stack(wkvs), (0, 4, 1, 2, 3))
    return y, wkv_new, jnp.stack(shifts), jnp.stack(chunk_vs)


def kernel(x_prompt, x_sample, state_rwkv_wkv, state_rwkv_shift, p_prompt, p_sample, norm_g, w_in, shift_mu, w0, w_up, a0, a_up, vres_down, vres_up, vres_b, k_k, k_a, r_k, gn_g, gn_b, ln_v_g, ln_v_b, w_spatial, b_spatial, w_br_a, w_br_b, w_out, w_ple, w_ple_gate, b_ple_gate, final_g):
    W = dict(norm_g=norm_g, w_in=w_in, shift_mu=shift_mu, w0=w0, w_up=w_up, a0=a0, a_up=a_up,
             vres_down=vres_down, vres_up=vres_up, vres_b=vres_b, k_k=k_k, k_a=k_a, r_k=r_k,
             gn_g=gn_g, gn_b=gn_b, ln_v_g=ln_v_g, ln_v_b=ln_v_b, w_spatial=w_spatial,
             b_spatial=b_spatial, w_br_a=w_br_a, w_br_b=w_br_b, w_out=w_out, w_ple=w_ple,
             w_ple_gate=w_ple_gate, b_ple_gate=b_ple_gate, final_g=final_g)
    params = [_layer_params(W, l) for l in range(w_in.shape[0])]
    w_in_all = w_in.astype(BF16)
    y_p, wkv_p, shift_p = _forward_prompt(x_prompt, p_prompt, W, params, w_in_all)
    y_s, wkv_s, shift_s, chunk_v = _forward_sample(
        x_sample, p_sample, state_rwkv_wkv, state_rwkv_shift, W, params, w_in_all)
    return (y_p, y_s, wkv_p, shift_p, wkv_s, shift_s, chunk_v)
```

```python
import functools

import jax
import jax.numpy as jnp
from jax import lax
from jax.experimental import pallas as pl
from jax.experimental.pallas import tpu as pltpu

D_MODEL = 1024
HEAD = 64
N_HEADS = 8
D_A = N_HEADS * HEAD
D_LORA = 64
D_B = 512
N_GROUPS_B = 4
GROUP_B = D_B // N_GROUPS_B
SPATIAL_CHUNK = 128
D_PLE = 256
N_SHIFT = 3 * D_A + 2 * D_LORA
D_IN = N_SHIFT + D_A + 3 * D_B + 2 * D_MODEL
EPS = 1e-6
GN_EPS = 64e-5
LN_EPS = 1e-5

O_GA = N_SHIFT
O_U = O_GA + D_A
O_VB = O_U + D_B
O_GB = O_VB + D_B
O_MA = O_GB + D_B
O_MB = O_MA + D_MODEL

WKV_CHUNK = 64
HEADS_PER_GROUP = 2
GROUP_W = HEADS_PER_GROUP * HEAD
N_HGROUPS = N_HEADS // HEADS_PER_GROUP
HS_W = 256
VMEM_LIMIT_BYTES = 56 * 1024 * 1024

F32 = jnp.float32
BF16 = jnp.bfloat16

NN = ((1,), (0,))
NT = ((1,), (1,))
TN = ((0,), (0,))


def _dg(a, b, dims):
    return lax.dot_general(a, b, (dims, ((), ())), preferred_element_type=F32)


def _split(x):
    hi = x.astype(BF16)
    lo = (x - hi.astype(F32)).astype(BF16)
    return hi, lo


def _sigmoid(x):
    return 1.0 / (1.0 + jnp.exp(-x))


def _gelu_tanh(x):
    return 0.5 * x * (1.0 + jnp.tanh(0.7978845608028654 * (x + 0.044715 * (x * x * x))))


def _softplus(x):
    return jnp.maximum(x, 0.0) + jnp.log(1.0 + jnp.exp(-jnp.abs(x)))


def _head_ones(n):
    r = lax.broadcasted_iota(jnp.int32, (n, n), 0) // HEAD
    c = lax.broadcasted_iota(jnp.int32, (n, n), 1) // HEAD
    return jnp.where(r == c, 1.0, 0.0).astype(BF16)


def _headsums(xs, ones_g):
    rows = xs[0].shape[0]
    pieces = []
    for x in xs:
        hi, lo = _split(x)
        pieces += [hi[:, :HS_W], lo[:, :HS_W], hi[:, HS_W:], lo[:, HS_W:]]
    o = _dg(jnp.concatenate(pieces, axis=0), ones_g, NN)
    outs = []
    for i in range(len(xs)):
        q = [o[(4 * i + j) * rows:(4 * i + j + 1) * rows] for j in range(4)]
        outs.append(jnp.concatenate([q[0] + q[1], q[2] + q[3]], axis=1))
    return outs


def _spatial_mix(vn, sp_a_ref, sp_b_ref, chunked):
    if not chunked:
        return vn * sp_a_ref[...] + sp_b_ref[...]
    tril = (lax.broadcasted_iota(jnp.int32, (SPATIAL_CHUNK, SPATIAL_CHUNK), 1)
            <= lax.broadcasted_iota(jnp.int32, (SPATIAL_CHUNK, SPATIAL_CHUNK), 0))
    vb16 = vn.astype(BF16)
    wm = [jnp.where(tril, sp_a_ref[g], 0.0).astype(BF16) for g in range(N_GROUPS_B)]
    rows_out = []
    for ci in range(vn.shape[0] // SPATIAL_CHUNK):
        rs = slice(ci * SPATIAL_CHUNK, (ci + 1) * SPATIAL_CHUNK)
        cols = [jnp.dot(wm[g], vb16[rs, g * GROUP_B:(g + 1) * GROUP_B], preferred_element_type=F32)
                + sp_b_ref[:, g:g + 1] for g in range(N_GROUPS_B)]
        rows_out.append(jnp.concatenate(cols, axis=1))
    return jnp.concatenate(rows_out, axis=0)


def _proj_kernel(*refs, has_vres, chunked):
    (x_ref, g_ref, w_ref, lng_ref, lnb_ref, sp_a_ref, sp_b_ref, wbb_ref) = refs[:8]
    rest = refs[8:]
    if has_vres:
        vd_ref, vu_ref, vb_ref = rest[:3]
        rest = rest[3:]
    z_ref, ga_ref, ma_ref, pb_ref = rest[:4]
    rest = rest[4:]
    if not chunked:
        vn_ref = rest[0]
        rest = rest[1:]
    x = x_ref[...]
    xn = x * lax.rsqrt(jnp.mean(x * x, axis=-1, keepdims=True) + EPS) * g_ref[...]
    xb = xn.astype(BF16)

    def seg(lo, hi):
        return jnp.dot(xb, w_ref[:, lo:hi], preferred_element_type=F32)

    z_ref[...] = seg(0, N_SHIFT)
    t = seg(O_GA, O_U)
    ga_ref[...] = t * _sigmoid(t)
    ma_ref[...] = _sigmoid(seg(O_MA, O_MB))
    u = _gelu_tanh(seg(O_U, O_VB))
    vb = _gelu_tanh(seg(O_VB, O_GB))
    mu = jnp.mean(vb, axis=-1, keepdims=True)
    d = vb - mu
    var = jnp.mean(d * d, axis=-1, keepdims=True)
    vn = d * lax.rsqrt(var + LN_EPS) * lng_ref[...] + lnb_ref[...]
    if not chunked:
        vn_ref[...] = vn
    t = seg(O_GB, O_MA)
    yb = u * _spatial_mix(vn, sp_a_ref, sp_b_ref, chunked) * (t * _sigmoid(t))
    br_b = jnp.dot(yb.astype(BF16), wbb_ref[...], preferred_element_type=F32)
    pb_ref[...] = _sigmoid(seg(O_MB, D_IN)) * br_b
    if has_vres:
        (vg_ref,) = rest
        low = jnp.dot(xb, vd_ref[...], preferred_element_type=F32)
        up = jnp.dot(low.astype(BF16), vu_ref[...], preferred_element_type=F32)
        vg_ref[...] = _sigmoid(vb_ref[...] + up)


def _proj_call(x2d, w_in_all, layer, p, chunked, tm):
    n = x2d.shape[0]
    vres = p["vres"]
    has_vres = vres is not None

    def full(a):
        return pl.BlockSpec(a.shape, lambda i: (0,) * a.ndim, pipeline_mode=pl.Buffered(1))

    def rows(w):
        return pl.BlockSpec((tm, w), lambda i: (i, 0))

    w_spec = pl.BlockSpec((None, D_MODEL, D_IN), lambda i: (layer, 0, 0), pipeline_mode=pl.Buffered(1))
    sp_a, sp_b = (p["sp_w"], p["sp_bT"]) if chunked else (p["sp_w0"], p["sp_b0"])
    consts = [p["ln_g"], p["ln_b"], sp_a, sp_b, p["w_br_b"]] + (list(vres) if has_vres else [])
    widths = [N_SHIFT, D_A, D_MODEL, D_MODEL] + ([] if chunked else [D_B]) + ([D_A] if has_vres else [])
    return pl.pallas_call(
        functools.partial(_proj_kernel, has_vres=has_vres, chunked=chunked),
        grid=(n // tm,),
        in_specs=[rows(D_MODEL), full(p["norm_g"]), w_spec] + [full(a) for a in consts],
        out_specs=[rows(w) for w in widths],
        out_shape=[jax.ShapeDtypeStruct((n, w), F32) for w in widths],
        compiler_params=pltpu.CompilerParams(
            dimension_semantics=("arbitrary",), vmem_limit_bytes=VMEM_LIMIT_BYTES),
        name="proj" + ("_c" if chunked else "_s") + ("_vres" if has_vres else ""),
    )(x2d, p["norm_g"], w_in_all, *consts)


def _timemix_pre(z, z_prev, mu, w0, wup_pad, a0, aup_pad, k_k, k_a, r_k, vres, ones_g):
    zs = z + (z_prev - z) * mu
    r = zs[:, 0:D_A]
    k = zs[:, D_A:2 * D_A]
    v = zs[:, 2 * D_A:3 * D_A]
    lora_in = zs[:, 3 * D_A:N_SHIFT]
    wq = w0 + jnp.dot(jnp.tanh(lora_in).astype(BF16), wup_pad, preferred_element_type=F32)
    w = -_softplus(-wq) - 0.5
    lw = -jnp.exp(w)
    a = _sigmoid(a0 + jnp.dot(lora_in.astype(BF16), aup_pad, preferred_element_type=F32))
    if vres is not None:
        vgate, vfirst = vres
        v = v + (vfirst - v) * vgate
    kk = k * k_k
    k2 = k * (1.0 + (a - 1.0) * k_a)
    ss, rk_sum = _headsums([kk * kk, r * k2 * r_k], ones_g)
    kk = kk / jnp.maximum(jnp.sqrt(ss), 1e-12)
    return r, k2, v, -kk, kk * a, lw, rk_sum * v


def _chunk_groups(chains, masks, fillers):
    t_idx, s_idx, bdmask = masks
    fillers = list(fillers)

    def fill():
        if fillers:
            fillers.pop(0)()

    C = chains[0][0].shape[0]
    n = range(len(chains))
    At, Rt, Bt, Kt, Bh, Kh, V, S, wcc = (list(col) for col in zip(*chains))
    strict = s_idx < t_idx
    incl = s_idx <= t_idx

    def b16(x):
        return x.astype(BF16)

    def bd(x):
        return jnp.where(bdmask, jnp.concatenate([x] * HEADS_PER_GROUP, axis=0), 0)

    def bd_t(x):
        return jnp.where(bdmask, jnp.concatenate([x] * HEADS_PER_GROUP, axis=0).T, 0.0).astype(BF16)

    ar = [b16(jnp.concatenate([At[i], Rt[i]], axis=0)) for i in n]
    sc_b = [_dg(ar[i], bd_t(Bt[i]), NN) for i in n]
    fill()
    sc_k = [_dg(ar[i], bd_t(Kt[i]), NN) for i in n]
    fill()
    N = [jnp.where(strict, sc_b[i][:C], 0.0) for i in n]
    Srb = [jnp.where(incl, sc_b[i][C:], 0.0) for i in n]
    M = [jnp.where(strict, sc_k[i][:C], 0.0) for i in n]
    Srk = [jnp.where(incl, sc_k[i][C:], 0.0) for i in n]
    ar_s = [_dg(ar[i], b16(S[i].T), NN) for i in n]
    fill()
    mv = [_dg(b16(jnp.concatenate([M[i], Srk[i]], axis=0)), bd(b16(V[i])), NN) for i in n]
    X = [ar_s[i][:C] + mv[i][:C] for i in n]

    def level_mask(b):
        return ((t_idx // (2 * b)) == (s_idx // (2 * b))) & ((t_idx % (2 * b)) >= b) & ((s_idx % (2 * b)) < b)

    T = [jnp.where(s_idx == t_idx, 1.0, 0.0) + jnp.where(level_mask(1), N[i], 0.0) for i in n]
    b = 2
    while b < C:
        lm = level_mask(b)
        d = [b16(T[i]) for i in n]
        E = [_dg(b16(jnp.where(lm, N[i], 0.0)), bd(d[i]), NN) for i in n]
        fill()
        F = [_dg(d[i], bd(b16(E[i])), NN) for i in n]
        fill()
        T = [T[i] + F[i] for i in n]
        b *= 2

    U = [_dg(b16(T[i]), bd(b16(X[i])), NN) for i in n]
    fill()
    Y = [ar_s[i][C:] + mv[i][C:] + _dg(b16(Srb[i]), bd(b16(U[i])), NN) for i in n]
    upd = [_dg(b16(jnp.concatenate([U[i], V[i]], axis=0)),
               b16(jnp.concatenate([Bh[i], Kh[i]], axis=0)), TN) for i in n]
    while fillers:
        fill()
    return [(Y[i], S[i] * wcc[i] + jnp.where(bdmask, upd[i], 0.0)) for i in n]


def _front_kernel(*refs, bb, has_vres):
    (x_ref, g_ref, w_ref, lng_ref, lnb_ref, sp_a_ref, sp_b_ref, wbb_ref) = refs[:8]
    rest = refs[8:]
    if has_vres:
        vd_ref, vu_ref, vb_ref, vf_ref = rest[:4]
        rest = rest[4:]
    (s0_ref, sh0_ref, mu_ref, w0_ref, wup_ref, a0_ref, aup_ref, kk_ref, ka_ref, rk_ref, gng_ref, gnb_ref,
     ga_ref, ma_ref, pb_ref, y_ref, v_ref, zl_ref, s_ref, zp_scr, vn_scr) = rest
    C = x_ref.shape[1]
    rows = bb * C
    c = pl.program_id(1)

    @pl.when(c == 0)
    def _():
        s_ref[...] = s0_ref[...]
        zp_scr[...] = sh0_ref[...]
        vn_scr[...] = jnp.zeros(vn_scr.shape, F32)

    x = x_ref[...].reshape(rows, D_MODEL)
    xn = x * lax.rsqrt(jnp.mean(x * x, axis=-1, keepdims=True) + EPS) * g_ref[...]
    xb = xn.astype(BF16)

    def seg(lo, hi):
        return jnp.dot(xb, w_ref[:, lo:hi], preferred_element_type=F32)

    z_all = seg(0, N_SHIFT)
    if has_vres:
        low = jnp.dot(xb, vd_ref[...], preferred_element_type=F32)
        vgate = _sigmoid(vb_ref[...] + jnp.dot(low.astype(BF16), vu_ref[...], preferred_element_type=F32))

    ones_g = _head_ones(HS_W)
    t_idx = lax.broadcasted_iota(jnp.int32, (C, GROUP_W), 0)
    s_idx = lax.broadcasted_iota(jnp.int32, (C, GROUP_W), 1) % C
    rr = lax.broadcasted_iota(jnp.int32, (GROUP_W, GROUP_W), 0) // HEAD
    cc = lax.broadcasted_iota(jnp.int32, (GROUP_W, GROUP_W), 1) // HEAD
    masks = (t_idx, s_idx, rr == cc)
    tri = jnp.where(lax.broadcasted_iota(jnp.int32, (C, C), 1)
                    <= lax.broadcasted_iota(jnp.int32, (C, C), 0), 1.0, 0.0).astype(BF16)
    row0 = lax.broadcasted_iota(jnp.int32, (C, N_SHIFT), 0) == 0

    chains, bonuses = [], []
    for bi in range(bb):
        z = z_all[bi * C:(bi + 1) * C]
        z_prev = jnp.where(row0, zp_scr[bi], pltpu.roll(z, 1, 0))
        zp_scr[bi] = z[C - 1:C, :]
        zl_ref[bi] = z[C - 1:C, :]
        vres = (vgate[bi * C:(bi + 1) * C], vf_ref[bi]) if has_vres else None
        r, k2, v, aa, bbv, lw, bonus = _timemix_pre(
            z, z_prev, mu_ref[...], w0_ref[...], wup_ref[...], a0_ref[...], aup_ref[...],
            kk_ref[...], ka_ref[...], rk_ref[...], vres, ones_g)
        v_ref[bi] = v
        bonuses.append(bonus)

        p1 = lw.astype(BF16)
        r1 = lw - p1.astype(F32)
        p2 = r1.astype(BF16)
        p3 = (r1 - p2.astype(F32)).astype(BF16)
        cum = _dg(tri, p1, NN) + (_dg(tri, p2, NN) + _dg(tri, p3, NN))
        cum_c = cum[C - 1:C, :]
        wc = jnp.exp(cum)
        wi = jnp.exp(-cum)
        wrel = jnp.exp(cum_c - cum)
        At = aa * jnp.exp(cum - lw)
        Rt = r * wc
        Bt = bbv * wi
        Kt = k2 * wi
        Bh = bbv * wrel
        Kh = k2 * wrel
        wcc = jnp.exp(cum_c)

        for g in range(N_HGROUPS):
            sl = slice(g * GROUP_W, (g + 1) * GROUP_W)
            chains.append((At[:, sl], Rt[:, sl], Bt[:, sl], Kt[:, sl], Bh[:, sl], Kh[:, sl],
                           v[:, sl], s_ref[bi, g], wcc[:, sl]))

    half = c % 2
    box = {}

    def f_ga():
        t = seg(O_GA, O_U)
        ga_ref[...] = (t * _sigmoid(t)).reshape(bb, C, D_A)

    def f_ma(lo, hi):
        def f():
            ma_ref[:, :, lo:hi] = _sigmoid(seg(O_MA + lo, O_MA + hi)).reshape(bb, C, hi - lo)
        return f

    def f_u():
        box["u"] = _gelu_tanh(seg(O_U, O_VB))

    def f_vn():
        vb = _gelu_tanh(seg(O_VB, O_GB))
        d = vb - jnp.mean(vb, axis=-1, keepdims=True)
        var = jnp.mean(d * d, axis=-1, keepdims=True)
        box["vn"] = d * lax.rsqrt(var + LN_EPS) * lng_ref[...] + lnb_ref[...]

    def f_mix():
        vn = box["vn"]
        srow = lax.broadcasted_iota(jnp.int32, (C, SPATIAL_CHUNK), 0) + half * C
        scol = lax.broadcasted_iota(jnp.int32, (C, SPATIAL_CHUNK), 1)
        keep = scol <= srow
        wm = [jnp.where(keep, sp_a_ref[g, pl.ds(half * C, C), :], 0.0).astype(BF16) for g in range(N_GROUPS_B)]
        bias = sp_b_ref[pl.ds(half * C, C), :]
        outs = []
        for bi in range(bb):
            cur = vn[bi * C:(bi + 1) * C]
            first = jnp.where(half == 0, cur, vn_scr[bi])
            both = jnp.concatenate([first, cur], axis=0).astype(BF16)
            vn_scr[bi] = cur
            cols = [jnp.dot(wm[g], both[:, g * GROUP_B:(g + 1) * GROUP_B], preferred_element_type=F32)
                    + bias[:, g:g + 1] for g in range(N_GROUPS_B)]
            outs.append(jnp.concatenate(cols, axis=1))
        box["mixed"] = jnp.concatenate(outs, axis=0)

    def f_yb():
        t = seg(O_GB, O_MA)
        box["yb"] = (box["u"] * box["mixed"] * (t * _sigmoid(t))).astype(BF16)

    def f_pb(lo, hi):
        def f():
            br_b = jnp.dot(box["yb"], wbb_ref[:, lo:hi], preferred_element_type=F32)
            pb_ref[:, :, lo:hi] = (_sigmoid(seg(O_MB + lo, O_MB + hi)) * br_b).reshape(bb, C, hi - lo)
        return f

    hm = D_MODEL // 2
    fillers = [f_ga, f_ma(0, hm), f_ma(hm, D_MODEL), f_u, f_vn, f_mix, f_yb, f_pb(0, hm), f_pb(hm, D_MODEL)]
    res = _chunk_groups(chains, masks, fillers)

    for bi in range(bb):
        for g in range(N_HGROUPS):
            s_ref[bi, g] = res[bi * N_HGROUPS + g][1]
        y = jnp.concatenate([res[bi * N_HGROUPS + g][0] for g in range(N_HGROUPS)], axis=1)
        d = y - _headsums([y], ones_g)[0] * (1.0 / HEAD)
        var = _headsums([d * d], ones_g)[0] * (1.0 / HEAD)
        y_ref[bi] = d * lax.rsqrt(var + GN_EPS) * gng_ref[...] + gnb_ref[...] + bonuses[bi]


def _front_call(x3, w_in_all, layer, p, v_first, s0_bd, shift0, bb):
    B, T, _ = x3.shape
    C = WKV_CHUNK
    vres = p["vres"]
    has_vres = vres is not None

    def tok(w):
        return pl.BlockSpec((bb, C, w), lambda b, c: (b, c, 0))

    def full(a):
        return pl.BlockSpec(a.shape, lambda b, c: (0,) * a.ndim, pipeline_mode=pl.Buffered(1))

    w_spec = pl.BlockSpec((None, D_MODEL, D_IN), lambda b, c: (layer, 0, 0), pipeline_mode=pl.Buffered(1))
    s_spec = pl.BlockSpec((bb, N_HGROUPS, GROUP_W, GROUP_W), lambda b, c: (b, 0, 0, 0))
    row_spec = pl.BlockSpec((bb, 1, N_SHIFT), lambda b, c: (b, 0, 0))
    consts = [p["ln_g"], p["ln_b"], p["sp_w"], p["sp_bT"], p["w_br_b"]]
    in_specs = [tok(D_MODEL), full(p["norm_g"]), w_spec] + [full(a) for a in consts]
    args = [x3, p["norm_g"], w_in_all] + consts
    if has_vres:
        in_specs += [full(a) for a in vres] + [tok(D_A)]
        args += list(vres) + [v_first]
    in_specs += [s_spec, row_spec]
    args += [s0_bd, shift0]
    for name in ("mu", "w0", "wup", "a0", "aup", "k_k", "k_a", "r_k", "gn_g", "gn_b"):
        in_specs.append(full(p[name]))
        args.append(p[name])
    widths = [D_A, D_MODEL, D_MODEL, D_A, D_A]
    return pl.pallas_call(
        functools.partial(_front_kernel, bb=bb, has_vres=has_vres),
        grid=(B // bb, T // C),
        in_specs=in_specs,
        out_specs=[tok(w) for w in widths] + [row_spec, s_spec],
        out_shape=[jax.ShapeDtypeStruct((B, T, w), F32) for w in widths]
        + [jax.ShapeDtypeStruct((B, 1, N_SHIFT), F32), jax.ShapeDtypeStruct(s0_bd.shape, F32)],
        scratch_shapes=[pltpu.VMEM((bb, 1, N_SHIFT), F32), pltpu.VMEM((bb, C, D_B), F32)],
        compiler_params=pltpu.CompilerParams(
            dimension_semantics=("arbitrary", "arbitrary"), vmem_limit_bytes=VMEM_LIMIT_BYTES),
        name="front_vres" if has_vres else "front",
    )(*args)


def _wkv_pre_kernel(*refs, has_vres):
    if has_vres:
        (z_ref, zp_ref, vg_ref, vf_ref, mu_ref, w0_ref, wup_ref, a0_ref, aup_ref, kk_ref, ka_ref, rk_ref,
         r_o, w_o, k_o, vt_o, a_o, b_o, bonus_o, v_o) = refs
        vres = (vg_ref[...], vf_ref[...])
    else:
        (z_ref, zp_ref, mu_ref, w0_ref, wup_ref, a0_ref, aup_ref, kk_ref, ka_ref, rk_ref,
         r_o, w_o, k_o, vt_o, a_o, b_o, bonus_o, v_o) = refs
        vres = None
    r, k2, v, aa, bbv, lw, bonus = _timemix_pre(
        z_ref[...], zp_ref[...], mu_ref[...], w0_ref[...], wup_ref[...], a0_ref[...], aup_ref[...],
        kk_ref[...], ka_ref[...], rk_ref[...], vres, _head_ones(HS_W))
    r_o[...] = r.T
    w_o[...] = jnp.exp(lw).T
    k_o[...] = k2.T
    vt_o[...] = v.T
    a_o[...] = aa.T
    b_o[...] = bbv.T
    bonus_o[...] = bonus.T
    v_o[...] = v


def _wkv_pre_call(z, z_prev, vres, p):
    n = z.shape[0]
    has_vres = vres is not None
    args = [z, z_prev] + (list(vres) if has_vres else [])
    args += [p[k] for k in ("mu", "w0", "wup", "a0", "aup", "k_k", "k_a", "r_k")]
    return pl.pallas_call(
        functools.partial(_wkv_pre_kernel, has_vres=has_vres),
        out_shape=[jax.ShapeDtypeStruct((D_A, n), F32)] * 7 + [jax.ShapeDtypeStruct((n, D_A), F32)],
        compiler_params=pltpu.CompilerParams(vmem_limit_bytes=VMEM_LIMIT_BYTES),
        name="wkv_pre_vres" if has_vres else "wkv_pre",
    )(*args)


def _wkv_step_kernel(s_ref, r_ref, w_ref, k_ref, v_ref, a_ref, b_ref, bonus_ref, gng_ref, gnb_ref,
                     y_ref, so_ref, y_scr):
    aT = a_ref[...]
    wT = w_ref[...]
    bT = b_ref[...]
    kT = k_ref[...]
    rT = r_ref[...]

    def body(i, carry):
        si = s_ref[i]
        sa = jnp.sum(si * aT, axis=0, keepdims=True)
        s2 = si * wT + sa * bT + v_ref[pl.ds(i, 1), :] * kT
        so_ref[i] = s2
        y_scr[pl.ds(i, 1), :] = jnp.sum(s2 * rT, axis=0, keepdims=True)
        return carry

    lax.fori_loop(0, HEAD, body, 0, unroll=4)
    y = y_scr[...]
    d = y - jnp.mean(y, axis=0, keepdims=True)
    var = jnp.mean(d * d, axis=0, keepdims=True)
    y_ref[...] = d * lax.rsqrt(var + GN_EPS) * gng_ref[...] + gnb_ref[...] + bonus_ref[...]


def _wkv_step_call(s_all, layer, vecs, gng_col, gnb_col):
    n = s_all.shape[-1]
    vec_spec = pl.BlockSpec((HEAD, n), lambda h: (h, 0))
    col_spec = pl.BlockSpec((HEAD, 1), lambda h: (h, 0))
    s_in = pl.BlockSpec((None, None, HEAD, HEAD, n), lambda h: (layer, h, 0, 0, 0))
    s_out = pl.BlockSpec((None, HEAD, HEAD, n), lambda h: (h, 0, 0, 0))
    return pl.pallas_call(
        _wkv_step_kernel,
        grid=(N_HEADS,),
        in_specs=[s_in] + [vec_spec] * 7 + [col_spec, col_spec],
        out_specs=[vec_spec, s_out],
        out_shape=[jax.ShapeDtypeStruct((D_A, n), F32), jax.ShapeDtypeStruct(s_all.shape[1:], F32)],
        scratch_shapes=[pltpu.VMEM((HEAD, n), F32)],
        compiler_params=pltpu.CompilerParams(
            dimension_semantics=("arbitrary",), vmem_limit_bytes=VMEM_LIMIT_BYTES),
        name="wkv_step",
    )(s_all, *vecs, gng_col, gnb_col)


def _merge_kernel(*refs, final):
    (h_ref, ya_ref, ga_ref, ma_ref, pb_ref, p_ref, wba_ref, wout_ref, wple_ref, wpg_ref, bpg_ref) = refs[:11]
    rest = refs[11:]
    if final:
        fg_ref, h_out, y_out = rest
    else:
        (h_out,) = rest
    br_a = jnp.dot((ya_ref[...] * ga_ref[...]).astype(BF16), wba_ref[...], preferred_element_type=F32)
    merged = ma_ref[...] * br_a + pb_ref[...]
    h = h_ref[...] + jnp.dot(merged.astype(BF16), wout_ref[...], preferred_element_type=F32)
    ple = jnp.dot(p_ref[...].astype(BF16), wple_ref[...], preferred_element_type=F32)
    gate = _sigmoid(jnp.dot(h.astype(BF16), wpg_ref[...], preferred_element_type=F32) + bpg_ref[...])
    h = h + gate * ple
    h_out[...] = h
    if final:
        y_out[...] = h * lax.rsqrt(jnp.mean(h * h, axis=-1, keepdims=True) + EPS) * fg_ref[...]


def _merge_call(h, ya, ga, ma, pb, pe_all, layer, wl, final_g, tm):
    n = h.shape[0]
    final = final_g is not None

    def rows(w):
        return pl.BlockSpec((tm, w), lambda i: (i, 0))

    def full(a):
        return pl.BlockSpec(a.shape, lambda i: (0,) * a.ndim, pipeline_mode=pl.Buffered(1))

    row_args = [h, ya, ga, ma, pb]
    pe_spec = pl.BlockSpec((tm, D_PLE), lambda i: (layer * (n // tm) + i, 0))
    w_args = [wl["w_br_a"], wl["w_out"], wl["w_ple"], wl["w_pg"], wl["b_pg"]]
    if final:
        w_args.append(final_g)
    n_out = 2 if final else 1
    return pl.pallas_call(
        functools.partial(_merge_kernel, final=final),
        grid=(n // tm,),
        in_specs=[rows(a.shape[1]) for a in row_args] + [pe_spec] + [full(a) for a in w_args],
        out_specs=[rows(D_MODEL)] * n_out,
        out_shape=[jax.ShapeDtypeStruct((n, D_MODEL), F32)] * n_out,
        compiler_params=pltpu.CompilerParams(
            dimension_semantics=("arbitrary",), vmem_limit_bytes=VMEM_LIMIT_BYTES),
        name="merge" + ("_f" if final else ""),
    )(*row_args, pe_all, *w_args)


def _layer_params(W, l):
    row = lambda a: a.reshape(1, -1)
    zpad = jnp.zeros((D_LORA, D_A), F32)
    p = dict(
        norm_g=row(W["norm_g"][l]),
        ln_g=row(W["ln_v_g"][l]), ln_b=row(W["ln_v_b"][l]),
        mu=row(W["shift_mu"][l]), w0=row(W["w0"][l]), a0=row(W["a0"][l]),
        wup=jnp.concatenate([W["w_up"][l], zpad], axis=0).astype(BF16),
        aup=jnp.concatenate([zpad, W["a_up"][l]], axis=0).astype(BF16),
        k_k=row(W["k_k"][l]), k_a=row(W["k_a"][l]), r_k=row(W["r_k"][l]),
        gn_g=row(W["gn_g"][l]), gn_b=row(W["gn_b"][l]),
        w_br_a=W["w_br_a"][l].astype(BF16), w_br_b=W["w_br_b"][l].astype(BF16),
        w_out=W["w_out"][l].astype(BF16), w_ple=W["w_ple"][l].astype(BF16),
        w_pg=W["w_ple_gate"][l].astype(BF16), b_pg=row(W["b_ple_gate"][l]),
        sp_w=W["w_spatial"][l], sp_bT=W["b_spatial"][l].T,
        sp_w0=jnp.repeat(W["w_spatial"][l][:, 0, 0], GROUP_B).reshape(1, D_B),
        sp_b0=jnp.repeat(W["b_spatial"][l][:, 0], GROUP_B).reshape(1, D_B),
    )
    if l > 0:
        p["vres"] = (W["vres_down"][l - 1].astype(BF16), W["vres_up"][l - 1].astype(BF16),
                     row(W["vres_b"][l - 1]))
    else:
        p["vres"] = None
    return p


def _diag_blocks(s_bd):
    B = s_bd.shape[0]
    blocks = [s_bd[:, :, h * HEAD:(h + 1) * HEAD, h * HEAD:(h + 1) * HEAD] for h in range(HEADS_PER_GROUP)]
    return jnp.stack(blocks, axis=2).reshape(B, N_HEADS, HEAD, HEAD)


def _forward_prompt(x, pe, W, params, w_in_all):
    B, T, _ = x.shape
    n = B * T
    h = x.reshape(n, D_MODEL)
    pe_all = pe.reshape(-1, D_PLE)
    v_first = None
    wkvs, shifts = [], []
    depth = len(params)
    s0 = jnp.zeros((B, N_HGROUPS, GROUP_W, GROUP_W), F32)
    shift0 = jnp.zeros((B, 1, N_SHIFT), F32)
    for l, p in enumerate(params):
        ga, ma, pb, ya, v, z_last, s_bd = _front_call(
            h.reshape(B, T, D_MODEL), w_in_all, l, p, v_first, s0, shift0, bb=8)
        if l == 0:
            v_first = v
        wkvs.append(_diag_blocks(s_bd))
        shifts.append(z_last.reshape(B, N_SHIFT))
        fg = W["final_g"].reshape(1, -1) if l == depth - 1 else None
        res = _merge_call(h, ya.reshape(n, D_A), ga.reshape(n, D_A), ma.reshape(n, D_MODEL),
                          pb.reshape(n, D_MODEL), pe_all, l, p, fg, tm=512)
        h = res[0]
    y = res[1].reshape(B, T, D_MODEL)
    return y, jnp.stack(wkvs), jnp.stack(shifts)


def _forward_sample(x, pe, wkv0, shift0, W, params, w_in_all):
    B = x.shape[0]
    h = x.reshape(B, D_MODEL)
    pe_all = pe.reshape(-1, D_PLE)
    s_all = jnp.transpose(wkv0, (0, 2, 3, 4, 1))
    v_first = None
    wkvs, shifts, chunk_vs = [], [], []
    depth = len(params)
    for l, p in enumerate(params):
        outs = _proj_call(h, w_in_all, l, p, chunked=False, tm=B)
        z, ga, ma, pb, vn = outs[:5]
        vres = None if l == 0 else (outs[5], v_first)
        pre = _wkv_pre_call(z, shift0[l], vres, p)
        if l == 0:
            v_first = pre[7]
        ya_t, s_new = _wkv_step_call(s_all, l, pre[:7], p["gn_g"].reshape(D_A, 1), p["gn_b"].reshape(D_A, 1))
        wkvs.append(s_new)
        shifts.append(z)
        chunk_vs.append(vn.reshape(B, 1, D_B))
        fg = W["final_g"].reshape(1, -1) if l == depth - 1 else None
        res = _merge_call(h, ya_t.T, ga, ma, pb, pe_all, l, p, fg, tm=B)
        h = res[0]
    y = res[1].reshape(B, 1, D_MODEL)
    wkv_new = jnp.transpose(jnp.stack(wkvs), (0, 4, 1, 2, 3))
    return y, wkv_new, jnp.stack(shifts), jnp.stack(chunk_vs)


def kernel(x_prompt, x_sample, state_rwkv_wkv, state_rwkv_shift, p_prompt, p_sample, norm_g, w_in, shift_mu, w0, w_up, a0, a_up, vres_down, vres_up, vres_b, k_k, k_a, r_k, gn_g, gn_b, ln_v_g, ln_v_b, w_spatial, b_spatial, w_br_a, w_br_b, w_out, w_ple, w_ple_gate, b_ple_gate, final_g):
    W = dict(norm_g=norm_g, w_in=w_in, shift_mu=shift_mu, w0=w0, w_up=w_up, a0=a0, a_up=a_up,
             vres_down=vres_down, vres_up=vres_up, vres_b=vres_b, k_k=k_k, k_a=k_a, r_k=r_k,
             gn_g=gn_g, gn_b=gn_b, ln_v_g=ln_v_g, ln_v_b=ln_v_b, w_spatial=w_spatial,
             b_spatial=b_spatial, w_br_a=w_br_a, w_br_b=w_br_b, w_out=w_out, w_ple=w_ple,
             w_ple_gate=w_ple_gate, b_ple_gate=b_ple_gate, final_g=final_g)
    params = [_layer_params(W, l) for l in range(w_in.shape[0])]
    w_in_all = w_in.astype(BF16)
    y_p, wkv_p, shift_p = _forward_prompt(x_prompt, p_prompt, W, params, w_in_all)
    y_s, wkv_s, shift_s, chunk_v = _forward_sample(
        x_sample, p_sample, state_rwkv_wkv, state_rwkv_shift, W, params, w_in_all)
    return (y_p, y_s, wkv_p, shift_p, wkv_s, shift_s, chunk_v)
```

```python
import functools

import jax
import jax.numpy as jnp
from jax import lax
from jax.experimental import pallas as pl
from jax.experimental.pallas import tpu as pltpu

D_MODEL = 1024
HEAD = 64
N_HEADS = 8
D_A = N_HEADS * HEAD
D_LORA = 64
D_B = 512
N_GROUPS_B = 4
GROUP_B = D_B // N_GROUPS_B
SPATIAL_CHUNK = 128
D_PLE = 256
N_SHIFT = 3 * D_A + 2 * D_LORA
D_IN = N_SHIFT + D_A + 3 * D_B + 2 * D_MODEL
EPS = 1e-6
GN_EPS = 64e-5
LN_EPS = 1e-5

O_GA = N_SHIFT
O_U = O_GA + D_A
O_VB = O_U + D_B
O_GB = O_VB + D_B
O_MA = O_GB + D_B
O_MB = O_MA + D_MODEL

WKV_CHUNK = 64
HEADS_PER_GROUP = 2
GROUP_W = HEADS_PER_GROUP * HEAD
N_HGROUPS = N_HEADS // HEADS_PER_GROUP
HS_W = 256
VMEM_LIMIT_BYTES = 56 * 1024 * 1024

F32 = jnp.float32
BF16 = jnp.bfloat16

NN = ((1,), (0,))
NT = ((1,), (1,))
TN = ((0,), (0,))


def _dg(a, b, dims):
    return lax.dot_general(a, b, (dims, ((), ())), preferred_element_type=F32)


def _split(x):
    hi = x.astype(BF16)
    lo = (x - hi.astype(F32)).astype(BF16)
    return hi, lo


def _sigmoid(x):
    return 1.0 / (1.0 + jnp.exp(-x))


def _gelu_tanh(x):
    return 0.5 * x * (1.0 + jnp.tanh(0.7978845608028654 * (x + 0.044715 * (x * x * x))))


def _softplus(x):
    return jnp.maximum(x, 0.0) + jnp.log(1.0 + jnp.exp(-jnp.abs(x)))


def _head_ones(n):
    r = lax.broadcasted_iota(jnp.int32, (n, n), 0) // HEAD
    c = lax.broadcasted_iota(jnp.int32, (n, n), 1) // HEAD
    return jnp.where(r == c, 1.0, 0.0).astype(BF16)


def _headsums(xs, ones_g):
    rows = xs[0].shape[0]
    pieces = []
    for x in xs:
        hi, lo = _split(x)
        pieces += [hi[:, :HS_W], lo[:, :HS_W], hi[:, HS_W:], lo[:, HS_W:]]
    o = _dg(jnp.concatenate(pieces, axis=0), ones_g, NN)
    outs = []
    for i in range(len(xs)):
        q = [o[(4 * i + j) * rows:(4 * i + j + 1) * rows] for j in range(4)]
        outs.append(jnp.concatenate([q[0] + q[1], q[2] + q[3]], axis=1))
    return outs


def _spatial_mix(vn, sp_a_ref, sp_b_ref, chunked):
    if not chunked:
        return vn * sp_a_ref[...] + sp_b_ref[...]
    tril = (lax.broadcasted_iota(jnp.int32, (SPATIAL_CHUNK, SPATIAL_CHUNK), 1)
            <= lax.broadcasted_iota(jnp.int32, (SPATIAL_CHUNK, SPATIAL_CHUNK), 0))
    vb16 = vn.astype(BF16)
    wm = [jnp.where(tril, sp_a_ref[g], 0.0).astype(BF16) for g in range(N_GROUPS_B)]
    rows_out = []
    for ci in range(vn.shape[0] // SPATIAL_CHUNK):
        rs = slice(ci * SPATIAL_CHUNK, (ci + 1) * SPATIAL_CHUNK)
        cols = [jnp.dot(wm[g], vb16[rs, g * GROUP_B:(g + 1) * GROUP_B], preferred_element_type=F32)
                + sp_b_ref[:, g:g + 1] for g in range(N_GROUPS_B)]
        rows_out.append(jnp.concatenate(cols, axis=1))
    return jnp.concatenate(rows_out, axis=0)


def _proj_kernel(*refs, has_vres, chunked):
    (x_ref, g_ref, w_ref, lng_ref, lnb_ref, sp_a_ref, sp_b_ref, wbb_ref) = refs[:8]
    rest = refs[8:]
    if has_vres:
        vd_ref, vu_ref, vb_ref = rest[:3]
        rest = rest[3:]
    z_ref, ga_ref, ma_ref, pb_ref = rest[:4]
    rest = rest[4:]
    if not chunked:
        vn_ref = rest[0]
        rest = rest[1:]
    x = x_ref[...]
    xn = x * lax.rsqrt(jnp.mean(x * x, axis=-1, keepdims=True) + EPS) * g_ref[...]
    xb = xn.astype(BF16)

    def seg(lo, hi):
        return jnp.dot(xb, w_ref[:, lo:hi], preferred_element_type=F32)

    z_ref[...] = seg(0, N_SHIFT)
    t = seg(O_GA, O_U)
    ga_ref[...] = t * _sigmoid(t)
    ma_ref[...] = _sigmoid(seg(O_MA, O_MB))
    u = _gelu_tanh(seg(O_U, O_VB))
    vb = _gelu_tanh(seg(O_VB, O_GB))
    mu = jnp.mean(vb, axis=-1, keepdims=True)
    d = vb - mu
    var = jnp.mean(d * d, axis=-1, keepdims=True)
    vn = d * lax.rsqrt(var + LN_EPS) * lng_ref[...] + lnb_ref[...]
    if not chunked:
        vn_ref[...] = vn
    t = seg(O_GB, O_MA)
    yb = u * _spatial_mix(vn, sp_a_ref, sp_b_ref, chunked) * (t * _sigmoid(t))
    br_b = jnp.dot(yb.astype(BF16), wbb_ref[...], preferred_element_type=F32)
    pb_ref[...] = _sigmoid(seg(O_MB, D_IN)) * br_b
    if has_vres:
        (vg_ref,) = rest
        low = jnp.dot(xb, vd_ref[...], preferred_element_type=F32)
        up = jnp.dot(low.astype(BF16), vu_ref[...], preferred_element_type=F32)
        vg_ref[...] = _sigmoid(vb_ref[...] + up)


def _proj_call(x2d, w_in_all, layer, p, chunked, tm):
    n = x2d.shape[0]
    vres = p["vres"]
    has_vres = vres is not None

    def full(a):
        return pl.BlockSpec(a.shape, lambda i: (0,) * a.ndim, pipeline_mode=pl.Buffered(1))

    def rows(w):
        return pl.BlockSpec((tm, w), lambda i: (i, 0))

    w_spec = pl.BlockSpec((None, D_MODEL, D_IN), lambda i: (layer, 0, 0), pipeline_mode=pl.Buffered(1))
    sp_a, sp_b = (p["sp_w"], p["sp_bT"]) if chunked else (p["sp_w0"], p["sp_b0"])
    consts = [p["ln_g"], p["ln_b"], sp_a, sp_b, p["w_br_b"]] + (list(vres) if has_vres else [])
    widths = [N_SHIFT, D_A, D_MODEL, D_MODEL] + ([] if chunked else [D_B]) + ([D_A] if has_vres else [])
    return pl.pallas_call(
        functools.partial(_proj_kernel, has_vres=has_vres, chunked=chunked),
        grid=(n // tm,),
        in_specs=[rows(D_MODEL), full(p["norm_g"]), w_spec] + [full(a) for a in consts],
        out_specs=[rows(w) for w in widths],
        out_shape=[jax.ShapeDtypeStruct((n, w), F32) for w in widths],
        compiler_params=pltpu.CompilerParams(
            dimension_semantics=("arbitrary",), vmem_limit_bytes=VMEM_LIMIT_BYTES),
        name="proj" + ("_c" if chunked else "_s") + ("_vres" if has_vres else ""),
    )(x2d, p["norm_g"], w_in_all, *consts)


def _timemix_pre(z, z_prev, mu, w0, wup_pad, a0, aup_pad, k_k, k_a, r_k, vres, ones_g, fill=lambda: None):
    zs = z + (z_prev - z) * mu
    fill()
    r = zs[:, 0:D_A]
    k = zs[:, D_A:2 * D_A]
    v = zs[:, 2 * D_A:3 * D_A]
    lora_in = zs[:, 3 * D_A:N_SHIFT]
    wq = w0 + jnp.dot(jnp.tanh(lora_in).astype(BF16), wup_pad, preferred_element_type=F32)
    w = -_softplus(-wq) - 0.5
    lw = -jnp.exp(w)
    fill()
    a = _sigmoid(a0 + jnp.dot(lora_in.astype(BF16), aup_pad, preferred_element_type=F32))
    if vres is not None:
        vgate, vfirst = vres
        v = v + (vfirst - v) * vgate
    kk = k * k_k
    k2 = k * (1.0 + (a - 1.0) * k_a)
    fill()
    ss, rk_sum = _headsums([kk * kk, r * k2 * r_k], ones_g)
    kk = kk / jnp.maximum(jnp.sqrt(ss), 1e-12)
    return r, k2, v, -kk, kk * a, lw, rk_sum * v


def _chunk_groups(chains, masks, fillers):
    t_idx, s_idx, bdmask = masks
    fillers = list(fillers)

    def fill():
        if fillers:
            fillers.pop(0)()

    C = chains[0][0].shape[0]
    n = range(len(chains))
    At, Rt, Bt, Kt, Bh, Kh, V, S, wcc = (list(col) for col in zip(*chains))
    strict = s_idx < t_idx
    incl = s_idx <= t_idx

    def b16(x):
        return x.astype(BF16)

    def bd(x):
        return jnp.where(bdmask, jnp.concatenate([x] * HEADS_PER_GROUP, axis=0), 0)

    def bd_t(x):
        return jnp.where(bdmask, jnp.concatenate([x] * HEADS_PER_GROUP, axis=0).T, 0.0).astype(BF16)

    ar = [b16(jnp.concatenate([At[i], Rt[i]], axis=0)) for i in n]
    sc_b = [_dg(ar[i], bd_t(Bt[i]), NN) for i in n]
    fill()
    sc_k = [_dg(ar[i], bd_t(Kt[i]), NN) for i in n]
    fill()
    N = [jnp.where(strict, sc_b[i][:C], 0.0) for i in n]
    Srb = [jnp.where(incl, sc_b[i][C:], 0.0) for i in n]
    M = [jnp.where(strict, sc_k[i][:C], 0.0) for i in n]
    Srk = [jnp.where(incl, sc_k[i][C:], 0.0) for i in n]
    ar_s = [_dg(ar[i], b16(S[i].T), NN) for i in n]
    fill()
    mv = [_dg(b16(jnp.concatenate([M[i], Srk[i]], axis=0)), bd(b16(V[i])), NN) for i in n]
    X = [ar_s[i][:C] + mv[i][:C] for i in n]

    def level_mask(b):
        return ((t_idx // (2 * b)) == (s_idx // (2 * b))) & ((t_idx % (2 * b)) >= b) & ((s_idx % (2 * b)) < b)

    T = [jnp.where(s_idx == t_idx, 1.0, 0.0) + jnp.where(level_mask(1), N[i], 0.0) for i in n]
    b = 2
    while b < C:
        lm = level_mask(b)
        d = [b16(T[i]) for i in n]
        E = [_dg(b16(jnp.where(lm, N[i], 0.0)), bd(d[i]), NN) for i in n]
        fill()
        F = [_dg(d[i], bd(b16(E[i])), NN) for i in n]
        fill()
        T = [T[i] + F[i] for i in n]
        b *= 2

    U = [_dg(b16(T[i]), bd(b16(X[i])), NN) for i in n]
    fill()
    Y = [ar_s[i][C:] + mv[i][C:] + _dg(b16(Srb[i]), bd(b16(U[i])), NN) for i in n]
    upd = [_dg(b16(jnp.concatenate([U[i], V[i]], axis=0)),
               b16(jnp.concatenate([Bh[i], Kh[i]], axis=0)), TN) for i in n]
    while fillers:
        fill()
    return [(Y[i], S[i] * wcc[i] + jnp.where(bdmask, upd[i], 0.0)) for i in n]


def _front_kernel(*refs, bb, has_vres):
    (x_ref, g_ref, w_ref, lng_ref, lnb_ref, sp_a_ref, sp_b_ref, wbb_ref) = refs[:8]
    rest = refs[8:]
    if has_vres:
        vd_ref, vu_ref, vb_ref, vf_ref = rest[:4]
        rest = rest[4:]
    (s0_ref, sh0_ref, mu_ref, w0_ref, wup_ref, a0_ref, aup_ref, kk_ref, ka_ref, rk_ref, gng_ref, gnb_ref,
     ga_ref, ma_ref, pb_ref, y_ref, v_ref, zl_ref, s_ref, zp_scr, vn_scr) = rest
    C = x_ref.shape[1]
    rows = bb * C
    c = pl.program_id(1)

    @pl.when(c == 0)
    def _():
        s_ref[...] = s0_ref[...]
        zp_scr[...] = sh0_ref[...]
        vn_scr[...] = jnp.zeros(vn_scr.shape, F32)

    x = x_ref[...].reshape(rows, D_MODEL)
    xn = x * lax.rsqrt(jnp.mean(x * x, axis=-1, keepdims=True) + EPS) * g_ref[...]
    xb = xn.astype(BF16)

    def seg(lo, hi):
        return jnp.dot(xb, w_ref[:, lo:hi], preferred_element_type=F32)

    z_all = seg(0, N_SHIFT)
    if has_vres:
        low = jnp.dot(xb, vd_ref[...], preferred_element_type=F32)
        vgate = _sigmoid(vb_ref[...] + jnp.dot(low.astype(BF16), vu_ref[...], preferred_element_type=F32))

    ones_g = _head_ones(HS_W)
    t_idx = lax.broadcasted_iota(jnp.int32, (C, GROUP_W), 0)
    s_idx = lax.broadcasted_iota(jnp.int32, (C, GROUP_W), 1) % C
    rr = lax.broadcasted_iota(jnp.int32, (GROUP_W, GROUP_W), 0) // HEAD
    cc = lax.broadcasted_iota(jnp.int32, (GROUP_W, GROUP_W), 1) // HEAD
    masks = (t_idx, s_idx, rr == cc)
    tri = jnp.where(lax.broadcasted_iota(jnp.int32, (C, C), 1)
                    <= lax.broadcasted_iota(jnp.int32, (C, C), 0), 1.0, 0.0).astype(BF16)
    row0 = lax.broadcasted_iota(jnp.int32, (bb, C, N_SHIFT), 1) == 0

    half = c % 2
    box = {}

    def f_ga():
        t = seg(O_GA, O_U)
        ga_ref[...] = (t * _sigmoid(t)).reshape(bb, C, D_A)

    def f_ma(lo, hi):
        def f():
            ma_ref[:, :, lo:hi] = _sigmoid(seg(O_MA + lo, O_MA + hi)).reshape(bb, C, hi - lo)
        return f

    def f_u():
        box["u"] = _gelu_tanh(seg(O_U, O_VB))

    def f_vn():
        vb = _gelu_tanh(seg(O_VB, O_GB))
        d = vb - jnp.mean(vb, axis=-1, keepdims=True)
        var = jnp.mean(d * d, axis=-1, keepdims=True)
        box["vn"] = d * lax.rsqrt(var + LN_EPS) * lng_ref[...] + lnb_ref[...]

    def f_mix():
        vn = box["vn"]
        srow = lax.broadcasted_iota(jnp.int32, (C, SPATIAL_CHUNK), 0) + half * C
        scol = lax.broadcasted_iota(jnp.int32, (C, SPATIAL_CHUNK), 1)
        keep = scol <= srow
        wm = [jnp.where(keep, sp_a_ref[g, pl.ds(half * C, C), :], 0.0).astype(BF16) for g in range(N_GROUPS_B)]
        bias = sp_b_ref[pl.ds(half * C, C), :]
        outs = []
        for bi in range(bb):
            cur = vn[bi * C:(bi + 1) * C]
            first = jnp.where(half == 0, cur, vn_scr[bi])
            both = jnp.concatenate([first, cur], axis=0).astype(BF16)
            vn_scr[bi] = cur
            cols = [jnp.dot(wm[g], both[:, g * GROUP_B:(g + 1) * GROUP_B], preferred_element_type=F32)
                    + bias[:, g:g + 1] for g in range(N_GROUPS_B)]
            outs.append(jnp.concatenate(cols, axis=1))
        box["mixed"] = jnp.concatenate(outs, axis=0)

    def f_yb():
        t = seg(O_GB, O_MA)
        box["yb"] = (box["u"] * box["mixed"] * (t * _sigmoid(t))).astype(BF16)

    def f_pb(lo, hi):
        def f():
            br_b = jnp.dot(box["yb"], wbb_ref[:, lo:hi], preferred_element_type=F32)
            pb_ref[:, :, lo:hi] = (_sigmoid(seg(O_MB + lo, O_MB + hi)) * br_b).reshape(bb, C, hi - lo)
        return f

    hm = D_MODEL // 2
    fillers = [f_ga, f_ma(0, hm), f_ma(hm, D_MODEL), f_u, f_vn, f_mix, f_yb, f_pb(0, hm), f_pb(hm, D_MODEL)]

    def fill():
        if fillers:
            fillers.pop(0)()

    z3 = z_all.reshape(bb, C, N_SHIFT)
    z_prev = jnp.where(row0, zp_scr[...], pltpu.roll(z_all, 1, 0).reshape(bb, C, N_SHIFT))
    zp_scr[...] = z3[:, C - 1:C, :]
    zl_ref[...] = z3[:, C - 1:C, :]
    vres = (vgate, vf_ref[...].reshape(rows, D_A)) if has_vres else None
    r, k2, v, aa, bbv, lw, bonus = _timemix_pre(
        z_all, z_prev.reshape(rows, N_SHIFT), mu_ref[...], w0_ref[...], wup_ref[...], a0_ref[...], aup_ref[...],
        kk_ref[...], ka_ref[...], rk_ref[...], vres, ones_g, fill)
    v_ref[...] = v.reshape(bb, C, D_A)

    p1 = lw.astype(BF16)
    r1 = lw - p1.astype(F32)
    p2 = r1.astype(BF16)
    p3 = (r1 - p2.astype(F32)).astype(BF16)
    cum = jnp.concatenate(
        [_dg(tri, p1[bi * C:(bi + 1) * C], NN)
         + (_dg(tri, p2[bi * C:(bi + 1) * C], NN) + _dg(tri, p3[bi * C:(bi + 1) * C], NN))
         for bi in range(bb)], axis=0)
    fill()
    cum3 = cum.reshape(bb, C, D_A)
    cum_c = cum3[:, C - 1:C, :]
    wc = jnp.exp(cum)
    wi = jnp.exp(-cum)
    wrel = jnp.exp(cum_c - cum3).reshape(rows, D_A)
    wcc = jnp.exp(cum_c)
    fill()
    At = aa * jnp.exp(cum - lw)
    Rt = r * wc
    Bt = bbv * wi
    Kt = k2 * wi
    Bh = bbv * wrel
    Kh = k2 * wrel
    fill()

    chains = []
    for bi in range(bb):
        rs = slice(bi * C, (bi + 1) * C)
        for g in range(N_HGROUPS):
            sl = slice(g * GROUP_W, (g + 1) * GROUP_W)
            chains.append((At[rs, sl], Rt[rs, sl], Bt[rs, sl], Kt[rs, sl], Bh[rs, sl], Kh[rs, sl],
                           v[rs, sl], s_ref[bi, g], wcc[bi][:, sl]))

    res = _chunk_groups(chains, masks, fillers)

    for bi in range(bb):
        for g in range(N_HGROUPS):
            s_ref[bi, g] = res[bi * N_HGROUPS + g][1]
    y = jnp.concatenate(
        [jnp.concatenate([res[bi * N_HGROUPS + g][0] for g in range(N_HGROUPS)], axis=1) for bi in range(bb)],
        axis=0)
    d = y - _headsums([y], ones_g)[0] * (1.0 / HEAD)
    var = _headsums([d * d], ones_g)[0] * (1.0 / HEAD)
    y_ref[...] = (d * lax.rsqrt(var + GN_EPS) * gng_ref[...] + gnb_ref[...] + bonus).reshape(bb, C, D_A)


def _front_call(x3, w_in_all, layer, p, v_first, s0_bd, shift0, bb):
    B, T, _ = x3.shape
    C = WKV_CHUNK
    vres = p["vres"]
    has_vres = vres is not None

    def tok(w):
        return pl.BlockSpec((bb, C, w), lambda b, c: (b, c, 0))

    def full(a):
        return pl.BlockSpec(a.shape, lambda b, c: (0,) * a.ndim, pipeline_mode=pl.Buffered(1))

    w_spec = pl.BlockSpec((None, D_MODEL, D_IN), lambda b, c: (layer, 0, 0), pipeline_mode=pl.Buffered(1))
    s_spec = pl.BlockSpec((bb, N_HGROUPS, GROUP_W, GROUP_W), lambda b, c: (b, 0, 0, 0))
    row_spec = pl.BlockSpec((bb, 1, N_SHIFT), lambda b, c: (b, 0, 0))
    consts = [p["ln_g"], p["ln_b"], p["sp_w"], p["sp_bT"], p["w_br_b"]]
    in_specs = [tok(D_MODEL), full(p["norm_g"]), w_spec] + [full(a) for a in consts]
    args = [x3, p["norm_g"], w_in_all] + consts
    if has_vres:
        in_specs += [full(a) for a in vres] + [tok(D_A)]
        args += list(vres) + [v_first]
    in_specs += [s_spec, row_spec]
    args += [s0_bd, shift0]
    for name in ("mu", "w0", "wup", "a0", "aup", "k_k", "k_a", "r_k", "gn_g", "gn_b"):
        in_specs.append(full(p[name]))
        args.append(p[name])
    widths = [D_A, D_MODEL, D_MODEL, D_A, D_A]
    return pl.pallas_call(
        functools.partial(_front_kernel, bb=bb, has_vres=has_vres),
        grid=(B // bb, T // C),
        in_specs=in_specs,
        out_specs=[tok(w) for w in widths] + [row_spec, s_spec],
        out_shape=[jax.ShapeDtypeStruct((B, T, w), F32) for w in widths]
        + [jax.ShapeDtypeStruct((B, 1, N_SHIFT), F32), jax.ShapeDtypeStruct(s0_bd.shape, F32)],
        scratch_shapes=[pltpu.VMEM((bb, 1, N_SHIFT), F32), pltpu.VMEM((bb, C, D_B), F32)],
        compiler_params=pltpu.CompilerParams(
            dimension_semantics=("arbitrary", "arbitrary"), vmem_limit_bytes=VMEM_LIMIT_BYTES),
        name="front_vres" if has_vres else "front",
    )(*args)


def _wkv_pre_kernel(*refs, has_vres):
    if has_vres:
        (z_ref, zp_ref, vg_ref, vf_ref, mu_ref, w0_ref, wup_ref, a0_ref, aup_ref, kk_ref, ka_ref, rk_ref,
         r_o, w_o, k_o, vt_o, a_o, b_o, bonus_o, v_o) = refs
        vres = (vg_ref[...], vf_ref[...])
    else:
        (z_ref, zp_ref, mu_ref, w0_ref, wup_ref, a0_ref, aup_ref, kk_ref, ka_ref, rk_ref,
         r_o, w_o, k_o, vt_o, a_o, b_o, bonus_o, v_o) = refs
        vres = None
    r, k2, v, aa, bbv, lw, bonus = _timemix_pre(
        z_ref[...], zp_ref[...], mu_ref[...], w0_ref[...], wup_ref[...], a0_ref[...], aup_ref[...],
        kk_ref[...], ka_ref[...], rk_ref[...], vres, _head_ones(HS_W))
    r_o[...] = r.T
    w_o[...] = jnp.exp(lw).T
    k_o[...] = k2.T
    vt_o[...] = v.T
    a_o[...] = aa.T
    b_o[...] = bbv.T
    bonus_o[...] = bonus.T
    v_o[...] = v


def _wkv_pre_call(z, z_prev, vres, p):
    n = z.shape[0]
    has_vres = vres is not None
    args = [z, z_prev] + (list(vres) if has_vres else [])
    args += [p[k] for k in ("mu", "w0", "wup", "a0", "aup", "k_k", "k_a", "r_k")]
    return pl.pallas_call(
        functools.partial(_wkv_pre_kernel, has_vres=has_vres),
        out_shape=[jax.ShapeDtypeStruct((D_A, n), F32)] * 7 + [jax.ShapeDtypeStruct((n, D_A), F32)],
        compiler_params=pltpu.CompilerParams(vmem_limit_bytes=VMEM_LIMIT_BYTES),
        name="wkv_pre_vres" if has_vres else "wkv_pre",
    )(*args)


def _wkv_step_kernel(s_ref, r_ref, w_ref, k_ref, v_ref, a_ref, b_ref, bonus_ref, gng_ref, gnb_ref,
                     y_ref, so_ref, y_scr):
    aT = a_ref[...]
    wT = w_ref[...]
    bT = b_ref[...]
    kT = k_ref[...]
    rT = r_ref[...]

    def body(i, carry):
        si = s_ref[i]
        sa = jnp.sum(si * aT, axis=0, keepdims=True)
        s2 = si * wT + sa * bT + v_ref[pl.ds(i, 1), :] * kT
        so_ref[i] = s2
        y_scr[pl.ds(i, 1), :] = jnp.sum(s2 * rT, axis=0, keepdims=True)
        return carry

    lax.fori_loop(0, HEAD, body, 0, unroll=4)
    y = y_scr[...]
    d = y - jnp.mean(y, axis=0, keepdims=True)
    var = jnp.mean(d * d, axis=0, keepdims=True)
    y_ref[...] = d * lax.rsqrt(var + GN_EPS) * gng_ref[...] + gnb_ref[...] + bonus_ref[...]


def _wkv_step_call(s_all, layer, vecs, gng_col, gnb_col):
    n = s_all.shape[-1]
    vec_spec = pl.BlockSpec((HEAD, n), lambda h: (h, 0))
    col_spec = pl.BlockSpec((HEAD, 1), lambda h: (h, 0))
    s_in = pl.BlockSpec((None, None, HEAD, HEAD, n), lambda h: (layer, h, 0, 0, 0))
    s_out = pl.BlockSpec((None, HEAD, HEAD, n), lambda h: (h, 0, 0, 0))
    return pl.pallas_call(
        _wkv_step_kernel,
        grid=(N_HEADS,),
        in_specs=[s_in] + [vec_spec] * 7 + [col_spec, col_spec],
        out_specs=[vec_spec, s_out],
        out_shape=[jax.ShapeDtypeStruct((D_A, n), F32), jax.ShapeDtypeStruct(s_all.shape[1:], F32)],
        scratch_shapes=[pltpu.VMEM((HEAD, n), F32)],
        compiler_params=pltpu.CompilerParams(
            dimension_semantics=("arbitrary",), vmem_limit_bytes=VMEM_LIMIT_BYTES),
        name="wkv_step",
    )(s_all, *vecs, gng_col, gnb_col)


def _merge_kernel(*refs, final):
    (h_ref, ya_ref, ga_ref, ma_ref, pb_ref, p_ref, wba_ref, wout_ref, wple_ref, wpg_ref, bpg_ref) = refs[:11]
    rest = refs[11:]
    if final:
        fg_ref, h_out, y_out = rest
    else:
        (h_out,) = rest
    br_a = jnp.dot((ya_ref[...] * ga_ref[...]).astype(BF16), wba_ref[...], preferred_element_type=F32)
    merged = ma_ref[...] * br_a + pb_ref[...]
    h = h_ref[...] + jnp.dot(merged.astype(BF16), wout_ref[...], preferred_element_type=F32)
    ple = jnp.dot(p_ref[...].astype(BF16), wple_ref[...], preferred_element_type=F32)
    gate = _sigmoid(jnp.dot(h.astype(BF16), wpg_ref[...], preferred_element_type=F32) + bpg_ref[...])
    h = h + gate * ple
    h_out[...] = h
    if final:
        y_out[...] = h * lax.rsqrt(jnp.mean(h * h, axis=-1, keepdims=True) + EPS) * fg_ref[...]


def _merge_call(h, ya, ga, ma, pb, pe_all, layer, wl, final_g, tm):
    n = h.shape[0]
    final = final_g is not None

    def rows(w):
        return pl.BlockSpec((tm, w), lambda i: (i, 0))

    def full(a):
        return pl.BlockSpec(a.shape, lambda i: (0,) * a.ndim, pipeline_mode=pl.Buffered(1))

    row_args = [h, ya, ga, ma, pb]
    pe_spec = pl.BlockSpec((tm, D_PLE), lambda i: (layer * (n // tm) + i, 0))
    w_args = [wl["w_br_a"], wl["w_out"], wl["w_ple"], wl["w_pg"], wl["b_pg"]]
    if final:
        w_args.append(final_g)
    n_out = 2 if final else 1
    return pl.pallas_call(
        functools.partial(_merge_kernel, final=final),
        grid=(n // tm,),
        in_specs=[rows(a.shape[1]) for a in row_args] + [pe_spec] + [full(a) for a in w_args],
        out_specs=[rows(D_MODEL)] * n_out,
        out_shape=[jax.ShapeDtypeStruct((n, D_MODEL), F32)] * n_out,
        compiler_params=pltpu.CompilerParams(
            dimension_semantics=("arbitrary",), vmem_limit_bytes=VMEM_LIMIT_BYTES),
        name="merge" + ("_f" if final else ""),
    )(*row_args, pe_all, *w_args)


def _layer_params(W, l):
    row = lambda a: a.reshape(1, -1)
    zpad = jnp.zeros((D_LORA, D_A), F32)
    p = dict(
        norm_g=row(W["norm_g"][l]),
        ln_g=row(W["ln_v_g"][l]), ln_b=row(W["ln_v_b"][l]),
        mu=row(W["shift_mu"][l]), w0=row(W["w0"][l]), a0=row(W["a0"][l]),
        wup=jnp.concatenate([W["w_up"][l], zpad], axis=0).astype(BF16),
        aup=jnp.concatenate([zpad, W["a_up"][l]], axis=0).astype(BF16),
        k_k=row(W["k_k"][l]), k_a=row(W["k_a"][l]), r_k=row(W["r_k"][l]),
        gn_g=row(W["gn_g"][l]), gn_b=row(W["gn_b"][l]),
        w_br_a=W["w_br_a"][l].astype(BF16), w_br_b=W["w_br_b"][l].astype(BF16),
        w_out=W["w_out"][l].astype(BF16), w_ple=W["w_ple"][l].astype(BF16),
        w_pg=W["w_ple_gate"][l].astype(BF16), b_pg=row(W["b_ple_gate"][l]),
        sp_w=W["w_spatial"][l], sp_bT=W["b_spatial"][l].T,
        sp_w0=jnp.repeat(W["w_spatial"][l][:, 0, 0], GROUP_B).reshape(1, D_B),
        sp_b0=jnp.repeat(W["b_spatial"][l][:, 0], GROUP_B).reshape(1, D_B),
    )
    if l > 0:
        p["vres"] = (W["vres_down"][l - 1].astype(BF16), W["vres_up"][l - 1].astype(BF16),
                     row(W["vres_b"][l - 1]))
    else:
        p["vres"] = None
    return p


def _diag_blocks(s_bd):
    B = s_bd.shape[0]
    blocks = [s_bd[:, :, h * HEAD:(h + 1) * HEAD, h * HEAD:(h + 1) * HEAD] for h in range(HEADS_PER_GROUP)]
    return jnp.stack(blocks, axis=2).reshape(B, N_HEADS, HEAD, HEAD)


def _forward_prompt(x, pe, W, params, w_in_all):
    B, T, _ = x.shape
    n = B * T
    h = x.reshape(n, D_MODEL)
    pe_all = pe.reshape(-1, D_PLE)
    v_first = None
    wkvs, shifts = [], []
    depth = len(params)
    s0 = jnp.zeros((B, N_HGROUPS, GROUP_W, GROUP_W), F32)
    shift0 = jnp.zeros((B, 1, N_SHIFT), F32)
    for l, p in enumerate(params):
        ga, ma, pb, ya, v, z_last, s_bd = _front_call(
            h.reshape(B, T, D_MODEL), w_in_all, l, p, v_first, s0, shift0, bb=8)
        if l == 0:
            v_first = v
        wkvs.append(_diag_blocks(s_bd))
        shifts.append(z_last.reshape(B, N_SHIFT))
        fg = W["final_g"].reshape(1, -1) if l == depth - 1 else None
        res = _merge_call(h, ya.reshape(n, D_A), ga.reshape(n, D_A), ma.reshape(n, D_MODEL),
                          pb.reshape(n, D_MODEL), pe_all, l, p, fg, tm=512)
        h = res[0]
    y = res[1].reshape(B, T, D_MODEL)
    return y, jnp.stack(wkvs), jnp.stack(shifts)


def _forward_sample(x, pe, wkv0, shift0, W, params, w_in_all):
    B = x.shape[0]
    h = x.reshape(B, D_MODEL)
    pe_all = pe.reshape(-1, D_PLE)
    s_all = jnp.transpose(wkv0, (0, 2, 3, 4, 1))
    v_first = None
    wkvs, shifts, chunk_vs = [], [], []
    depth = len(params)
    for l, p in enumerate(params):
        outs = _proj_call(h, w_in_all, l, p, chunked=False, tm=B)
        z, ga, ma, pb, vn = outs[:5]
        vres = None if l == 0 else (outs[5], v_first)
        pre = _wkv_pre_call(z, shift0[l], vres, p)
        if l == 0:
            v_first = pre[7]
        ya_t, s_new = _wkv_step_call(s_all, l, pre[:7], p["gn_g"].reshape(D_A, 1), p["gn_b"].reshape(D_A, 1))
        wkvs.append(s_new)
        shifts.append(z)
        chunk_vs.append(vn.reshape(B, 1, D_B))
        fg = W["final_g"].reshape(1, -1) if l == depth - 1 else None
        res = _merge_call(h, ya_t.T, ga, ma, pb, pe_all, l, p, fg, tm=B)
        h = res[0]
    y = res[1].reshape(B, 1, D_MODEL)
    wkv_new = jnp.transpose(jnp.stack(wkvs), (0, 4, 1, 2, 3))
    return y, wkv_new, jnp.stack(shifts), jnp.stack(chunk_vs)


def kernel(x_prompt, x_sample, state_rwkv_wkv, state_rwkv_shift, p_prompt, p_sample, norm_g, w_in, shift_mu, w0, w_up, a0, a_up, vres_down, vres_up, vres_b, k_k, k_a, r_k, gn_g, gn_b, ln_v_g, ln_v_b, w_spatial, b_spatial, w_br_a, w_br_b, w_out, w_ple, w_ple_gate, b_ple_gate, final_g):
    W = dict(norm_g=norm_g, w_in=w_in, shift_mu=shift_mu, w0=w0, w_up=w_up, a0=a0, a_up=a_up,
             vres_down=vres_down, vres_up=vres_up, vres_b=vres_b, k_k=k_k, k_a=k_a, r_k=r_k,
             gn_g=gn_g, gn_b=gn_b, ln_v_g=ln_v_g, ln_v_b=ln_v_b, w_spatial=w_spatial,
             b_spatial=b_spatial, w_br_a=w_br_a, w_br_b=w_br_b, w_out=w_out, w_ple=w_ple,
             w_ple_gate=w_ple_gate, b_ple_gate=b_ple_gate, final_g=final_g)
    params = [_layer_params(W, l) for l in range(w_in.shape[0])]
    w_in_all = w_in.astype(BF16)
    y_p, wkv_p, shift_p = _forward_prompt(x_prompt, p_prompt, W, params, w_in_all)
    y_s, wkv_s, shift_s, chunk_v = _forward_sample(
        x_sample, p_sample, state_rwkv_wkv, state_rwkv_shift, W, params, w_in_all)
    return (y_p, y_s, wkv_p, shift_p, wkv_s, shift_s, chunk_v)
```

```python
import functools

import jax
import jax.numpy as jnp
from jax import lax
from jax.experimental import pallas as pl
from jax.experimental.pallas import tpu as pltpu

D_MODEL = 1024
HEAD = 64
N_HEADS = 8
D_A = N_HEADS * HEAD
D_LORA = 64
D_B = 512
N_GROUPS_B = 4
GROUP_B = D_B // N_GROUPS_B
SPATIAL_CHUNK = 128
D_PLE = 256
N_SHIFT = 3 * D_A + 2 * D_LORA
D_IN = N_SHIFT + D_A + 3 * D_B + 2 * D_MODEL
EPS = 1e-6
GN_EPS = 64e-5
LN_EPS = 1e-5

O_GA = N_SHIFT
O_U = O_GA + D_A
O_VB = O_U + D_B
O_GB = O_VB + D_B
O_MA = O_GB + D_B
O_MB = O_MA + D_MODEL

WKV_CHUNK = 64
HEADS_PER_GROUP = 2
GROUP_W = HEADS_PER_GROUP * HEAD
N_HGROUPS = N_HEADS // HEADS_PER_GROUP
HS_W = 256
VMEM_LIMIT_BYTES = 56 * 1024 * 1024

F32 = jnp.float32
BF16 = jnp.bfloat16
GATE_DTYPE = BF16

NN = ((1,), (0,))
NT = ((1,), (1,))
TN = ((0,), (0,))


def _dg(a, b, dims):
    return lax.dot_general(a, b, (dims, ((), ())), preferred_element_type=F32)


def _split(x):
    hi = x.astype(BF16)
    lo = (x - hi.astype(F32)).astype(BF16)
    return hi, lo


def _sigmoid(x):
    return 1.0 / (1.0 + jnp.exp(-x))


def _gelu_tanh(x):
    return 0.5 * x * (1.0 + jnp.tanh(0.7978845608028654 * (x + 0.044715 * (x * x * x))))


def _softplus(x):
    return jnp.maximum(x, 0.0) + jnp.log(1.0 + jnp.exp(-jnp.abs(x)))


def _head_ones(n):
    r = lax.broadcasted_iota(jnp.int32, (n, n), 0) // HEAD
    c = lax.broadcasted_iota(jnp.int32, (n, n), 1) // HEAD
    return jnp.where(r == c, 1.0, 0.0).astype(BF16)


def _headsums(xs, ones_g):
    rows = xs[0].shape[0]
    pieces = []
    for x in xs:
        hi, lo = _split(x)
        pieces += [hi[:, :HS_W], lo[:, :HS_W], hi[:, HS_W:], lo[:, HS_W:]]
    o = _dg(jnp.concatenate(pieces, axis=0), ones_g, NN)
    outs = []
    for i in range(len(xs)):
        q = [o[(4 * i + j) * rows:(4 * i + j + 1) * rows] for j in range(4)]
        outs.append(jnp.concatenate([q[0] + q[1], q[2] + q[3]], axis=1))
    return outs


def _spatial_mix(vn, sp_a_ref, sp_b_ref, chunked):
    if not chunked:
        return vn * sp_a_ref[...] + sp_b_ref[...]
    tril = (lax.broadcasted_iota(jnp.int32, (SPATIAL_CHUNK, SPATIAL_CHUNK), 1)
            <= lax.broadcasted_iota(jnp.int32, (SPATIAL_CHUNK, SPATIAL_CHUNK), 0))
    vb16 = vn.astype(BF16)
    wm = [jnp.where(tril, sp_a_ref[g], 0.0).astype(BF16) for g in range(N_GROUPS_B)]
    rows_out = []
    for ci in range(vn.shape[0] // SPATIAL_CHUNK):
        rs = slice(ci * SPATIAL_CHUNK, (ci + 1) * SPATIAL_CHUNK)
        cols = [jnp.dot(wm[g], vb16[rs, g * GROUP_B:(g + 1) * GROUP_B], preferred_element_type=F32)
                + sp_b_ref[:, g:g + 1] for g in range(N_GROUPS_B)]
        rows_out.append(jnp.concatenate(cols, axis=1))
    return jnp.concatenate(rows_out, axis=0)


def _proj_kernel(*refs, has_vres, chunked):
    (x_ref, g_ref, w_ref, lng_ref, lnb_ref, sp_a_ref, sp_b_ref, wbb_ref) = refs[:8]
    rest = refs[8:]
    if has_vres:
        vd_ref, vu_ref, vb_ref = rest[:3]
        rest = rest[3:]
    z_ref, ga_ref, ma_ref, pb_ref = rest[:4]
    rest = rest[4:]
    if not chunked:
        vn_ref = rest[0]
        rest = rest[1:]
    x = x_ref[...]
    xn = x * lax.rsqrt(jnp.mean(x * x, axis=-1, keepdims=True) + EPS) * g_ref[...]
    xb = xn.astype(BF16)

    def seg(lo, hi):
        return jnp.dot(xb, w_ref[:, lo:hi], preferred_element_type=F32)

    z_ref[...] = seg(0, N_SHIFT)
    t = seg(O_GA, O_U)
    ga_ref[...] = (t * _sigmoid(t)).astype(ga_ref.dtype)
    ma_ref[...] = _sigmoid(seg(O_MA, O_MB)).astype(ma_ref.dtype)
    u = _gelu_tanh(seg(O_U, O_VB))
    vb = _gelu_tanh(seg(O_VB, O_GB))
    mu = jnp.mean(vb, axis=-1, keepdims=True)
    d = vb - mu
    var = jnp.mean(d * d, axis=-1, keepdims=True)
    vn = d * lax.rsqrt(var + LN_EPS) * lng_ref[...] + lnb_ref[...]
    if not chunked:
        vn_ref[...] = vn
    t = seg(O_GB, O_MA)
    yb = u * _spatial_mix(vn, sp_a_ref, sp_b_ref, chunked) * (t * _sigmoid(t))
    br_b = jnp.dot(yb.astype(BF16), wbb_ref[...], preferred_element_type=F32)
    pb_ref[...] = (_sigmoid(seg(O_MB, D_IN)) * br_b).astype(pb_ref.dtype)
    if has_vres:
        (vg_ref,) = rest
        low = jnp.dot(xb, vd_ref[...], preferred_element_type=F32)
        up = jnp.dot(low.astype(BF16), vu_ref[...], preferred_element_type=F32)
        vg_ref[...] = _sigmoid(vb_ref[...] + up)


def _proj_call(x2d, w_in_all, layer, p, chunked, tm):
    n = x2d.shape[0]
    vres = p["vres"]
    has_vres = vres is not None

    def full(a):
        return pl.BlockSpec(a.shape, lambda i: (0,) * a.ndim, pipeline_mode=pl.Buffered(1))

    def rows(w):
        return pl.BlockSpec((tm, w), lambda i: (i, 0))

    w_spec = pl.BlockSpec((None, D_MODEL, D_IN), lambda i: (layer, 0, 0), pipeline_mode=pl.Buffered(1))
    sp_a, sp_b = (p["sp_w"], p["sp_bT"]) if chunked else (p["sp_w0"], p["sp_b0"])
    consts = [p["ln_g"], p["ln_b"], sp_a, sp_b, p["w_br_b"]] + (list(vres) if has_vres else [])
    widths = [N_SHIFT, D_A, D_MODEL, D_MODEL] + ([] if chunked else [D_B]) + ([D_A] if has_vres else [])
    return pl.pallas_call(
        functools.partial(_proj_kernel, has_vres=has_vres, chunked=chunked),
        grid=(n // tm,),
        in_specs=[rows(D_MODEL), full(p["norm_g"]), w_spec] + [full(a) for a in consts],
        out_specs=[rows(w) for w in widths],
        out_shape=[jax.ShapeDtypeStruct((n, w), GATE_DTYPE if 1 <= j <= 3 else F32) for j, w in enumerate(widths)],
        compiler_params=pltpu.CompilerParams(
            dimension_semantics=("arbitrary",), vmem_limit_bytes=VMEM_LIMIT_BYTES),
        name="proj" + ("_c" if chunked else "_s") + ("_vres" if has_vres else ""),
    )(x2d, p["norm_g"], w_in_all, *consts)


def _timemix_pre(z, z_prev, mu, w0, wup_pad, a0, aup_pad, k_k, k_a, r_k, vres, ones_g, fill=lambda: None):
    zs = z + (z_prev - z) * mu
    fill()
    r = zs[:, 0:D_A]
    k = zs[:, D_A:2 * D_A]
    v = zs[:, 2 * D_A:3 * D_A]
    lora_in = zs[:, 3 * D_A:N_SHIFT]
    wq = w0 + jnp.dot(jnp.tanh(lora_in).astype(BF16), wup_pad, preferred_element_type=F32)
    w = -_softplus(-wq) - 0.5
    lw = -jnp.exp(w)
    fill()
    a = _sigmoid(a0 + jnp.dot(lora_in.astype(BF16), aup_pad, preferred_element_type=F32))
    if vres is not None:
        vgate, vfirst = vres
        v = v + (vfirst - v) * vgate
    kk = k * k_k
    k2 = k * (1.0 + (a - 1.0) * k_a)
    fill()
    ss, rk_sum = _headsums([kk * kk, r * k2 * r_k], ones_g)
    kk = kk / jnp.maximum(jnp.sqrt(ss), 1e-12)
    return r, k2, v, -kk, kk * a, lw, rk_sum * v


def _chunk_groups(chains, masks, fillers):
    t_idx, s_idx, bdmask = masks
    fillers = list(fillers)

    def fill():
        if fillers:
            fillers.pop(0)()

    C = chains[0][0].shape[0]
    n = range(len(chains))
    At, Rt, Bt, Kt, Bh, Kh, V, S, wcc = (list(col) for col in zip(*chains))
    strict = s_idx < t_idx
    incl = s_idx <= t_idx

    def b16(x):
        return x.astype(BF16)

    def bd(x):
        return jnp.where(bdmask, jnp.concatenate([x] * HEADS_PER_GROUP, axis=0), 0)

    def bd_t(x):
        return jnp.where(bdmask, jnp.concatenate([x] * HEADS_PER_GROUP, axis=0).T, 0.0).astype(BF16)

    ar = [b16(jnp.concatenate([At[i], Rt[i]], axis=0)) for i in n]
    sc_b = [_dg(ar[i], bd_t(Bt[i]), NN) for i in n]
    fill()
    sc_k = [_dg(ar[i], bd_t(Kt[i]), NN) for i in n]
    fill()
    N = [jnp.where(strict, sc_b[i][:C], 0.0) for i in n]
    Srb = [jnp.where(incl, sc_b[i][C:], 0.0) for i in n]
    M = [jnp.where(strict, sc_k[i][:C], 0.0) for i in n]
    Srk = [jnp.where(incl, sc_k[i][C:], 0.0) for i in n]
    ar_s = [_dg(ar[i], b16(S[i].T), NN) for i in n]
    fill()
    mv = [_dg(b16(jnp.concatenate([M[i], Srk[i]], axis=0)), bd(b16(V[i])), NN) for i in n]
    X = [ar_s[i][:C] + mv[i][:C] for i in n]

    def level_mask(b):
        return ((t_idx // (2 * b)) == (s_idx // (2 * b))) & ((t_idx % (2 * b)) >= b) & ((s_idx % (2 * b)) < b)

    T = [jnp.where(s_idx == t_idx, 1.0, 0.0) + jnp.where(level_mask(1), N[i], 0.0) for i in n]
    b = 2
    while b < C:
        lm = level_mask(b)
        d = [b16(T[i]) for i in n]
        E = [_dg(b16(jnp.where(lm, N[i], 0.0)), bd(d[i]), NN) for i in n]
        fill()
        F = [_dg(d[i], bd(b16(E[i])), NN) for i in n]
        fill()
        T = [T[i] + F[i] for i in n]
        b *= 2

    U = [_dg(b16(T[i]), bd(b16(X[i])), NN) for i in n]
    fill()
    Y = [ar_s[i][C:] + mv[i][C:] + _dg(b16(Srb[i]), bd(b16(U[i])), NN) for i in n]
    upd = [_dg(b16(jnp.concatenate([U[i], V[i]], axis=0)),
               b16(jnp.concatenate([Bh[i], Kh[i]], axis=0)), TN) for i in n]
    while fillers:
        fill()
    return [(Y[i], S[i] * wcc[i] + jnp.where(bdmask, upd[i], 0.0)) for i in n]


def _front_kernel(*refs, bb, has_vres):
    (x_ref, g_ref, w_ref, lng_ref, lnb_ref, sp_a_ref, sp_b_ref, wbb_ref) = refs[:8]
    rest = refs[8:]
    if has_vres:
        vd_ref, vu_ref, vb_ref, vf_ref = rest[:4]
        rest = rest[4:]
    (s0_ref, sh0_ref, mu_ref, w0_ref, wup_ref, a0_ref, aup_ref, kk_ref, ka_ref, rk_ref, gng_ref, gnb_ref,
     ga_ref, ma_ref, pb_ref, y_ref, v_ref, zl_ref, s_ref, zp_scr, vn_scr) = rest
    C = x_ref.shape[1]
    rows = bb * C
    c = pl.program_id(1)

    @pl.when(c == 0)
    def _():
        s_ref[...] = s0_ref[...]
        zp_scr[...] = sh0_ref[...]
        vn_scr[...] = jnp.zeros(vn_scr.shape, F32)

    x = x_ref[...].reshape(rows, D_MODEL)
    xn = x * lax.rsqrt(jnp.mean(x * x, axis=-1, keepdims=True) + EPS) * g_ref[...]
    xb = xn.astype(BF16)

    def seg(lo, hi):
        return jnp.dot(xb, w_ref[:, lo:hi], preferred_element_type=F32)

    z_all = seg(0, N_SHIFT)
    if has_vres:
        low = jnp.dot(xb, vd_ref[...], preferred_element_type=F32)
        vgate = _sigmoid(vb_ref[...] + jnp.dot(low.astype(BF16), vu_ref[...], preferred_element_type=F32))

    ones_g = _head_ones(HS_W)
    t_idx = lax.broadcasted_iota(jnp.int32, (C, GROUP_W), 0)
    s_idx = lax.broadcasted_iota(jnp.int32, (C, GROUP_W), 1) % C
    rr = lax.broadcasted_iota(jnp.int32, (GROUP_W, GROUP_W), 0) // HEAD
    cc = lax.broadcasted_iota(jnp.int32, (GROUP_W, GROUP_W), 1) // HEAD
    masks = (t_idx, s_idx, rr == cc)
    tri = jnp.where(lax.broadcasted_iota(jnp.int32, (C, C), 1)
                    <= lax.broadcasted_iota(jnp.int32, (C, C), 0), 1.0, 0.0).astype(BF16)
    row0 = lax.broadcasted_iota(jnp.int32, (bb, C, N_SHIFT), 1) == 0

    half = c % 2
    box = {}

    def f_ga():
        t = seg(O_GA, O_U)
        ga_ref[...] = (t * _sigmoid(t)).astype(ga_ref.dtype).reshape(bb, C, D_A)

    def f_ma(lo, hi):
        def f():
            ma_ref[:, :, lo:hi] = _sigmoid(seg(O_MA + lo, O_MA + hi)).astype(ma_ref.dtype).reshape(bb, C, hi - lo)
        return f

    def f_u():
        box["u"] = _gelu_tanh(seg(O_U, O_VB))

    def f_vn():
        vb = _gelu_tanh(seg(O_VB, O_GB))
        d = vb - jnp.mean(vb, axis=-1, keepdims=True)
        var = jnp.mean(d * d, axis=-1, keepdims=True)
        box["vn"] = d * lax.rsqrt(var + LN_EPS) * lng_ref[...] + lnb_ref[...]

    def f_mix():
        vn = box["vn"]
        srow = lax.broadcasted_iota(jnp.int32, (C, SPATIAL_CHUNK), 0) + half * C
        scol = lax.broadcasted_iota(jnp.int32, (C, SPATIAL_CHUNK), 1)
        keep = scol <= srow
        wm = [jnp.where(keep, sp_a_ref[g, pl.ds(half * C, C), :], 0.0).astype(BF16) for g in range(N_GROUPS_B)]
        bias = sp_b_ref[pl.ds(half * C, C), :]
        outs = []
        for bi in range(bb):
            cur = vn[bi * C:(bi + 1) * C]
            first = jnp.where(half == 0, cur, vn_scr[bi])
            both = jnp.concatenate([first, cur], axis=0).astype(BF16)
            vn_scr[bi] = cur
            cols = [jnp.dot(wm[g], both[:, g * GROUP_B:(g + 1) * GROUP_B], preferred_element_type=F32)
                    + bias[:, g:g + 1] for g in range(N_GROUPS_B)]
            outs.append(jnp.concatenate(cols, axis=1))
        box["mixed"] = jnp.concatenate(outs, axis=0)

    def f_yb():
        t = seg(O_GB, O_MA)
        box["yb"] = (box["u"] * box["mixed"] * (t * _sigmoid(t))).astype(BF16)

    def f_pb(lo, hi):
        def f():
            br_b = jnp.dot(box["yb"], wbb_ref[:, lo:hi], preferred_element_type=F32)
            pb_ref[:, :, lo:hi] = (_sigmoid(seg(O_MB + lo, O_MB + hi)) * br_b).astype(pb_ref.dtype).reshape(bb, C, hi - lo)
        return f

    hm = D_MODEL // 2
    fillers = [f_ga, f_ma(0, hm), f_ma(hm, D_MODEL), f_u, f_vn, f_mix, f_yb, f_pb(0, hm), f_pb(hm, D_MODEL)]

    def fill():
        if fillers:
            fillers.pop(0)()

    z3 = z_all.reshape(bb, C, N_SHIFT)
    z_prev = jnp.where(row0, zp_scr[...], pltpu.roll(z_all, 1, 0).reshape(bb, C, N_SHIFT))
    zp_scr[...] = z3[:, C - 1:C, :]
    zl_ref[...] = z3[:, C - 1:C, :]
    vres = (vgate, vf_ref[...].reshape(rows, D_A)) if has_vres else None
    r, k2, v, aa, bbv, lw, bonus = _timemix_pre(
        z_all, z_prev.reshape(rows, N_SHIFT), mu_ref[...], w0_ref[...], wup_ref[...], a0_ref[...], aup_ref[...],
        kk_ref[...], ka_ref[...], rk_ref[...], vres, ones_g, fill)
    v_ref[...] = v.reshape(bb, C, D_A)

    p1 = lw.astype(BF16)
    r1 = lw - p1.astype(F32)
    p2 = r1.astype(BF16)
    p3 = (r1 - p2.astype(F32)).astype(BF16)
    cum = jnp.concatenate(
        [_dg(tri, p1[bi * C:(bi + 1) * C], NN)
         + (_dg(tri, p2[bi * C:(bi + 1) * C], NN) + _dg(tri, p3[bi * C:(bi + 1) * C], NN))
         for bi in range(bb)], axis=0)
    fill()
    cum3 = cum.reshape(bb, C, D_A)
    cum_c = cum3[:, C - 1:C, :]
    wc = jnp.exp(cum)
    wi = jnp.exp(-cum)
    wrel = jnp.exp(cum_c - cum3).reshape(rows, D_A)
    wcc = jnp.exp(cum_c)
    fill()
    At = aa * jnp.exp(cum - lw)
    Rt = r * wc
    Bt = bbv * wi
    Kt = k2 * wi
    Bh = bbv * wrel
    Kh = k2 * wrel
    fill()

    chains = []
    for bi in range(bb):
        rs = slice(bi * C, (bi + 1) * C)
        for g in range(N_HGROUPS):
            sl = slice(g * GROUP_W, (g + 1) * GROUP_W)
            chains.append((At[rs, sl], Rt[rs, sl], Bt[rs, sl], Kt[rs, sl], Bh[rs, sl], Kh[rs, sl],
                           v[rs, sl], s_ref[bi, g], wcc[bi][:, sl]))

    res = _chunk_groups(chains, masks, fillers)

    for bi in range(bb):
        for g in range(N_HGROUPS):
            s_ref[bi, g] = res[bi * N_HGROUPS + g][1]
    y = jnp.concatenate(
        [jnp.concatenate([res[bi * N_HGROUPS + g][0] for g in range(N_HGROUPS)], axis=1) for bi in range(bb)],
        axis=0)
    d = y - _headsums([y], ones_g)[0] * (1.0 / HEAD)
    var = _headsums([d * d], ones_g)[0] * (1.0 / HEAD)
    y_ref[...] = (d * lax.rsqrt(var + GN_EPS) * gng_ref[...] + gnb_ref[...] + bonus).reshape(bb, C, D_A)


def _front_call(x3, w_in_all, layer, p, v_first, s0_bd, shift0, bb):
    B, T, _ = x3.shape
    C = WKV_CHUNK
    vres = p["vres"]
    has_vres = vres is not None

    def tok(w):
        return pl.BlockSpec((bb, C, w), lambda b, c: (b, c, 0))

    def full(a):
        return pl.BlockSpec(a.shape, lambda b, c: (0,) * a.ndim, pipeline_mode=pl.Buffered(1))

    w_spec = pl.BlockSpec((None, D_MODEL, D_IN), lambda b, c: (layer, 0, 0), pipeline_mode=pl.Buffered(1))
    s_spec = pl.BlockSpec((bb, N_HGROUPS, GROUP_W, GROUP_W), lambda b, c: (b, 0, 0, 0))
    row_spec = pl.BlockSpec((bb, 1, N_SHIFT), lambda b, c: (b, 0, 0))
    consts = [p["ln_g"], p["ln_b"], p["sp_w"], p["sp_bT"], p["w_br_b"]]
    in_specs = [tok(D_MODEL), full(p["norm_g"]), w_spec] + [full(a) for a in consts]
    args = [x3, p["norm_g"], w_in_all] + consts
    if has_vres:
        in_specs += [full(a) for a in vres] + [tok(D_A)]
        args += list(vres) + [v_first]
    in_specs += [s_spec, row_spec]
    args += [s0_bd, shift0]
    for name in ("mu", "w0", "wup", "a0", "aup", "k_k", "k_a", "r_k", "gn_g", "gn_b"):
        in_specs.append(full(p[name]))
        args.append(p[name])
    widths = [D_A, D_MODEL, D_MODEL, D_A, D_A]
    return pl.pallas_call(
        functools.partial(_front_kernel, bb=bb, has_vres=has_vres),
        grid=(B // bb, T // C),
        in_specs=in_specs,
        out_specs=[tok(w) for w in widths] + [row_spec, s_spec],
        out_shape=[jax.ShapeDtypeStruct((B, T, w), GATE_DTYPE if j < 3 else F32) for j, w in enumerate(widths)]
        + [jax.ShapeDtypeStruct((B, 1, N_SHIFT), F32), jax.ShapeDtypeStruct(s0_bd.shape, F32)],
        scratch_shapes=[pltpu.VMEM((bb, 1, N_SHIFT), F32), pltpu.VMEM((bb, C, D_B), F32)],
        compiler_params=pltpu.CompilerParams(
            dimension_semantics=("arbitrary", "arbitrary"), vmem_limit_bytes=VMEM_LIMIT_BYTES),
        name="front_vres" if has_vres else "front",
    )(*args)


def _wkv_pre_kernel(*refs, has_vres):
    if has_vres:
        (z_ref, zp_ref, vg_ref, vf_ref, mu_ref, w0_ref, wup_ref, a0_ref, aup_ref, kk_ref, ka_ref, rk_ref,
         r_o, w_o, k_o, vt_o, a_o, b_o, bonus_o, v_o) = refs
        vres = (vg_ref[...], vf_ref[...])
    else:
        (z_ref, zp_ref, mu_ref, w0_ref, wup_ref, a0_ref, aup_ref, kk_ref, ka_ref, rk_ref,
         r_o, w_o, k_o, vt_o, a_o, b_o, bonus_o, v_o) = refs
        vres = None
    r, k2, v, aa, bbv, lw, bonus = _timemix_pre(
        z_ref[...], zp_ref[...], mu_ref[...], w0_ref[...], wup_ref[...], a0_ref[...], aup_ref[...],
        kk_ref[...], ka_ref[...], rk_ref[...], vres, _head_ones(HS_W))
    r_o[...] = r.T
    w_o[...] = jnp.exp(lw).T
    k_o[...] = k2.T
    vt_o[...] = v.T
    a_o[...] = aa.T
    b_o[...] = bbv.T
    bonus_o[...] = bonus.T
    v_o[...] = v


def _wkv_pre_call(z, z_prev, vres, p):
    n = z.shape[0]
    has_vres = vres is not None
    args = [z, z_prev] + (list(vres) if has_vres else [])
    args += [p[k] for k in ("mu", "w0", "wup", "a0", "aup", "k_k", "k_a", "r_k")]
    return pl.pallas_call(
        functools.partial(_wkv_pre_kernel, has_vres=has_vres),
        out_shape=[jax.ShapeDtypeStruct((D_A, n), F32)] * 7 + [jax.ShapeDtypeStruct((n, D_A), F32)],
        compiler_params=pltpu.CompilerParams(vmem_limit_bytes=VMEM_LIMIT_BYTES),
        name="wkv_pre_vres" if has_vres else "wkv_pre",
    )(*args)


def _wkv_step_kernel(s_ref, r_ref, w_ref, k_ref, v_ref, a_ref, b_ref, bonus_ref, gng_ref, gnb_ref,
                     y_ref, so_ref, y_scr):
    aT = a_ref[...]
    wT = w_ref[...]
    bT = b_ref[...]
    kT = k_ref[...]
    rT = r_ref[...]

    def body(i, carry):
        si = s_ref[i]
        sa = jnp.sum(si * aT, axis=0, keepdims=True)
        s2 = si * wT + sa * bT + v_ref[pl.ds(i, 1), :] * kT
        so_ref[i] = s2
        y_scr[pl.ds(i, 1), :] = jnp.sum(s2 * rT, axis=0, keepdims=True)
        return carry

    lax.fori_loop(0, HEAD, body, 0, unroll=4)
    y = y_scr[...]
    d = y - jnp.mean(y, axis=0, keepdims=True)
    var = jnp.mean(d * d, axis=0, keepdims=True)
    y_ref[...] = d * lax.rsqrt(var + GN_EPS) * gng_ref[...] + gnb_ref[...] + bonus_ref[...]


def _wkv_step_call(s_all, layer, vecs, gng_col, gnb_col):
    n = s_all.shape[-1]
    vec_spec = pl.BlockSpec((HEAD, n), lambda h: (h, 0))
    col_spec = pl.BlockSpec((HEAD, 1), lambda h: (h, 0))
    s_in = pl.BlockSpec((None, None, HEAD, HEAD, n), lambda h: (layer, h, 0, 0, 0))
    s_out = pl.BlockSpec((None, HEAD, HEAD, n), lambda h: (h, 0, 0, 0))
    return pl.pallas_call(
        _wkv_step_kernel,
        grid=(N_HEADS,),
        in_specs=[s_in] + [vec_spec] * 7 + [col_spec, col_spec],
        out_specs=[vec_spec, s_out],
        out_shape=[jax.ShapeDtypeStruct((D_A, n), F32), jax.ShapeDtypeStruct(s_all.shape[1:], F32)],
        scratch_shapes=[pltpu.VMEM((HEAD, n), F32)],
        compiler_params=pltpu.CompilerParams(
            dimension_semantics=("arbitrary",), vmem_limit_bytes=VMEM_LIMIT_BYTES),
        name="wkv_step",
    )(s_all, *vecs, gng_col, gnb_col)


def _merge_kernel(*refs, final):
    (h_ref, ya_ref, ga_ref, ma_ref, pb_ref, p_ref, wba_ref, wout_ref, wple_ref, wpg_ref, bpg_ref) = refs[:11]
    rest = refs[11:]
    if final:
        fg_ref, h_out, y_out = rest
    else:
        (h_out,) = rest
    br_a = jnp.dot((ya_ref[...] * ga_ref[...].astype(F32)).astype(BF16), wba_ref[...],
                   preferred_element_type=F32)
    merged = ma_ref[...].astype(F32) * br_a + pb_ref[...].astype(F32)
    h = h_ref[...] + jnp.dot(merged.astype(BF16), wout_ref[...], preferred_element_type=F32)
    ple = jnp.dot(p_ref[...].astype(BF16), wple_ref[...], preferred_element_type=F32)
    gate = _sigmoid(jnp.dot(h.astype(BF16), wpg_ref[...], preferred_element_type=F32) + bpg_ref[...])
    h = h + gate * ple
    h_out[...] = h
    if final:
        y_out[...] = h * lax.rsqrt(jnp.mean(h * h, axis=-1, keepdims=True) + EPS) * fg_ref[...]


def _merge_call(h, ya, ga, ma, pb, pe_all, layer, wl, final_g, tm):
    n = h.shape[0]
    final = final_g is not None

    def rows(w):
        return pl.BlockSpec((tm, w), lambda i: (i, 0))

    def full(a):
        return pl.BlockSpec(a.shape, lambda i: (0,) * a.ndim, pipeline_mode=pl.Buffered(1))

    row_args = [h, ya, ga, ma, pb]
    pe_spec = pl.BlockSpec((tm, D_PLE), lambda i: (layer * (n // tm) + i, 0))
    w_args = [wl["w_br_a"], wl["w_out"], wl["w_ple"], wl["w_pg"], wl["b_pg"]]
    if final:
        w_args.append(final_g)
    n_out = 2 if final else 1
    return pl.pallas_call(
        functools.partial(_merge_kernel, final=final),
        grid=(n // tm,),
        in_specs=[rows(a.shape[1]) for a in row_args] + [pe_spec] + [full(a) for a in w_args],
        out_specs=[rows(D_MODEL)] * n_out,
        out_shape=[jax.ShapeDtypeStruct((n, D_MODEL), F32)] * n_out,
        compiler_params=pltpu.CompilerParams(
            dimension_semantics=("arbitrary",), vmem_limit_bytes=VMEM_LIMIT_BYTES),
        name="merge" + ("_f" if final else ""),
    )(*row_args, pe_all, *w_args)


def _layer_params(W, l):
    row = lambda a: a.reshape(1, -1)
    zpad = jnp.zeros((D_LORA, D_A), F32)
    p = dict(
        norm_g=row(W["norm_g"][l]),
        ln_g=row(W["ln_v_g"][l]), ln_b=row(W["ln_v_b"][l]),
        mu=row(W["shift_mu"][l]), w0=row(W["w0"][l]), a0=row(W["a0"][l]),
        wup=jnp.concatenate([W["w_up"][l], zpad], axis=0).astype(BF16),
        aup=jnp.concatenate([zpad, W["a_up"][l]], axis=0).astype(BF16),
        k_k=row(W["k_k"][l]), k_a=row(W["k_a"][l]), r_k=row(W["r_k"][l]),
        gn_g=row(W["gn_g"][l]), gn_b=row(W["gn_b"][l]),
        w_br_a=W["w_br_a"][l].astype(BF16), w_br_b=W["w_br_b"][l].astype(BF16),
        w_out=W["w_out"][l].astype(BF16), w_ple=W["w_ple"][l].astype(BF16),
        w_pg=W["w_ple_gate"][l].astype(BF16), b_pg=row(W["b_ple_gate"][l]),
        sp_w=W["w_spatial"][l], sp_bT=W["b_spatial"][l].T,
        sp_w0=jnp.repeat(W["w_spatial"][l][:, 0, 0], GROUP_B).reshape(1, D_B),
        sp_b0=jnp.repeat(W["b_spatial"][l][:, 0], GROUP_B).reshape(1, D_B),
    )
    if l > 0:
        p["vres"] = (W["vres_down"][l - 1].astype(BF16), W["vres_up"][l - 1].astype(BF16),
                     row(W["vres_b"][l - 1]))
    else:
        p["vres"] = None
    return p


def _diag_blocks(s_bd):
    B = s_bd.shape[0]
    blocks = [s_bd[:, :, h * HEAD:(h + 1) * HEAD, h * HEAD:(h + 1) * HEAD] for h in range(HEADS_PER_GROUP)]
    return jnp.stack(blocks, axis=2).reshape(B, N_HEADS, HEAD, HEAD)


def _forward_prompt(x, pe, W, params, w_in_all):
    B, T, _ = x.shape
    n = B * T
    h = x.reshape(n, D_MODEL)
    pe_all = pe.reshape(-1, D_PLE)
    v_first = None
    wkvs, shifts = [], []
    depth = len(params)
    s0 = jnp.zeros((B, N_HGROUPS, GROUP_W, GROUP_W), F32)
    shift0 = jnp.zeros((B, 1, N_SHIFT), F32)
    for l, p in enumerate(params):
        ga, ma, pb, ya, v, z_last, s_bd = _front_call(
            h.reshape(B, T, D_MODEL), w_in_all, l, p, v_first, s0, shift0, bb=8)
        if l == 0:
            v_first = v
        wkvs.append(_diag_blocks(s_bd))
        shifts.append(z_last.reshape(B, N_SHIFT))
        fg = W["final_g"].reshape(1, -1) if l == depth - 1 else None
        res = _merge_call(h, ya.reshape(n, D_A), ga.reshape(n, D_A), ma.reshape(n, D_MODEL),
                          pb.reshape(n, D_MODEL), pe_all, l, p, fg, tm=512)
        h = res[0]
    y = res[1].reshape(B, T, D_MODEL)
    return y, jnp.stack(wkvs), jnp.stack(shifts)


def _forward_sample(x, pe, wkv0, shift0, W, params, w_in_all):
    B = x.shape[0]
    h = x.reshape(B, D_MODEL)
    pe_all = pe.reshape(-1, D_PLE)
    s_all = jnp.transpose(wkv0, (0, 2, 3, 4, 1))
    v_first = None
    wkvs, shifts, chunk_vs = [], [], []
    depth = len(params)
    for l, p in enumerate(params):
        outs = _proj_call(h, w_in_all, l, p, chunked=False, tm=B)
        z, ga, ma, pb, vn = outs[:5]
        vres = None if l == 0 else (outs[5], v_first)
        pre = _wkv_pre_call(z, shift0[l], vres, p)
        if l == 0:
            v_first = pre[7]
        ya_t, s_new = _wkv_step_call(s_all, l, pre[:7], p["gn_g"].reshape(D_A, 1), p["gn_b"].reshape(D_A, 1))
        wkvs.append(s_new)
        shifts.append(z)
        chunk_vs.append(vn.reshape(B, 1, D_B))
        fg = W["final_g"].reshape(1, -1) if l == depth - 1 else None
        res = _merge_call(h, ya_t.T, ga, ma, pb, pe_all, l, p, fg, tm=B)
        h = res[0]
    y = res[1].reshape(B, 1, D_MODEL)
    wkv_new = jnp.transpose(jnp.stack(wkvs), (0, 4, 1, 2, 3))
    return y, wkv_new, jnp.stack(shifts), jnp.stack(chunk_vs)


def kernel(x_prompt, x_sample, state_rwkv_wkv, state_rwkv_shift, p_prompt, p_sample, norm_g, w_in, shift_mu, w0, w_up, a0, a_up, vres_down, vres_up, vres_b, k_k, k_a, r_k, gn_g, gn_b, ln_v_g, ln_v_b, w_spatial, b_spatial, w_br_a, w_br_b, w_out, w_ple, w_ple_gate, b_ple_gate, final_g):
    W = dict(norm_g=norm_g, w_in=w_in, shift_mu=shift_mu, w0=w0, w_up=w_up, a0=a0, a_up=a_up,
             vres_down=vres_down, vres_up=vres_up, vres_b=vres_b, k_k=k_k, k_a=k_a, r_k=r_k,
             gn_g=gn_g, gn_b=gn_b, ln_v_g=ln_v_g, ln_v_b=ln_v_b, w_spatial=w_spatial,
             b_spatial=b_spatial, w_br_a=w_br_a, w_br_b=w_br_b, w_out=w_out, w_ple=w_ple,
             w_ple_gate=w_ple_gate, b_ple_gate=b_ple_gate, final_g=final_g)
    params = [_layer_params(W, l) for l in range(w_in.shape[0])]
    w_in_all = w_in.astype(BF16)
    y_p, wkv_p, shift_p = _forward_prompt(x_prompt, p_prompt, W, params, w_in_all)
    y_s, wkv_s, shift_s, chunk_v = _forward_sample(
        x_sample, p_sample, state_rwkv_wkv, state_rwkv_shift, W, params, w_in_all)
    return (y_p, y_s, wkv_p, shift_p, wkv_s, shift_s, chunk_v)
```

```python
import functools

import jax
import jax.numpy as jnp
from jax import lax
from jax.experimental import pallas as pl
from jax.experimental.pallas import tpu as pltpu

D_MODEL = 1024
HEAD = 64
N_HEADS = 8
D_A = N_HEADS * HEAD
D_LORA = 64
D_B = 512
N_GROUPS_B = 4
GROUP_B = D_B // N_GROUPS_B
SPATIAL_CHUNK = 128
D_PLE = 256
N_SHIFT = 3 * D_A + 2 * D_LORA
D_IN = N_SHIFT + D_A + 3 * D_B + 2 * D_MODEL
EPS = 1e-6
GN_EPS = 64e-5
LN_EPS = 1e-5

O_GA = N_SHIFT
O_U = O_GA + D_A
O_VB = O_U + D_B
O_GB = O_VB + D_B
O_MA = O_GB + D_B
O_MB = O_MA + D_MODEL

WKV_CHUNK = 64
HEADS_PER_GROUP = 2
GROUP_W = HEADS_PER_GROUP * HEAD
N_HGROUPS = N_HEADS // HEADS_PER_GROUP
HS_W = 256
VMEM_LIMIT_BYTES = 56 * 1024 * 1024

F32 = jnp.float32
BF16 = jnp.bfloat16
GATE_DTYPE = BF16

NN = ((1,), (0,))
NT = ((1,), (1,))
TN = ((0,), (0,))


def _dg(a, b, dims):
    return lax.dot_general(a, b, (dims, ((), ())), preferred_element_type=F32)


def _split(x):
    hi = x.astype(BF16)
    lo = (x - hi.astype(F32)).astype(BF16)
    return hi, lo


def _sigmoid(x):
    return 1.0 / (1.0 + jnp.exp(-x))


def _gelu_tanh(x):
    return 0.5 * x * (1.0 + jnp.tanh(0.7978845608028654 * (x + 0.044715 * (x * x * x))))


def _softplus(x):
    return jnp.maximum(x, 0.0) + jnp.log(1.0 + jnp.exp(-jnp.abs(x)))


def _head_ones(n):
    r = lax.broadcasted_iota(jnp.int32, (n, n), 0) // HEAD
    c = lax.broadcasted_iota(jnp.int32, (n, n), 1) // HEAD
    return jnp.where(r == c, 1.0, 0.0).astype(BF16)


def _headsums(xs, ones_g):
    rows = xs[0].shape[0]
    pieces = []
    for x in xs:
        hi, lo = _split(x)
        pieces += [hi[:, :HS_W], lo[:, :HS_W], hi[:, HS_W:], lo[:, HS_W:]]
    o = _dg(jnp.concatenate(pieces, axis=0), ones_g, NN)
    outs = []
    for i in range(len(xs)):
        q = [o[(4 * i + j) * rows:(4 * i + j + 1) * rows] for j in range(4)]
        outs.append(jnp.concatenate([q[0] + q[1], q[2] + q[3]], axis=1))
    return outs


def _spatial_mix(vn, sp_a_ref, sp_b_ref, chunked):
    if not chunked:
        return vn * sp_a_ref[...] + sp_b_ref[...]
    tril = (lax.broadcasted_iota(jnp.int32, (SPATIAL_CHUNK, SPATIAL_CHUNK), 1)
            <= lax.broadcasted_iota(jnp.int32, (SPATIAL_CHUNK, SPATIAL_CHUNK), 0))
    vb16 = vn.astype(BF16)
    wm = [jnp.where(tril, sp_a_ref[g], 0.0).astype(BF16) for g in range(N_GROUPS_B)]
    rows_out = []
    for ci in range(vn.shape[0] // SPATIAL_CHUNK):
        rs = slice(ci * SPATIAL_CHUNK, (ci + 1) * SPATIAL_CHUNK)
        cols = [jnp.dot(wm[g], vb16[rs, g * GROUP_B:(g + 1) * GROUP_B], preferred_element_type=F32)
                + sp_b_ref[:, g:g + 1] for g in range(N_GROUPS_B)]
        rows_out.append(jnp.concatenate(cols, axis=1))
    return jnp.concatenate(rows_out, axis=0)


def _proj_kernel(*refs, has_vres, chunked):
    (x_ref, g_ref, w_ref, lng_ref, lnb_ref, sp_a_ref, sp_b_ref, wbb_ref) = refs[:8]
    rest = refs[8:]
    if has_vres:
        vd_ref, vu_ref, vb_ref = rest[:3]
        rest = rest[3:]
    z_ref, ga_ref, ma_ref, pb_ref = rest[:4]
    rest = rest[4:]
    if not chunked:
        vn_ref = rest[0]
        rest = rest[1:]
    x = x_ref[...]
    xn = x * lax.rsqrt(jnp.mean(x * x, axis=-1, keepdims=True) + EPS) * g_ref[...]
    xb = xn.astype(BF16)

    def seg(lo, hi):
        return jnp.dot(xb, w_ref[:, lo:hi], preferred_element_type=F32)

    z_ref[...] = seg(0, N_SHIFT)
    t = seg(O_GA, O_U)
    ga_ref[...] = (t * _sigmoid(t)).astype(ga_ref.dtype)
    ma_ref[...] = _sigmoid(seg(O_MA, O_MB)).astype(ma_ref.dtype)
    u = _gelu_tanh(seg(O_U, O_VB))
    vb = _gelu_tanh(seg(O_VB, O_GB))
    mu = jnp.mean(vb, axis=-1, keepdims=True)
    d = vb - mu
    var = jnp.mean(d * d, axis=-1, keepdims=True)
    vn = d * lax.rsqrt(var + LN_EPS) * lng_ref[...] + lnb_ref[...]
    if not chunked:
        vn_ref[...] = vn
    t = seg(O_GB, O_MA)
    yb = u * _spatial_mix(vn, sp_a_ref, sp_b_ref, chunked) * (t * _sigmoid(t))
    br_b = jnp.dot(yb.astype(BF16), wbb_ref[...], preferred_element_type=F32)
    pb_ref[...] = (_sigmoid(seg(O_MB, D_IN)) * br_b).astype(pb_ref.dtype)
    if has_vres:
        (vg_ref,) = rest
        low = jnp.dot(xb, vd_ref[...], preferred_element_type=F32)
        up = jnp.dot(low.astype(BF16), vu_ref[...], preferred_element_type=F32)
        vg_ref[...] = _sigmoid(vb_ref[...] + up)


def _proj_call(x2d, w_in_all, layer, p, chunked, tm):
    n = x2d.shape[0]
    vres = p["vres"]
    has_vres = vres is not None

    def full(a):
        return pl.BlockSpec(a.shape, lambda i: (0,) * a.ndim, pipeline_mode=pl.Buffered(1))

    def rows(w):
        return pl.BlockSpec((tm, w), lambda i: (i, 0))

    w_spec = pl.BlockSpec((None, D_MODEL, D_IN), lambda i: (layer, 0, 0), pipeline_mode=pl.Buffered(1))
    sp_a, sp_b = (p["sp_w"], p["sp_bT"]) if chunked else (p["sp_w0"], p["sp_b0"])
    consts = [p["ln_g"], p["ln_b"], sp_a, sp_b, p["w_br_b"]] + (list(vres) if has_vres else [])
    widths = [N_SHIFT, D_A, D_MODEL, D_MODEL] + ([] if chunked else [D_B]) + ([D_A] if has_vres else [])
    return pl.pallas_call(
        functools.partial(_proj_kernel, has_vres=has_vres, chunked=chunked),
        grid=(n // tm,),
        in_specs=[rows(D_MODEL), full(p["norm_g"]), w_spec] + [full(a) for a in consts],
        out_specs=[rows(w) for w in widths],
        out_shape=[jax.ShapeDtypeStruct((n, w), GATE_DTYPE if 1 <= j <= 3 else F32) for j, w in enumerate(widths)],
        compiler_params=pltpu.CompilerParams(
            dimension_semantics=("arbitrary",), vmem_limit_bytes=VMEM_LIMIT_BYTES),
        name="proj" + ("_c" if chunked else "_s") + ("_vres" if has_vres else ""),
    )(x2d, p["norm_g"], w_in_all, *consts)


def _timemix_pre(z, z_prev, mu, w0, wup_pad, a0, aup_pad, k_k, k_a, r_k, vres, ones_g, fill=lambda: None):
    zs = z + (z_prev - z) * mu
    fill()
    r = zs[:, 0:D_A]
    k = zs[:, D_A:2 * D_A]
    v = zs[:, 2 * D_A:3 * D_A]
    lora_in = zs[:, 3 * D_A:N_SHIFT]
    wq = w0 + jnp.dot(jnp.tanh(lora_in).astype(BF16), wup_pad, preferred_element_type=F32)
    w = -_softplus(-wq) - 0.5
    lw = -jnp.exp(w)
    fill()
    a = _sigmoid(a0 + jnp.dot(lora_in.astype(BF16), aup_pad, preferred_element_type=F32))
    if vres is not None:
        vgate, vfirst = vres
        v = v + (vfirst - v) * vgate
    kk = k * k_k
    k2 = k * (1.0 + (a - 1.0) * k_a)
    fill()
    ss, rk_sum = _headsums([kk * kk, r * k2 * r_k], ones_g)
    kk = kk / jnp.maximum(jnp.sqrt(ss), 1e-12)
    return r, k2, v, -kk, kk * a, lw, rk_sum * v


def _chunk_groups(chains, masks, fillers):
    t_idx, s_idx, bdmask = masks
    fillers = list(fillers)

    def fill():
        if fillers:
            fillers.pop(0)()

    C = chains[0][0].shape[0]
    n = range(len(chains))
    At, Rt, Bt, Kt, Bh, Kh, V, S, wcc = (list(col) for col in zip(*chains))
    strict = s_idx < t_idx
    incl = s_idx <= t_idx

    def b16(x):
        return x.astype(BF16)

    def bd(x):
        return jnp.where(bdmask, jnp.concatenate([x] * HEADS_PER_GROUP, axis=0), 0)

    def bd_t(x):
        return jnp.where(bdmask, jnp.concatenate([x] * HEADS_PER_GROUP, axis=0).T, 0.0).astype(BF16)

    ar = [b16(jnp.concatenate([At[i], Rt[i]], axis=0)) for i in n]
    sc_b = [_dg(ar[i], bd_t(Bt[i]), NN) for i in n]
    fill()
    sc_k = [_dg(ar[i], bd_t(Kt[i]), NN) for i in n]
    fill()
    N = [jnp.where(strict, sc_b[i][:C], 0.0) for i in n]
    Srb = [jnp.where(incl, sc_b[i][C:], 0.0) for i in n]
    M = [jnp.where(strict, sc_k[i][:C], 0.0) for i in n]
    Srk = [jnp.where(incl, sc_k[i][C:], 0.0) for i in n]
    ar_s = [_dg(ar[i], b16(S[i].T), NN) for i in n]
    fill()
    mv = [_dg(b16(jnp.concatenate([M[i], Srk[i]], axis=0)), bd(b16(V[i])), NN) for i in n]
    X = [ar_s[i][:C] + mv[i][:C] for i in n]

    def level_mask(b):
        return ((t_idx // (2 * b)) == (s_idx // (2 * b))) & ((t_idx % (2 * b)) >= b) & ((s_idx % (2 * b)) < b)

    T = [jnp.where(s_idx == t_idx, 1.0, 0.0) + jnp.where(level_mask(1), N[i], 0.0) for i in n]
    b = 2
    while b < C:
        lm = level_mask(b)
        d = [b16(T[i]) for i in n]
        E = [_dg(b16(jnp.where(lm, N[i], 0.0)), bd(d[i]), NN) for i in n]
        fill()
        F = [_dg(d[i], bd(b16(E[i])), NN) for i in n]
        fill()
        T = [T[i] + F[i] for i in n]
        b *= 2

    U = [_dg(b16(T[i]), bd(b16(X[i])), NN) for i in n]
    fill()
    Y = [ar_s[i][C:] + mv[i][C:] + _dg(b16(Srb[i]), bd(b16(U[i])), NN) for i in n]
    upd = [_dg(b16(jnp.concatenate([U[i], V[i]], axis=0)),
               b16(jnp.concatenate([Bh[i], Kh[i]], axis=0)), TN) for i in n]
    while fillers:
        fill()
    return [(Y[i], S[i] * wcc[i] + jnp.where(bdmask, upd[i], 0.0)) for i in n]


def _front_kernel(*refs, bb, has_vres):
    (x_ref, g_ref, w_ref, lng_ref, lnb_ref, sp_a_ref, sp_b_ref, wbb_ref) = refs[:8]
    rest = refs[8:]
    if has_vres:
        vd_ref, vu_ref, vb_ref, vf_ref = rest[:4]
        rest = rest[4:]
    (s0_ref, sh0_ref, mu_ref, w0_ref, wup_ref, a0_ref, aup_ref, kk_ref, ka_ref, rk_ref, gng_ref, gnb_ref,
     ga_ref, ma_ref, pb_ref, y_ref, v_ref, zl_ref, s_ref, zp_scr, vn_scr) = rest
    C = x_ref.shape[1]
    rows = bb * C
    c = pl.program_id(1)

    @pl.when(c == 0)
    def _():
        s_ref[...] = s0_ref[...]
        zp_scr[...] = sh0_ref[...]
        vn_scr[...] = jnp.zeros(vn_scr.shape, F32)

    x = x_ref[...].reshape(rows, D_MODEL)
    xn = x * lax.rsqrt(jnp.mean(x * x, axis=-1, keepdims=True) + EPS) * g_ref[...]
    xb = xn.astype(BF16)

    def seg(lo, hi):
        return jnp.dot(xb, w_ref[:, lo:hi], preferred_element_type=F32)

    z_all = seg(0, N_SHIFT)
    if has_vres:
        low = jnp.dot(xb, vd_ref[...], preferred_element_type=F32)
        vgate = _sigmoid(vb_ref[...] + jnp.dot(low.astype(BF16), vu_ref[...], preferred_element_type=F32))

    ones_g = _head_ones(HS_W)
    t_idx = lax.broadcasted_iota(jnp.int32, (C, GROUP_W), 0)
    s_idx = lax.broadcasted_iota(jnp.int32, (C, GROUP_W), 1) % C
    rr = lax.broadcasted_iota(jnp.int32, (GROUP_W, GROUP_W), 0) // HEAD
    cc = lax.broadcasted_iota(jnp.int32, (GROUP_W, GROUP_W), 1) // HEAD
    masks = (t_idx, s_idx, rr == cc)
    tri = jnp.where(lax.broadcasted_iota(jnp.int32, (C, C), 1)
                    <= lax.broadcasted_iota(jnp.int32, (C, C), 0), 1.0, 0.0).astype(BF16)
    row0 = lax.broadcasted_iota(jnp.int32, (bb, C, N_SHIFT), 1) == 0

    half = c % 2
    box = {}

    def f_ga():
        t = seg(O_GA, O_U)
        ga_ref[...] = (t * _sigmoid(t)).astype(ga_ref.dtype).reshape(bb, C, D_A)

    def f_ma(lo, hi):
        def f():
            ma_ref[:, :, lo:hi] = _sigmoid(seg(O_MA + lo, O_MA + hi)).astype(ma_ref.dtype).reshape(bb, C, hi - lo)
        return f

    def f_u():
        box["u"] = _gelu_tanh(seg(O_U, O_VB))

    def f_vn():
        vb = _gelu_tanh(seg(O_VB, O_GB))
        d = vb - jnp.mean(vb, axis=-1, keepdims=True)
        var = jnp.mean(d * d, axis=-1, keepdims=True)
        box["vn"] = d * lax.rsqrt(var + LN_EPS) * lng_ref[...] + lnb_ref[...]

    def f_mix():
        vn = box["vn"]
        srow = lax.broadcasted_iota(jnp.int32, (C, SPATIAL_CHUNK), 0) + half * C
        scol = lax.broadcasted_iota(jnp.int32, (C, SPATIAL_CHUNK), 1)
        keep = scol <= srow
        wm = [jnp.where(keep, sp_a_ref[g, pl.ds(half * C, C), :], 0.0).astype(BF16) for g in range(N_GROUPS_B)]
        bias = sp_b_ref[pl.ds(half * C, C), :]
        outs = []
        for bi in range(bb):
            cur = vn[bi * C:(bi + 1) * C]
            first = jnp.where(half == 0, cur, vn_scr[bi])
            both = jnp.concatenate([first, cur], axis=0).astype(BF16)
            vn_scr[bi] = cur
            cols = [jnp.dot(wm[g], both[:, g * GROUP_B:(g + 1) * GROUP_B], preferred_element_type=F32)
                    + bias[:, g:g + 1] for g in range(N_GROUPS_B)]
            outs.append(jnp.concatenate(cols, axis=1))
        box["mixed"] = jnp.concatenate(outs, axis=0)

    def f_yb():
        t = seg(O_GB, O_MA)
        box["yb"] = (box["u"] * box["mixed"] * (t * _sigmoid(t))).astype(BF16)

    def f_pb(lo, hi):
        def f():
            br_b = jnp.dot(box["yb"], wbb_ref[:, lo:hi], preferred_element_type=F32)
            pb_ref[:, :, lo:hi] = (_sigmoid(seg(O_MB + lo, O_MB + hi)) * br_b).astype(pb_ref.dtype).reshape(bb, C, hi - lo)
        return f

    hm = D_MODEL // 2
    fillers = [f_ga, f_ma(0, hm), f_ma(hm, D_MODEL), f_u, f_vn, f_mix, f_yb, f_pb(0, hm), f_pb(hm, D_MODEL)]

    def fill():
        if fillers:
            fillers.pop(0)()

    z3 = z_all.reshape(bb, C, N_SHIFT)
    z_prev = jnp.where(row0, zp_scr[...], pltpu.roll(z_all, 1, 0).reshape(bb, C, N_SHIFT))
    zp_scr[...] = z3[:, C - 1:C, :]
    zl_ref[...] = z3[:, C - 1:C, :]
    vres = (vgate, vf_ref[...].reshape(rows, D_A)) if has_vres else None
    r, k2, v, aa, bbv, lw, bonus = _timemix_pre(
        z_all, z_prev.reshape(rows, N_SHIFT), mu_ref[...], w0_ref[...], wup_ref[...], a0_ref[...], aup_ref[...],
        kk_ref[...], ka_ref[...], rk_ref[...], vres, ones_g, fill)
    v_ref[...] = v.reshape(bb, C, D_A)

    p1 = lw.astype(BF16)
    r1 = lw - p1.astype(F32)
    p2 = r1.astype(BF16)
    p3 = (r1 - p2.astype(F32)).astype(BF16)
    cum = jnp.concatenate(
        [_dg(tri, p1[bi * C:(bi + 1) * C], NN)
         + (_dg(tri, p2[bi * C:(bi + 1) * C], NN) + _dg(tri, p3[bi * C:(bi + 1) * C], NN))
         for bi in range(bb)], axis=0)
    fill()
    cum3 = cum.reshape(bb, C, D_A)
    cum_c = cum3[:, C - 1:C, :]
    wc = jnp.exp(cum)
    wi = jnp.exp(-cum)
    wrel = jnp.exp(cum_c - cum3).reshape(rows, D_A)
    wcc = jnp.exp(cum_c)
    fill()
    At = aa * jnp.exp(cum - lw)
    Rt = r * wc
    Bt = bbv * wi
    Kt = k2 * wi
    Bh = bbv * wrel
    Kh = k2 * wrel
    fill()

    chains = []
    for bi in range(bb):
        rs = slice(bi * C, (bi + 1) * C)
        for g in range(N_HGROUPS):
            sl = slice(g * GROUP_W, (g + 1) * GROUP_W)
            chains.append((At[rs, sl], Rt[rs, sl], Bt[rs, sl], Kt[rs, sl], Bh[rs, sl], Kh[rs, sl],
                           v[rs, sl], s_ref[bi, g], wcc[bi][:, sl]))

    res = _chunk_groups(chains, masks, fillers)

    for bi in range(bb):
        for g in range(N_HGROUPS):
            s_ref[bi, g] = res[bi * N_HGROUPS + g][1]
    y = jnp.concatenate(
        [jnp.concatenate([res[bi * N_HGROUPS + g][0] for g in range(N_HGROUPS)], axis=1) for bi in range(bb)],
        axis=0)
    d = y - _headsums([y], ones_g)[0] * (1.0 / HEAD)
    var = _headsums([d * d], ones_g)[0] * (1.0 / HEAD)
    y_ref[...] = (d * lax.rsqrt(var + GN_EPS) * gng_ref[...] + gnb_ref[...] + bonus).reshape(bb, C, D_A)


def _front_call(x3, w_in_all, layer, p, v_first, s0_bd, shift0, bb):
    B, T, _ = x3.shape
    C = WKV_CHUNK
    vres = p["vres"]
    has_vres = vres is not None

    def tok(w):
        return pl.BlockSpec((bb, C, w), lambda b, c: (b, c, 0))

    def full(a):
        return pl.BlockSpec(a.shape, lambda b, c: (0,) * a.ndim, pipeline_mode=pl.Buffered(1))

    w_spec = pl.BlockSpec((None, D_MODEL, D_IN), lambda b, c: (layer, 0, 0), pipeline_mode=pl.Buffered(1))
    s_spec = pl.BlockSpec((bb, N_HGROUPS, GROUP_W, GROUP_W), lambda b, c: (b, 0, 0, 0))
    row_spec = pl.BlockSpec((bb, 1, N_SHIFT), lambda b, c: (b, 0, 0))
    consts = [p["ln_g"], p["ln_b"], p["sp_w"], p["sp_bT"], p["w_br_b"]]
    in_specs = [tok(D_MODEL), full(p["norm_g"]), w_spec] + [full(a) for a in consts]
    args = [x3, p["norm_g"], w_in_all] + consts
    if has_vres:
        in_specs += [full(a) for a in vres] + [tok(D_A)]
        args += list(vres) + [v_first]
    in_specs += [s_spec, row_spec]
    args += [s0_bd, shift0]
    for name in ("mu", "w0", "wup", "a0", "aup", "k_k", "k_a", "r_k", "gn_g", "gn_b"):
        in_specs.append(full(p[name]))
        args.append(p[name])
    widths = [D_A, D_MODEL, D_MODEL, D_A, D_A]
    return pl.pallas_call(
        functools.partial(_front_kernel, bb=bb, has_vres=has_vres),
        grid=(B // bb, T // C),
        in_specs=in_specs,
        out_specs=[tok(w) for w in widths] + [row_spec, s_spec],
        out_shape=[jax.ShapeDtypeStruct((B, T, w), GATE_DTYPE if j < 3 else F32) for j, w in enumerate(widths)]
        + [jax.ShapeDtypeStruct((B, 1, N_SHIFT), F32), jax.ShapeDtypeStruct(s0_bd.shape, F32)],
        scratch_shapes=[pltpu.VMEM((bb, 1, N_SHIFT), F32), pltpu.VMEM((bb, C, D_B), F32)],
        compiler_params=pltpu.CompilerParams(
            dimension_semantics=("arbitrary", "arbitrary"), vmem_limit_bytes=VMEM_LIMIT_BYTES),
        name="front_vres" if has_vres else "front",
    )(*args)


def _wkv_pre_kernel(*refs, has_vres):
    if has_vres:
        (z_ref, zp_ref, vg_ref, vf_ref, mu_ref, w0_ref, wup_ref, a0_ref, aup_ref, kk_ref, ka_ref, rk_ref,
         r_o, w_o, k_o, vt_o, a_o, b_o, bonus_o, v_o) = refs
        vres = (vg_ref[...], vf_ref[...])
    else:
        (z_ref, zp_ref, mu_ref, w0_ref, wup_ref, a0_ref, aup_ref, kk_ref, ka_ref, rk_ref,
         r_o, w_o, k_o, vt_o, a_o, b_o, bonus_o, v_o) = refs
        vres = None
    r, k2, v, aa, bbv, lw, bonus = _timemix_pre(
        z_ref[...], zp_ref[...], mu_ref[...], w0_ref[...], wup_ref[...], a0_ref[...], aup_ref[...],
        kk_ref[...], ka_ref[...], rk_ref[...], vres, _head_ones(HS_W))
    r_o[...] = r.T
    w_o[...] = jnp.exp(lw).T
    k_o[...] = k2.T
    vt_o[...] = v.T
    a_o[...] = aa.T
    b_o[...] = bbv.T
    bonus_o[...] = bonus.T
    v_o[...] = v


def _wkv_pre_call(z, z_prev, vres, p):
    n = z.shape[0]
    has_vres = vres is not None
    args = [z, z_prev] + (list(vres) if has_vres else [])
    args += [p[k] for k in ("mu", "w0", "wup", "a0", "aup", "k_k", "k_a", "r_k")]
    return pl.pallas_call(
        functools.partial(_wkv_pre_kernel, has_vres=has_vres),
        out_shape=[jax.ShapeDtypeStruct((D_A, n), F32)] * 7 + [jax.ShapeDtypeStruct((n, D_A), F32)],
        compiler_params=pltpu.CompilerParams(vmem_limit_bytes=VMEM_LIMIT_BYTES),
        name="wkv_pre_vres" if has_vres else "wkv_pre",
    )(*args)


def _wkv_step_kernel(s_ref, r_ref, w_ref, k_ref, v_ref, a_ref, b_ref, bonus_ref, gng_ref, gnb_ref,
                     y_ref, so_ref, y_scr):
    aT = a_ref[...]
    wT = w_ref[...]
    bT = b_ref[...]
    kT = k_ref[...]
    rT = r_ref[...]

    def body(i, carry):
        si = s_ref[i]
        sa = jnp.sum(si * aT, axis=0, keepdims=True)
        s2 = si * wT + sa * bT + v_ref[pl.ds(i, 1), :] * kT
        so_ref[i] = s2
        y_scr[pl.ds(i, 1), :] = jnp.sum(s2 * rT, axis=0, keepdims=True)
        return carry

    lax.fori_loop(0, HEAD, body, 0, unroll=4)
    y = y_scr[...]
    d = y - jnp.mean(y, axis=0, keepdims=True)
    var = jnp.mean(d * d, axis=0, keepdims=True)
    y_ref[...] = d * lax.rsqrt(var + GN_EPS) * gng_ref[...] + gnb_ref[...] + bonus_ref[...]


def _wkv_step_call(s_all, layer, vecs, gng_col, gnb_col):
    n = s_all.shape[-1]
    vec_spec = pl.BlockSpec((HEAD, n), lambda h: (h, 0))
    col_spec = pl.BlockSpec((HEAD, 1), lambda h: (h, 0))
    s_in = pl.BlockSpec((None, None, HEAD, HEAD, n), lambda h: (layer, h, 0, 0, 0))
    s_out = pl.BlockSpec((None, HEAD, HEAD, n), lambda h: (h, 0, 0, 0))
    return pl.pallas_call(
        _wkv_step_kernel,
        grid=(N_HEADS,),
        in_specs=[s_in] + [vec_spec] * 7 + [col_spec, col_spec],
        out_specs=[vec_spec, s_out],
        out_shape=[jax.ShapeDtypeStruct((D_A, n), F32), jax.ShapeDtypeStruct(s_all.shape[1:], F32)],
        scratch_shapes=[pltpu.VMEM((HEAD, n), F32)],
        compiler_params=pltpu.CompilerParams(
            dimension_semantics=("arbitrary",), vmem_limit_bytes=VMEM_LIMIT_BYTES),
        name="wkv_step",
    )(s_all, *vecs, gng_col, gnb_col)


def _merge_kernel(*refs, final):
    (h_ref, ya_ref, ga_ref, ma_ref, pb_ref, p_ref, wba_ref, wout_ref, wple_ref, wpg_ref, bpg_ref) = refs[:11]
    rest = refs[11:]
    if final:
        fg_ref, out_ref = rest
    else:
        (out_ref,) = rest
    br_a = jnp.dot((ya_ref[...] * ga_ref[...].astype(F32)).astype(BF16), wba_ref[...],
                   preferred_element_type=F32)
    merged = ma_ref[...].astype(F32) * br_a + pb_ref[...].astype(F32)
    h = h_ref[...] + jnp.dot(merged.astype(BF16), wout_ref[...], preferred_element_type=F32)
    ple = jnp.dot(p_ref[...].astype(BF16), wple_ref[...], preferred_element_type=F32)
    gate = _sigmoid(jnp.dot(h.astype(BF16), wpg_ref[...], preferred_element_type=F32) + bpg_ref[...])
    h = h + gate * ple
    if final:
        out_ref[...] = h * lax.rsqrt(jnp.mean(h * h, axis=-1, keepdims=True) + EPS) * fg_ref[...]
    else:
        out_ref[...] = h


def _merge_call(h, ya, ga, ma, pb, pe_all, layer, wl, final_g, tm):
    n = h.shape[0]
    final = final_g is not None

    def rows(w):
        return pl.BlockSpec((tm, w), lambda i: (i, 0))

    def full(a):
        return pl.BlockSpec(a.shape, lambda i: (0,) * a.ndim, pipeline_mode=pl.Buffered(1))

    row_args = [h, ya, ga, ma, pb]
    pe_spec = pl.BlockSpec((tm, D_PLE), lambda i: (layer * (n // tm) + i, 0))
    w_args = [wl["w_br_a"], wl["w_out"], wl["w_ple"], wl["w_pg"], wl["b_pg"]]
    if final:
        w_args.append(final_g)
    return pl.pallas_call(
        functools.partial(_merge_kernel, final=final),
        grid=(n // tm,),
        in_specs=[rows(a.shape[1]) for a in row_args] + [pe_spec] + [full(a) for a in w_args],
        out_specs=rows(D_MODEL),
        out_shape=jax.ShapeDtypeStruct((n, D_MODEL), F32),
        compiler_params=pltpu.CompilerParams(
            dimension_semantics=("arbitrary",), vmem_limit_bytes=VMEM_LIMIT_BYTES),
        name="merge" + ("_f" if final else ""),
    )(*row_args, pe_all, *w_args)


def _layer_params(W, l):
    row = lambda a: a.reshape(1, -1)
    zpad = jnp.zeros((D_LORA, D_A), F32)
    p = dict(
        norm_g=row(W["norm_g"][l]),
        ln_g=row(W["ln_v_g"][l]), ln_b=row(W["ln_v_b"][l]),
        mu=row(W["shift_mu"][l]), w0=row(W["w0"][l]), a0=row(W["a0"][l]),
        wup=jnp.concatenate([W["w_up"][l], zpad], axis=0).astype(BF16),
        aup=jnp.concatenate([zpad, W["a_up"][l]], axis=0).astype(BF16),
        k_k=row(W["k_k"][l]), k_a=row(W["k_a"][l]), r_k=row(W["r_k"][l]),
        gn_g=row(W["gn_g"][l]), gn_b=row(W["gn_b"][l]),
        w_br_a=W["w_br_a"][l].astype(BF16), w_br_b=W["w_br_b"][l].astype(BF16),
        w_out=W["w_out"][l].astype(BF16), w_ple=W["w_ple"][l].astype(BF16),
        w_pg=W["w_ple_gate"][l].astype(BF16), b_pg=row(W["b_ple_gate"][l]),
        sp_w=W["w_spatial"][l], sp_bT=W["b_spatial"][l].T,
        sp_w0=jnp.repeat(W["w_spatial"][l][:, 0, 0], GROUP_B).reshape(1, D_B),
        sp_b0=jnp.repeat(W["b_spatial"][l][:, 0], GROUP_B).reshape(1, D_B),
    )
    if l > 0:
        p["vres"] = (W["vres_down"][l - 1].astype(BF16), W["vres_up"][l - 1].astype(BF16),
                     row(W["vres_b"][l - 1]))
    else:
        p["vres"] = None
    return p


def _diag_blocks(s_bd):
    B = s_bd.shape[0]
    blocks = [s_bd[:, :, h * HEAD:(h + 1) * HEAD, h * HEAD:(h + 1) * HEAD] for h in range(HEADS_PER_GROUP)]
    return jnp.stack(blocks, axis=2).reshape(B, N_HEADS, HEAD, HEAD)


def _forward_prompt(x, pe, W, params, w_in_all):
    B, T, _ = x.shape
    n = B * T
    h = x.reshape(n, D_MODEL)
    pe_all = pe.reshape(-1, D_PLE)
    v_first = None
    wkvs, shifts = [], []
    depth = len(params)
    s0 = jnp.zeros((B, N_HGROUPS, GROUP_W, GROUP_W), F32)
    shift0 = jnp.zeros((B, 1, N_SHIFT), F32)
    for l, p in enumerate(params):
        ga, ma, pb, ya, v, z_last, s_bd = _front_call(
            h.reshape(B, T, D_MODEL), w_in_all, l, p, v_first, s0, shift0, bb=8)
        if l == 0:
            v_first = v
        wkvs.append(_diag_blocks(s_bd))
        shifts.append(z_last.reshape(B, N_SHIFT))
        fg = W["final_g"].reshape(1, -1) if l == depth - 1 else None
        h = _merge_call(h, ya.reshape(n, D_A), ga.reshape(n, D_A), ma.reshape(n, D_MODEL),
                        pb.reshape(n, D_MODEL), pe_all, l, p, fg, tm=1024)
    y = h.reshape(B, T, D_MODEL)
    return y, jnp.stack(wkvs), jnp.stack(shifts)


def _forward_sample(x, pe, wkv0, shift0, W, params, w_in_all):
    B = x.shape[0]
    h = x.reshape(B, D_MODEL)
    pe_all = pe.reshape(-1, D_PLE)
    s_all = jnp.transpose(wkv0, (0, 2, 3, 4, 1))
    v_first = None
    wkvs, shifts, chunk_vs = [], [], []
    depth = len(params)
    for l, p in enumerate(params):
        outs = _proj_call(h, w_in_all, l, p, chunked=False, tm=B)
        z, ga, ma, pb, vn = outs[:5]
        vres = None if l == 0 else (outs[5], v_first)
        pre = _wkv_pre_call(z, shift0[l], vres, p)
        if l == 0:
            v_first = pre[7]
        ya_t, s_new = _wkv_step_call(s_all, l, pre[:7], p["gn_g"].reshape(D_A, 1), p["gn_b"].reshape(D_A, 1))
        wkvs.append(s_new)
        shifts.append(z)
        chunk_vs.append(vn.reshape(B, 1, D_B))
        fg = W["final_g"].reshape(1, -1) if l == depth - 1 else None
        h = _merge_call(h, ya_t.T, ga, ma, pb, pe_all, l, p, fg, tm=B)
    y = h.reshape(B, 1, D_MODEL)
    wkv_new = jnp.transpose(jnp.stack(wkvs), (0, 4, 1, 2, 3))
    return y, wkv_new, jnp.stack(shifts), jnp.stack(chunk_vs)


def kernel(x_prompt, x_sample, state_rwkv_wkv, state_rwkv_shift, p_prompt, p_sample, norm_g, w_in, shift_mu, w0, w_up, a0, a_up, vres_down, vres_up, vres_b, k_k, k_a, r_k, gn_g, gn_b, ln_v_g, ln_v_b, w_spatial, b_spatial, w_br_a, w_br_b, w_out, w_ple, w_ple_gate, b_ple_gate, final_g):
    W = dict(norm_g=norm_g, w_in=w_in, shift_mu=shift_mu, w0=w0, w_up=w_up, a0=a0, a_up=a_up,
             vres_down=vres_down, vres_up=vres_up, vres_b=vres_b, k_k=k_k, k_a=k_a, r_k=r_k,
             gn_g=gn_g, gn_b=gn_b, ln_v_g=ln_v_g, ln_v_b=ln_v_b, w_spatial=w_spatial,
             b_spatial=b_spatial, w_br_a=w_br_a, w_br_b=w_br_b, w_out=w_out, w_ple=w_ple,
             w_ple_gate=w_ple_gate, b_ple_gate=b_ple_gate, final_g=final_g)
    params = [_layer_params(W, l) for l in range(w_in.shape[0])]
    w_in_all = w_in.astype(BF16)
    y_p, wkv_p, shift_p = _forward_prompt(x_prompt, p_prompt, W, params, w_in_all)
    y_s, wkv_s, shift_s, chunk_v = _forward_sample(
        x_sample, p_sample, state_rwkv_wkv, state_rwkv_shift, W, params, w_in_all)
    return (y_p, y_s, wkv_p, shift_p, wkv_s, shift_s, chunk_v)
```

```python
import functools

import jax
import jax.numpy as jnp
from jax import lax
from jax.experimental import pallas as pl
from jax.experimental.pallas import tpu as pltpu

D_MODEL = 1024
HEAD = 64
N_HEADS = 8
D_A = N_HEADS * HEAD
D_LORA = 64
D_B = 512
N_GROUPS_B = 4
GROUP_B = D_B // N_GROUPS_B
SPATIAL_CHUNK = 128
D_PLE = 256
N_SHIFT = 3 * D_A + 2 * D_LORA
D_IN = N_SHIFT + D_A + 3 * D_B + 2 * D_MODEL
EPS = 1e-6
GN_EPS = 64e-5
LN_EPS = 1e-5

O_GA = N_SHIFT
O_U = O_GA + D_A
O_VB = O_U + D_B
O_GB = O_VB + D_B
O_MA = O_GB + D_B
O_MB = O_MA + D_MODEL

WKV_CHUNK = 64
HEADS_PER_GROUP = 2
GROUP_W = HEADS_PER_GROUP * HEAD
N_HGROUPS = N_HEADS // HEADS_PER_GROUP
HS_W = 256
FRONT_SEQS = 8
MERGE_ROWS = 1024
V7X_VMEM_BYTES = 64 * 1024 * 1024
VMEM_LIMIT_BYTES = V7X_VMEM_BYTES - 8 * 1024 * 1024

F32 = jnp.float32
BF16 = jnp.bfloat16
GATE_DTYPE = BF16

NN = ((1,), (0,))
TN = ((0,), (0,))


def _dg(a, b, dims):
    return lax.dot_general(a, b, (dims, ((), ())), preferred_element_type=F32)


def _split(x):
    hi = x.astype(BF16)
    lo = (x - hi.astype(F32)).astype(BF16)
    return hi, lo


def _sigmoid(x):
    return 0.5 + 0.5 * jnp.tanh(0.5 * x)


def _gelu_tanh(x):
    return 0.5 * x * (1.0 + jnp.tanh(0.7978845608028654 * (x + 0.044715 * (x * x * x))))


def _softplus(x):
    return jnp.maximum(x, 0.0) + jnp.log(1.0 + jnp.exp(-jnp.abs(x)))


def _head_ones(n):
    r = lax.broadcasted_iota(jnp.int32, (n, n), 0) // HEAD
    c = lax.broadcasted_iota(jnp.int32, (n, n), 1) // HEAD
    return jnp.where(r == c, 1.0, 0.0).astype(BF16)


def _headsums(xs, ones_g):
    rows = xs[0].shape[0]
    pieces = []
    for x in xs:
        hi, lo = _split(x)
        pieces += [hi[:, :HS_W], lo[:, :HS_W], hi[:, HS_W:], lo[:, HS_W:]]
    o = _dg(jnp.concatenate(pieces, axis=0), ones_g, NN)
    outs = []
    for i in range(len(xs)):
        q = [o[(4 * i + j) * rows:(4 * i + j + 1) * rows] for j in range(4)]
        outs.append(jnp.concatenate([q[0] + q[1], q[2] + q[3]], axis=1))
    return outs


def _proj_kernel(*refs, has_vres):
    (x_ref, g_ref, w_ref, lng_ref, lnb_ref, sp_a_ref, sp_b_ref, wbb_ref) = refs[:8]
    rest = refs[8:]
    if has_vres:
        vd_ref, vu_ref, vb_ref = rest[:3]
        rest = rest[3:]
    z_ref, ga_ref, ma_ref, pb_ref, vn_ref = rest[:5]
    rest = rest[5:]
    x = x_ref[...]
    xn = x * lax.rsqrt(jnp.mean(x * x, axis=-1, keepdims=True) + EPS) * g_ref[...]
    xb = xn.astype(BF16)

    def seg(lo, hi):
        return jnp.dot(xb, w_ref[:, lo:hi], preferred_element_type=F32)

    z_ref[...] = seg(0, N_SHIFT)
    t = seg(O_GA, O_U)
    ga_ref[...] = (t * _sigmoid(t)).astype(ga_ref.dtype)
    ma_ref[...] = _sigmoid(seg(O_MA, O_MB)).astype(ma_ref.dtype)
    u = _gelu_tanh(seg(O_U, O_VB))
    vb = _gelu_tanh(seg(O_VB, O_GB))
    mu = jnp.mean(vb, axis=-1, keepdims=True)
    d = vb - mu
    var = jnp.mean(d * d, axis=-1, keepdims=True)
    vn = d * lax.rsqrt(var + LN_EPS) * lng_ref[...] + lnb_ref[...]
    vn_ref[...] = vn
    t = seg(O_GB, O_MA)
    mixed = vn * sp_a_ref[...] + sp_b_ref[...]
    yb = u * mixed * (t * _sigmoid(t))
    br_b = jnp.dot(yb.astype(BF16), wbb_ref[...], preferred_element_type=F32)
    pb_ref[...] = (_sigmoid(seg(O_MB, D_IN)) * br_b).astype(pb_ref.dtype)
    if has_vres:
        (vg_ref,) = rest
        low = jnp.dot(xb, vd_ref[...], preferred_element_type=F32)
        up = jnp.dot(low.astype(BF16), vu_ref[...], preferred_element_type=F32)
        vg_ref[...] = _sigmoid(vb_ref[...] + up)


def _proj_call(x2d, w_in_all, layer, p, tm):
    n = x2d.shape[0]
    vres = p["vres"]
    has_vres = vres is not None

    def full(a):
        return pl.BlockSpec(a.shape, lambda i: (0,) * a.ndim, pipeline_mode=pl.Buffered(1))

    def rows(w):
        return pl.BlockSpec((tm, w), lambda i: (i, 0))

    w_spec = pl.BlockSpec((None, D_MODEL, D_IN), lambda i: (layer, 0, 0), pipeline_mode=pl.Buffered(1))
    consts = [p["ln_g"], p["ln_b"], p["sp_w0"], p["sp_b0"], p["w_br_b"]] + (list(vres) if has_vres else [])
    widths = [N_SHIFT, D_A, D_MODEL, D_MODEL, D_B] + ([D_A] if has_vres else [])
    return pl.pallas_call(
        functools.partial(_proj_kernel, has_vres=has_vres),
        grid=(n // tm,),
        in_specs=[rows(D_MODEL), full(p["norm_g"]), w_spec] + [full(a) for a in consts],
        out_specs=[rows(w) for w in widths],
        out_shape=[jax.ShapeDtypeStruct((n, w), GATE_DTYPE if 1 <= j <= 3 else F32) for j, w in enumerate(widths)],
        compiler_params=pltpu.CompilerParams(
            dimension_semantics=("arbitrary",), vmem_limit_bytes=VMEM_LIMIT_BYTES),
        name="proj_s" + ("_vres" if has_vres else ""),
    )(x2d, p["norm_g"], w_in_all, *consts)


def _timemix_pre(z, z_prev, mu, w0, wup_pad, a0, aup_pad, k_k, k_a, r_k, vres, ones_g, fill=lambda: None):
    zs = z + (z_prev - z) * mu
    fill()
    r = zs[:, 0:D_A]
    k = zs[:, D_A:2 * D_A]
    v = zs[:, 2 * D_A:3 * D_A]
    lora_in = zs[:, 3 * D_A:N_SHIFT]
    wq = w0 + jnp.dot(jnp.tanh(lora_in).astype(BF16), wup_pad, preferred_element_type=F32)
    w = -_softplus(-wq) - 0.5
    lw = -jnp.exp(w)
    fill()
    a = _sigmoid(a0 + jnp.dot(lora_in.astype(BF16), aup_pad, preferred_element_type=F32))
    fill()
    if vres is not None:
        vgate, vfirst = vres
        v = v + (vfirst - v) * vgate
    kk = k * k_k
    k2 = k * (1.0 + (a - 1.0) * k_a)
    fill()
    ss, rk_sum = _headsums([kk * kk, r * k2 * r_k], ones_g)
    fill()
    kk = kk / jnp.maximum(jnp.sqrt(ss), 1e-12)
    return r, k2, v, -kk, kk * a, lw, rk_sum * v


def _chunk_groups(chains, masks, fillers):
    t_idx, s_idx, bdmask = masks
    fillers = list(fillers)

    def fill():
        if fillers:
            fillers.pop(0)()

    C = chains[0][0].shape[0]
    n = range(len(chains))
    At, Rt, Bt, Kt, Bh, Kh, V, S, wcc = (list(col) for col in zip(*chains))
    strict = s_idx < t_idx
    incl = s_idx <= t_idx

    def b16(x):
        return x.astype(BF16)

    def bd(x):
        return jnp.where(bdmask, jnp.concatenate([x] * HEADS_PER_GROUP, axis=0), 0)

    def bd_t(x):
        return jnp.where(bdmask, jnp.concatenate([x] * HEADS_PER_GROUP, axis=0).T, 0.0).astype(BF16)

    ar = [b16(jnp.concatenate([At[i], Rt[i]], axis=0)) for i in n]
    sc_b = [_dg(ar[i], bd_t(Bt[i]), NN) for i in n]
    fill()
    sc_k = [_dg(ar[i], bd_t(Kt[i]), NN) for i in n]
    fill()
    N = [jnp.where(strict, sc_b[i][:C], 0.0) for i in n]
    Srb = [jnp.where(incl, sc_b[i][C:], 0.0) for i in n]
    M = [jnp.where(strict, sc_k[i][:C], 0.0) for i in n]
    Srk = [jnp.where(incl, sc_k[i][C:], 0.0) for i in n]
    ar_s = [_dg(ar[i], b16(S[i].T), NN) for i in n]
    fill()
    mv = [_dg(b16(jnp.concatenate([M[i], Srk[i]], axis=0)), bd(b16(V[i])), NN) for i in n]
    X = [ar_s[i][:C] + mv[i][:C] for i in n]

    def level_mask(b):
        return ((t_idx // (2 * b)) == (s_idx // (2 * b))) & ((t_idx % (2 * b)) >= b) & ((s_idx % (2 * b)) < b)

    T = [jnp.where(s_idx == t_idx, 1.0, 0.0) + jnp.where(level_mask(1), N[i], 0.0) for i in n]
    b = 2
    while b < C:
        lm = level_mask(b)
        d = [b16(T[i]) for i in n]
        E = [_dg(b16(jnp.where(lm, N[i], 0.0)), bd(d[i]), NN) for i in n]
        fill()
        F = [_dg(d[i], bd(b16(E[i])), NN) for i in n]
        fill()
        T = [T[i] + F[i] for i in n]
        b *= 2

    U = [_dg(b16(T[i]), bd(b16(X[i])), NN) for i in n]
    fill()
    Y = [ar_s[i][C:] + mv[i][C:] + _dg(b16(Srb[i]), bd(b16(U[i])), NN) for i in n]
    upd = [_dg(b16(jnp.concatenate([U[i], V[i]], axis=0)),
               b16(jnp.concatenate([Bh[i], Kh[i]], axis=0)), TN) for i in n]
    while fillers:
        fill()
    return [(Y[i], S[i] * wcc[i] + jnp.where(bdmask, upd[i], 0.0)) for i in n]


def _front_kernel(*refs, bb, has_vres):
    (x_ref, g_ref, w_ref, lng_ref, lnb_ref, sp_a_ref, sp_b_ref, wbb_ref) = refs[:8]
    rest = refs[8:]
    if has_vres:
        vd_ref, vu_ref, vb_ref, vf_ref = rest[:4]
        rest = rest[4:]
    (s0_ref, sh0_ref, mu_ref, w0_ref, wup_ref, a0_ref, aup_ref, kk_ref, ka_ref, rk_ref, gng_ref, gnb_ref,
     ga_ref, ma_ref, pb_ref, y_ref, v_ref, zl_ref, s_ref, zp_scr, vn_scr) = rest
    C = x_ref.shape[1]
    rows = bb * C
    c = pl.program_id(1)

    @pl.when(c == 0)
    def _():
        s_ref[...] = s0_ref[...]
        zp_scr[...] = sh0_ref[...]
        vn_scr[...] = jnp.zeros(vn_scr.shape, F32)

    x = x_ref[...].reshape(rows, D_MODEL)
    xn = x * lax.rsqrt(jnp.mean(x * x, axis=-1, keepdims=True) + EPS) * g_ref[...]
    xb = xn.astype(BF16)

    def seg(lo, hi):
        return jnp.dot(xb, w_ref[:, lo:hi], preferred_element_type=F32)

    z_all = seg(0, N_SHIFT)
    if has_vres:
        low = jnp.dot(xb, vd_ref[...], preferred_element_type=F32)
        vgate = _sigmoid(vb_ref[...] + jnp.dot(low.astype(BF16), vu_ref[...], preferred_element_type=F32))

    ones_g = _head_ones(HS_W)
    t_idx = lax.broadcasted_iota(jnp.int32, (C, GROUP_W), 0)
    s_idx = lax.broadcasted_iota(jnp.int32, (C, GROUP_W), 1) % C
    rr = lax.broadcasted_iota(jnp.int32, (GROUP_W, GROUP_W), 0) // HEAD
    cc = lax.broadcasted_iota(jnp.int32, (GROUP_W, GROUP_W), 1) // HEAD
    masks = (t_idx, s_idx, rr == cc)
    tri = jnp.where(lax.broadcasted_iota(jnp.int32, (C, C), 1)
                    <= lax.broadcasted_iota(jnp.int32, (C, C), 0), 1.0, 0.0).astype(BF16)
    row0 = lax.broadcasted_iota(jnp.int32, (bb, C, N_SHIFT), 1) == 0

    half = c % 2
    box = {}

    def f_ga():
        t = seg(O_GA, O_U)
        ga_ref[...] = (t * _sigmoid(t)).astype(ga_ref.dtype).reshape(bb, C, D_A)

    def f_ma(lo, hi):
        def f():
            ma_ref[:, :, lo:hi] = _sigmoid(seg(O_MA + lo, O_MA + hi)).astype(ma_ref.dtype).reshape(bb, C, hi - lo)
        return f

    def f_u():
        box["u"] = _gelu_tanh(seg(O_U, O_VB))

    def f_vn():
        vb = _gelu_tanh(seg(O_VB, O_GB))
        d = vb - jnp.mean(vb, axis=-1, keepdims=True)
        var = jnp.mean(d * d, axis=-1, keepdims=True)
        box["vn"] = d * lax.rsqrt(var + LN_EPS) * lng_ref[...] + lnb_ref[...]

    def f_mix():
        vn = box["vn"]
        srow = lax.broadcasted_iota(jnp.int32, (C, SPATIAL_CHUNK), 0) + half * C
        scol = lax.broadcasted_iota(jnp.int32, (C, SPATIAL_CHUNK), 1)
        keep = scol <= srow
        wm = [jnp.where(keep, sp_a_ref[g, pl.ds(half * C, C), :], 0.0).astype(BF16) for g in range(N_GROUPS_B)]
        bias = sp_b_ref[pl.ds(half * C, C), :]
        outs = []
        for bi in range(bb):
            cur = vn[bi * C:(bi + 1) * C]
            first = jnp.where(half == 0, cur, vn_scr[bi])
            both = jnp.concatenate([first, cur], axis=0).astype(BF16)
            vn_scr[bi] = cur
            cols = [jnp.dot(wm[g], both[:, g * GROUP_B:(g + 1) * GROUP_B], preferred_element_type=F32)
                    + bias[:, g:g + 1] for g in range(N_GROUPS_B)]
            outs.append(jnp.concatenate(cols, axis=1))
        box["mixed"] = jnp.concatenate(outs, axis=0)

    def f_yb():
        t = seg(O_GB, O_MA)
        box["yb"] = (box["u"] * box["mixed"] * (t * _sigmoid(t))).astype(BF16)

    def f_pb(lo, hi):
        def f():
            br_b = jnp.dot(box["yb"], wbb_ref[:, lo:hi], preferred_element_type=F32)
            pb_ref[:, :, lo:hi] = (_sigmoid(seg(O_MB + lo, O_MB + hi)) * br_b).astype(pb_ref.dtype).reshape(bb, C, hi - lo)
        return f

    hm = D_MODEL // 2
    fillers = [f_ga, f_ma(0, hm), f_ma(hm, D_MODEL), f_u, f_vn, f_mix, f_yb, f_pb(0, hm), f_pb(hm, D_MODEL)]

    def fill():
        if fillers:
            fillers.pop(0)()

    z3 = z_all.reshape(bb, C, N_SHIFT)
    z_prev = jnp.where(row0, zp_scr[...], pltpu.roll(z_all, 1, 0).reshape(bb, C, N_SHIFT))
    zp_scr[...] = z3[:, C - 1:C, :]
    zl_ref[...] = z3[:, C - 1:C, :]
    vres = (vgate, vf_ref[...].reshape(rows, D_A)) if has_vres else None
    r, k2, v, aa, bbv, lw, bonus = _timemix_pre(
        z_all, z_prev.reshape(rows, N_SHIFT), mu_ref[...], w0_ref[...], wup_ref[...], a0_ref[...], aup_ref[...],
        kk_ref[...], ka_ref[...], rk_ref[...], vres, ones_g, fill)
    v_ref[...] = v.reshape(bb, C, D_A)

    p1 = lw.astype(BF16)
    r1 = lw - p1.astype(F32)
    p2 = r1.astype(BF16)
    p3 = (r1 - p2.astype(F32)).astype(BF16)
    fill()
    cum = jnp.concatenate(
        [_dg(tri, p1[bi * C:(bi + 1) * C], NN)
         + (_dg(tri, p2[bi * C:(bi + 1) * C], NN) + _dg(tri, p3[bi * C:(bi + 1) * C], NN))
         for bi in range(bb)], axis=0)
    fill()
    cum3 = cum.reshape(bb, C, D_A)
    cum_c = cum3[:, C - 1:C, :]
    wc = jnp.exp(cum)
    wi = jnp.exp(-cum)
    wrel = jnp.exp(cum_c - cum3).reshape(rows, D_A)
    wcc = jnp.exp(cum_c)
    fill()
    At = aa * jnp.exp(cum - lw)
    Rt = r * wc
    Bt = bbv * wi
    Kt = k2 * wi
    Bh = bbv * wrel
    Kh = k2 * wrel
    fill()

    chains = []
    for bi in range(bb):
        rs = slice(bi * C, (bi + 1) * C)
        for g in range(N_HGROUPS):
            sl = slice(g * GROUP_W, (g + 1) * GROUP_W)
            chains.append((At[rs, sl], Rt[rs, sl], Bt[rs, sl], Kt[rs, sl], Bh[rs, sl], Kh[rs, sl],
                           v[rs, sl], s_ref[bi, g], wcc[bi][:, sl]))

    res = _chunk_groups(chains, masks, fillers)

    for bi in range(bb):
        for g in range(N_HGROUPS):
            s_ref[bi, g] = res[bi * N_HGROUPS + g][1]
    y = jnp.concatenate(
        [jnp.concatenate([res[bi * N_HGROUPS + g][0] for g in range(N_HGROUPS)], axis=1) for bi in range(bb)],
        axis=0)
    d = y - _headsums([y], ones_g)[0] * (1.0 / HEAD)
    var = _headsums([d * d], ones_g)[0] * (1.0 / HEAD)
    y_ref[...] = (d * lax.rsqrt(var + GN_EPS) * gng_ref[...] + gnb_ref[...] + bonus).reshape(bb, C, D_A)


def _front_call(x3, w_in_all, layer, p, v_first, s0_bd, shift0, bb):
    B, T, _ = x3.shape
    C = WKV_CHUNK
    vres = p["vres"]
    has_vres = vres is not None

    def tok(w):
        return pl.BlockSpec((bb, C, w), lambda b, c: (b, c, 0))

    def full(a):
        return pl.BlockSpec(a.shape, lambda b, c: (0,) * a.ndim, pipeline_mode=pl.Buffered(1))

    w_spec = pl.BlockSpec((None, D_MODEL, D_IN), lambda b, c: (layer, 0, 0), pipeline_mode=pl.Buffered(1))
    s_spec = pl.BlockSpec((bb, N_HGROUPS, GROUP_W, GROUP_W), lambda b, c: (b, 0, 0, 0))
    row_spec = pl.BlockSpec((bb, 1, N_SHIFT), lambda b, c: (b, 0, 0))
    consts = [p["ln_g"], p["ln_b"], p["sp_w"], p["sp_bT"], p["w_br_b"]]
    in_specs = [tok(D_MODEL), full(p["norm_g"]), w_spec] + [full(a) for a in consts]
    args = [x3, p["norm_g"], w_in_all] + consts
    if has_vres:
        in_specs += [full(a) for a in vres] + [tok(D_A)]
        args += list(vres) + [v_first]
    in_specs += [s_spec, row_spec]
    args += [s0_bd, shift0]
    for name in ("mu", "w0", "wup", "a0", "aup", "k_k", "k_a", "r_k", "gn_g", "gn_b"):
        in_specs.append(full(p[name]))
        args.append(p[name])
    widths = [D_A, D_MODEL, D_MODEL, D_A, D_A]
    return pl.pallas_call(
        functools.partial(_front_kernel, bb=bb, has_vres=has_vres),
        grid=(B // bb, T // C),
        in_specs=in_specs,
        out_specs=[tok(w) for w in widths] + [row_spec, s_spec],
        out_shape=[jax.ShapeDtypeStruct((B, T, w), GATE_DTYPE if j < 3 else F32) for j, w in enumerate(widths)]
        + [jax.ShapeDtypeStruct((B, 1, N_SHIFT), F32), jax.ShapeDtypeStruct(s0_bd.shape, F32)],
        scratch_shapes=[pltpu.VMEM((bb, 1, N_SHIFT), F32), pltpu.VMEM((bb, C, D_B), F32)],
        compiler_params=pltpu.CompilerParams(
            dimension_semantics=("arbitrary", "arbitrary"), vmem_limit_bytes=VMEM_LIMIT_BYTES),
        name="front_vres" if has_vres else "front",
    )(*args)


def _wkv_pre_kernel(*refs, has_vres):
    if has_vres:
        (z_ref, zp_ref, vg_ref, vf_ref, mu_ref, w0_ref, wup_ref, a0_ref, aup_ref, kk_ref, ka_ref, rk_ref,
         r_o, w_o, k_o, vt_o, a_o, b_o, bonus_o, v_o) = refs
        vres = (vg_ref[...], vf_ref[...])
    else:
        (z_ref, zp_ref, mu_ref, w0_ref, wup_ref, a0_ref, aup_ref, kk_ref, ka_ref, rk_ref,
         r_o, w_o, k_o, vt_o, a_o, b_o, bonus_o, v_o) = refs
        vres = None
    r, k2, v, aa, bbv, lw, bonus = _timemix_pre(
        z_ref[...], zp_ref[...], mu_ref[...], w0_ref[...], wup_ref[...], a0_ref[...], aup_ref[...],
        kk_ref[...], ka_ref[...], rk_ref[...], vres, _head_ones(HS_W))
    r_o[...] = r.T
    w_o[...] = jnp.exp(lw).T
    k_o[...] = k2.T
    vt_o[...] = v.T
    a_o[...] = aa.T
    b_o[...] = bbv.T
    bonus_o[...] = bonus.T
    v_o[...] = v


def _wkv_pre_call(z, z_prev, vres, p):
    n = z.shape[0]
    has_vres = vres is not None
    args = [z, z_prev] + (list(vres) if has_vres else [])
    args += [p[k] for k in ("mu", "w0", "wup", "a0", "aup", "k_k", "k_a", "r_k")]
    return pl.pallas_call(
        functools.partial(_wkv_pre_kernel, has_vres=has_vres),
        out_shape=[jax.ShapeDtypeStruct((D_A, n), F32)] * 7 + [jax.ShapeDtypeStruct((n, D_A), F32)],
        compiler_params=pltpu.CompilerParams(vmem_limit_bytes=VMEM_LIMIT_BYTES),
        name="wkv_pre_vres" if has_vres else "wkv_pre",
    )(*args)


def _wkv_step_kernel(s_ref, r_ref, w_ref, k_ref, v_ref, a_ref, b_ref, bonus_ref, gng_ref, gnb_ref,
                     y_ref, so_ref, y_scr):
    aT = a_ref[...]
    wT = w_ref[...]
    bT = b_ref[...]
    kT = k_ref[...]
    rT = r_ref[...]

    def body(i, carry):
        si = s_ref[i]
        sa = jnp.sum(si * aT, axis=0, keepdims=True)
        s2 = si * wT + sa * bT + v_ref[pl.ds(i, 1), :] * kT
        so_ref[i] = s2
        y_scr[pl.ds(i, 1), :] = jnp.sum(s2 * rT, axis=0, keepdims=True)
        return carry

    lax.fori_loop(0, HEAD, body, 0, unroll=4)
    y = y_scr[...]
    d = y - jnp.mean(y, axis=0, keepdims=True)
    var = jnp.mean(d * d, axis=0, keepdims=True)
    y_ref[...] = d * lax.rsqrt(var + GN_EPS) * gng_ref[...] + gnb_ref[...] + bonus_ref[...]


def _wkv_step_call(s_all, layer, vecs, gng_col, gnb_col):
    n = s_all.shape[-1]
    vec_spec = pl.BlockSpec((HEAD, n), lambda h: (h, 0))
    col_spec = pl.BlockSpec((HEAD, 1), lambda h: (h, 0))
    s_in = pl.BlockSpec((None, None, HEAD, HEAD, n), lambda h: (layer, h, 0, 0, 0))
    s_out = pl.BlockSpec((None, HEAD, HEAD, n), lambda h: (h, 0, 0, 0))
    return pl.pallas_call(
        _wkv_step_kernel,
        grid=(N_HEADS,),
        in_specs=[s_in] + [vec_spec] * 7 + [col_spec, col_spec],
        out_specs=[vec_spec, s_out],
        out_shape=[jax.ShapeDtypeStruct((D_A, n), F32), jax.ShapeDtypeStruct(s_all.shape[1:], F32)],
        scratch_shapes=[pltpu.VMEM((HEAD, n), F32)],
        compiler_params=pltpu.CompilerParams(
            dimension_semantics=("arbitrary",), vmem_limit_bytes=VMEM_LIMIT_BYTES),
        name="wkv_step",
    )(s_all, *vecs, gng_col, gnb_col)


def _merge_kernel(*refs, final):
    (h_ref, ya_ref, ga_ref, ma_ref, pb_ref, p_ref, wba_ref, wout_ref, wple_ref, wpg_ref, bpg_ref) = refs[:11]
    rest = refs[11:]
    if final:
        fg_ref, out_ref = rest
    else:
        (out_ref,) = rest
    br_a = jnp.dot((ya_ref[...] * ga_ref[...].astype(F32)).astype(BF16), wba_ref[...],
                   preferred_element_type=F32)
    merged = ma_ref[...].astype(F32) * br_a + pb_ref[...].astype(F32)
    h = h_ref[...] + jnp.dot(merged.astype(BF16), wout_ref[...], preferred_element_type=F32)
    ple = jnp.dot(p_ref[...].astype(BF16), wple_ref[...], preferred_element_type=F32)
    gate = _sigmoid(jnp.dot(h.astype(BF16), wpg_ref[...], preferred_element_type=F32) + bpg_ref[...])
    h = h + gate * ple
    if final:
        out_ref[...] = h * lax.rsqrt(jnp.mean(h * h, axis=-1, keepdims=True) + EPS) * fg_ref[...]
    else:
        out_ref[...] = h


def _merge_call(h, ya, ga, ma, pb, pe_all, layer, wl, final_g, tm):
    n = h.shape[0]
    final = final_g is not None

    def rows(w):
        return pl.BlockSpec((tm, w), lambda i: (i, 0))

    def full(a):
        return pl.BlockSpec(a.shape, lambda i: (0,) * a.ndim, pipeline_mode=pl.Buffered(1))

    row_args = [h, ya, ga, ma, pb]
    pe_spec = pl.BlockSpec((tm, D_PLE), lambda i: (layer * (n // tm) + i, 0))
    w_args = [wl["w_br_a"], wl["w_out"], wl["w_ple"], wl["w_pg"], wl["b_pg"]]
    if final:
        w_args.append(final_g)
    return pl.pallas_call(
        functools.partial(_merge_kernel, final=final),
        grid=(n // tm,),
        in_specs=[rows(a.shape[1]) for a in row_args] + [pe_spec] + [full(a) for a in w_args],
        out_specs=rows(D_MODEL),
        out_shape=jax.ShapeDtypeStruct((n, D_MODEL), F32),
        compiler_params=pltpu.CompilerParams(
            dimension_semantics=("arbitrary",), vmem_limit_bytes=VMEM_LIMIT_BYTES),
        name="merge" + ("_f" if final else ""),
    )(*row_args, pe_all, *w_args)


def _layer_params(W, l):
    row = lambda a: a.reshape(1, -1)
    zpad = jnp.zeros((D_LORA, D_A), F32)
    p = dict(
        norm_g=row(W["norm_g"][l]),
        ln_g=row(W["ln_v_g"][l]), ln_b=row(W["ln_v_b"][l]),
        mu=row(W["shift_mu"][l]), w0=row(W["w0"][l]), a0=row(W["a0"][l]),
        wup=jnp.concatenate([W["w_up"][l], zpad], axis=0).astype(BF16),
        aup=jnp.concatenate([zpad, W["a_up"][l]], axis=0).astype(BF16),
        k_k=row(W["k_k"][l]), k_a=row(W["k_a"][l]), r_k=row(W["r_k"][l]),
        gn_g=row(W["gn_g"][l]), gn_b=row(W["gn_b"][l]),
        w_br_a=W["w_br_a"][l].astype(BF16), w_br_b=W["w_br_b"][l].astype(BF16),
        w_out=W["w_out"][l].astype(BF16), w_ple=W["w_ple"][l].astype(BF16),
        w_pg=W["w_ple_gate"][l].astype(BF16), b_pg=row(W["b_ple_gate"][l]),
        sp_w=W["w_spatial"][l], sp_bT=W["b_spatial"][l].T,
        sp_w0=jnp.repeat(W["w_spatial"][l][:, 0, 0], GROUP_B).reshape(1, D_B),
        sp_b0=jnp.repeat(W["b_spatial"][l][:, 0], GROUP_B).reshape(1, D_B),
    )
    if l > 0:
        p["vres"] = (W["vres_down"][l - 1].astype(BF16), W["vres_up"][l - 1].astype(BF16),
                     row(W["vres_b"][l - 1]))
    else:
        p["vres"] = None
    return p


def _diag_blocks(s_bd):
    B = s_bd.shape[0]
    blocks = [s_bd[:, :, h * HEAD:(h + 1) * HEAD, h * HEAD:(h + 1) * HEAD] for h in range(HEADS_PER_GROUP)]
    return jnp.stack(blocks, axis=2).reshape(B, N_HEADS, HEAD, HEAD)


def _forward_prompt(x, pe, W, params, w_in_all):
    B, T, _ = x.shape
    n = B * T
    assert B % FRONT_SEQS == 0 and T % SPATIAL_CHUNK == 0 and n % MERGE_ROWS == 0, (B, T)
    h = x.reshape(n, D_MODEL)
    pe_all = pe.reshape(-1, D_PLE)
    v_first = None
    wkvs, shifts = [], []
    depth = len(params)
    s0 = jnp.zeros((B, N_HGROUPS, GROUP_W, GROUP_W), F32)
    shift0 = jnp.zeros((B, 1, N_SHIFT), F32)
    for l, p in enumerate(params):
        ga, ma, pb, ya, v, z_last, s_bd = _front_call(
            h.reshape(B, T, D_MODEL), w_in_all, l, p, v_first, s0, shift0, bb=FRONT_SEQS)
        if l == 0:
            v_first = v
        wkvs.append(_diag_blocks(s_bd))
        shifts.append(z_last.reshape(B, N_SHIFT))
        fg = W["final_g"].reshape(1, -1) if l == depth - 1 else None
        h = _merge_call(h, ya.reshape(n, D_A), ga.reshape(n, D_A), ma.reshape(n, D_MODEL),
                        pb.reshape(n, D_MODEL), pe_all, l, p, fg, tm=MERGE_ROWS)
    y = h.reshape(B, T, D_MODEL)
    return y, jnp.stack(wkvs), jnp.stack(shifts)


def _forward_sample(x, pe, wkv0, shift0, W, params, w_in_all):
    B = x.shape[0]
    h = x.reshape(B, D_MODEL)
    pe_all = pe.reshape(-1, D_PLE)
    s_all = jnp.transpose(wkv0, (0, 2, 3, 4, 1))
    v_first = None
    wkvs, shifts, chunk_vs = [], [], []
    depth = len(params)
    for l, p in enumerate(params):
        outs = _proj_call(h, w_in_all, l, p, tm=B)
        z, ga, ma, pb, vn = outs[:5]
        vres = None if l == 0 else (outs[5], v_first)
        pre = _wkv_pre_call(z, shift0[l], vres, p)
        if l == 0:
            v_first = pre[7]
        ya_t, s_new = _wkv_step_call(s_all, l, pre[:7], p["gn_g"].reshape(D_A, 1), p["gn_b"].reshape(D_A, 1))
        wkvs.append(s_new)
        shifts.append(z)
        chunk_vs.append(vn.reshape(B, 1, D_B))
        fg = W["final_g"].reshape(1, -1) if l == depth - 1 else None
        h = _merge_call(h, ya_t.T, ga, ma, pb, pe_all, l, p, fg, tm=B)
    y = h.reshape(B, 1, D_MODEL)
    wkv_new = jnp.transpose(jnp.stack(wkvs), (0, 4, 1, 2, 3))
    return y, wkv_new, jnp.stack(shifts), jnp.stack(chunk_vs)


def kernel(x_prompt, x_sample, state_rwkv_wkv, state_rwkv_shift, p_prompt, p_sample, norm_g, w_in, shift_mu, w0, w_up, a0, a_up, vres_down, vres_up, vres_b, k_k, k_a, r_k, gn_g, gn_b, ln_v_g, ln_v_b, w_spatial, b_spatial, w_br_a, w_br_b, w_out, w_ple, w_ple_gate, b_ple_gate, final_g):
    W = dict(norm_g=norm_g, w_in=w_in, shift_mu=shift_mu, w0=w0, w_up=w_up, a0=a0, a_up=a_up,
             vres_down=vres_down, vres_up=vres_up, vres_b=vres_b, k_k=k_k, k_a=k_a, r_k=r_k,
             gn_g=gn_g, gn_b=gn_b, ln_v_g=ln_v_g, ln_v_b=ln_v_b, w_spatial=w_spatial,
             b_spatial=b_spatial, w_br_a=w_br_a, w_br_b=w_br_b, w_out=w_out, w_ple=w_ple,
             w_ple_gate=w_ple_gate, b_ple_gate=b_ple_gate, final_g=final_g)
    params = [_layer_params(W, l) for l in range(w_in.shape[0])]
    w_in_all = w_in.astype(BF16)
    y_p, wkv_p, shift_p = _forward_prompt(x_prompt, p_prompt, W, params, w_in_all)
    y_s, wkv_s, shift_s, chunk_v = _forward_sample(
        x_sample, p_sample, state_rwkv_wkv, state_rwkv_shift, W, params, w_in_all)
    return (y_p, y_s, wkv_p, shift_p, wkv_s, shift_s, chunk_v)
```

```python
import functools

import jax
import jax.numpy as jnp
from jax import lax
from jax.experimental import pallas as pl
from jax.experimental.pallas import tpu as pltpu

D_MODEL = 1024
HEAD = 64
N_HEADS = 8
D_A = N_HEADS * HEAD
D_LORA = 64
D_B = 512
N_GROUPS_B = 4
GROUP_B = D_B // N_GROUPS_B
SPATIAL_CHUNK = 128
D_PLE = 256
N_SHIFT = 3 * D_A + 2 * D_LORA
D_IN = N_SHIFT + D_A + 3 * D_B + 2 * D_MODEL
EPS = 1e-6
GN_EPS = 64e-5
LN_EPS = 1e-5

O_GA = N_SHIFT
O_U = O_GA + D_A
O_VB = O_U + D_B
O_GB = O_VB + D_B
O_MA = O_GB + D_B
O_MB = O_MA + D_MODEL

WKV_CHUNK = 64
HEADS_PER_GROUP = 2
GROUP_W = HEADS_PER_GROUP * HEAD
N_HGROUPS = N_HEADS // HEADS_PER_GROUP
HS_W = 256
FRONT_SEQS = 8
MERGE_ROWS = 1024
V7X_VMEM_BYTES = 64 * 1024 * 1024
VMEM_LIMIT_BYTES = V7X_VMEM_BYTES - 8 * 1024 * 1024

F32 = jnp.float32
BF16 = jnp.bfloat16
GATE_DTYPE = BF16

NN = ((1,), (0,))
TN = ((0,), (0,))


def _dg(a, b, dims):
    return lax.dot_general(a, b, (dims, ((), ())), preferred_element_type=F32)


def _split(x):
    hi = x.astype(BF16)
    lo = (x - hi.astype(F32)).astype(BF16)
    return hi, lo


def _sigmoid(x):
    return 0.5 + 0.5 * jnp.tanh(0.5 * x)


def _gelu_tanh(x):
    return 0.5 * x * (1.0 + jnp.tanh(0.7978845608028654 * (x + 0.044715 * (x * x * x))))


def _softplus(x):
    return jnp.maximum(x, 0.0) + jnp.log(1.0 + jnp.exp(-jnp.abs(x)))


def _head_ones(n):
    r = lax.broadcasted_iota(jnp.int32, (n, n), 0) // HEAD
    c = lax.broadcasted_iota(jnp.int32, (n, n), 1) // HEAD
    return jnp.where(r == c, 1.0, 0.0).astype(BF16)


def _headsums(xs, ones_g):
    rows = xs[0].shape[0]
    pieces = []
    for x in xs:
        hi, lo = _split(x)
        pieces += [hi[:, :HS_W], lo[:, :HS_W], hi[:, HS_W:], lo[:, HS_W:]]
    o = _dg(jnp.concatenate(pieces, axis=0), ones_g, NN)
    outs = []
    for i in range(len(xs)):
        q = [o[(4 * i + j) * rows:(4 * i + j + 1) * rows] for j in range(4)]
        outs.append(jnp.concatenate([q[0] + q[1], q[2] + q[3]], axis=1))
    return outs


def _proj_kernel(*refs, has_vres):
    (x_ref, g_ref, w_ref, lng_ref, lnb_ref, sp_a_ref, sp_b_ref, wbb_ref) = refs[:8]
    rest = refs[8:]
    if has_vres:
        vd_ref, vu_ref, vb_ref = rest[:3]
        rest = rest[3:]
    z_ref, ga_ref, ma_ref, pb_ref, vn_ref = rest[:5]
    rest = rest[5:]
    x = x_ref[...]
    xn = x * lax.rsqrt(jnp.mean(x * x, axis=-1, keepdims=True) + EPS) * g_ref[...]
    xb = xn.astype(BF16)

    def seg(lo, hi):
        return jnp.dot(xb, w_ref[:, lo:hi], preferred_element_type=F32)

    z_ref[...] = seg(0, N_SHIFT)
    t = seg(O_GA, O_U)
    ga_ref[...] = (t * _sigmoid(t)).astype(ga_ref.dtype)
    ma_ref[...] = _sigmoid(seg(O_MA, O_MB)).astype(ma_ref.dtype)
    u = _gelu_tanh(seg(O_U, O_VB))
    vb = _gelu_tanh(seg(O_VB, O_GB))
    mu = jnp.mean(vb, axis=-1, keepdims=True)
    d = vb - mu
    var = jnp.mean(d * d, axis=-1, keepdims=True)
    vn = d * lax.rsqrt(var + LN_EPS) * lng_ref[...] + lnb_ref[...]
    vn_ref[...] = vn
    t = seg(O_GB, O_MA)
    mixed = vn * sp_a_ref[...] + sp_b_ref[...]
    yb = u * mixed * (t * _sigmoid(t))
    br_b = jnp.dot(yb.astype(BF16), wbb_ref[...], preferred_element_type=F32)
    pb_ref[...] = (_sigmoid(seg(O_MB, D_IN)) * br_b).astype(pb_ref.dtype)
    if has_vres:
        (vg_ref,) = rest
        low = jnp.dot(xb, vd_ref[...], preferred_element_type=F32)
        up = jnp.dot(low.astype(BF16), vu_ref[...], preferred_element_type=F32)
        vg_ref[...] = _sigmoid(vb_ref[...] + up)


def _proj_call(x2d, w_in_all, layer, p, tm):
    n = x2d.shape[0]
    vres = p["vres"]
    has_vres = vres is not None

    def full(a):
        return pl.BlockSpec(a.shape, lambda i: (0,) * a.ndim, pipeline_mode=pl.Buffered(1))

    def rows(w):
        return pl.BlockSpec((tm, w), lambda i: (i, 0))

    w_spec = pl.BlockSpec((None, D_MODEL, D_IN), lambda i: (layer, 0, 0), pipeline_mode=pl.Buffered(1))
    consts = [p["ln_g"], p["ln_b"], p["sp_w0"], p["sp_b0"]]
    tail = list(vres) if has_vres else []
    wbb_spec = pl.BlockSpec((None,) + p["w_br_b"].shape[1:], lambda i: (layer, 0, 0), pipeline_mode=pl.Buffered(1))
    widths = [N_SHIFT, D_A, D_MODEL, D_MODEL, D_B] + ([D_A] if has_vres else [])
    return pl.pallas_call(
        functools.partial(_proj_kernel, has_vres=has_vres),
        grid=(n // tm,),
        in_specs=([rows(D_MODEL), full(p["norm_g"]), w_spec] + [full(a) for a in consts] + [wbb_spec]
                  + [full(a) for a in tail]),
        out_specs=[rows(w) for w in widths],
        out_shape=[jax.ShapeDtypeStruct((n, w), GATE_DTYPE if 1 <= j <= 3 else F32) for j, w in enumerate(widths)],
        compiler_params=pltpu.CompilerParams(
            dimension_semantics=("arbitrary",), vmem_limit_bytes=VMEM_LIMIT_BYTES),
        name="proj_s" + ("_vres" if has_vres else ""),
    )(x2d, p["norm_g"], w_in_all, *consts, p["w_br_b"], *tail)


def _timemix_pre(z, z_prev, mu, w0, wup_pad, a0, aup_pad, k_k, k_a, r_k, vres, ones_g, fill=lambda: None):
    zs = z + (z_prev - z) * mu
    fill()
    r = zs[:, 0:D_A]
    k = zs[:, D_A:2 * D_A]
    v = zs[:, 2 * D_A:3 * D_A]
    lora_in = zs[:, 3 * D_A:N_SHIFT]
    wq = w0 + jnp.dot(jnp.tanh(lora_in).astype(BF16), wup_pad, preferred_element_type=F32)
    w = -_softplus(-wq) - 0.5
    lw = -jnp.exp(w)
    fill()
    a = _sigmoid(a0 + jnp.dot(lora_in.astype(BF16), aup_pad, preferred_element_type=F32))
    fill()
    if vres is not None:
        vgate, vfirst = vres
        v = v + (vfirst - v) * vgate
    kk = k * k_k
    k2 = k * (1.0 + (a - 1.0) * k_a)
    fill()
    ss, rk_sum = _headsums([kk * kk, r * k2 * r_k], ones_g)
    fill()
    kk = kk / jnp.maximum(jnp.sqrt(ss), 1e-12)
    return r, k2, v, -kk, kk * a, lw, rk_sum * v


def _chunk_groups(chains, masks, fillers):
    t_idx, s_idx, bdmask = masks
    fillers = list(fillers)

    def fill():
        if fillers:
            fillers.pop(0)()

    C = chains[0][0].shape[0]
    n = range(len(chains))
    At, Rt, Bt, Kt, Bh, Kh, V, S, wcc = (list(col) for col in zip(*chains))
    strict = s_idx < t_idx
    incl = s_idx <= t_idx

    def b16(x):
        return x.astype(BF16)

    def bd(x):
        return jnp.where(bdmask, jnp.concatenate([x] * HEADS_PER_GROUP, axis=0), 0)

    def bd_t(x):
        return jnp.where(bdmask, jnp.concatenate([x] * HEADS_PER_GROUP, axis=0).T, 0.0).astype(BF16)

    ar = [b16(jnp.concatenate([At[i], Rt[i]], axis=0)) for i in n]
    sc_b = [_dg(ar[i], bd_t(Bt[i]), NN) for i in n]
    fill()
    sc_k = [_dg(ar[i], bd_t(Kt[i]), NN) for i in n]
    fill()
    N = [jnp.where(strict, sc_b[i][:C], 0.0) for i in n]
    Srb = [jnp.where(incl, sc_b[i][C:], 0.0) for i in n]
    M = [jnp.where(strict, sc_k[i][:C], 0.0) for i in n]
    Srk = [jnp.where(incl, sc_k[i][C:], 0.0) for i in n]
    ar_s = [_dg(ar[i], b16(S[i].T), NN) for i in n]
    fill()
    mv = [_dg(b16(jnp.concatenate([M[i], Srk[i]], axis=0)), bd(b16(V[i])), NN) for i in n]
    X = [ar_s[i][:C] + mv[i][:C] for i in n]

    def level_mask(b):
        return ((t_idx // (2 * b)) == (s_idx // (2 * b))) & ((t_idx % (2 * b)) >= b) & ((s_idx % (2 * b)) < b)

    T = [jnp.where(s_idx == t_idx, 1.0, 0.0) + jnp.where(level_mask(1), N[i], 0.0) for i in n]
    b = 2
    while b < C:
        lm = level_mask(b)
        d = [b16(T[i]) for i in n]
        E = [_dg(b16(jnp.where(lm, N[i], 0.0)), bd(d[i]), NN) for i in n]
        fill()
        F = [_dg(d[i], bd(b16(E[i])), NN) for i in n]
        fill()
        T = [T[i] + F[i] for i in n]
        b *= 2

    U = [_dg(b16(T[i]), bd(b16(X[i])), NN) for i in n]
    fill()
    Y = [ar_s[i][C:] + mv[i][C:] + _dg(b16(Srb[i]), bd(b16(U[i])), NN) for i in n]
    upd = [_dg(b16(jnp.concatenate([U[i], V[i]], axis=0)),
               b16(jnp.concatenate([Bh[i], Kh[i]], axis=0)), TN) for i in n]
    while fillers:
        fill()
    return [(Y[i], S[i] * wcc[i] + jnp.where(bdmask, upd[i], 0.0)) for i in n]


def _front_kernel(*refs, bb, has_vres):
    (x_ref, g_ref, w_ref, lng_ref, lnb_ref, sp_a_ref, sp_b_ref, wbb_ref) = refs[:8]
    rest = refs[8:]
    if has_vres:
        vd_ref, vu_ref, vb_ref, vf_ref = rest[:4]
        rest = rest[4:]
    (s0_ref, sh0_ref, mu_ref, w0_ref, wup_ref, a0_ref, aup_ref, kk_ref, ka_ref, rk_ref, gng_ref, gnb_ref,
     ga_ref, ma_ref, pb_ref, y_ref, v_ref, zl_ref, s_ref, zp_scr, vn_scr) = rest
    C = x_ref.shape[1]
    rows = bb * C
    c = pl.program_id(1)

    @pl.when(c == 0)
    def _():
        s_ref[...] = s0_ref[...]
        zp_scr[...] = sh0_ref[...]
        vn_scr[...] = jnp.zeros(vn_scr.shape, F32)

    x = x_ref[...].reshape(rows, D_MODEL)
    xn = x * lax.rsqrt(jnp.mean(x * x, axis=-1, keepdims=True) + EPS) * g_ref[...]
    xb = xn.astype(BF16)

    def seg(lo, hi):
        return jnp.dot(xb, w_ref[:, lo:hi], preferred_element_type=F32)

    z_all = seg(0, N_SHIFT)
    if has_vres:
        low = jnp.dot(xb, vd_ref[...], preferred_element_type=F32)
        vgate = _sigmoid(vb_ref[...] + jnp.dot(low.astype(BF16), vu_ref[...], preferred_element_type=F32))

    ones_g = _head_ones(HS_W)
    t_idx = lax.broadcasted_iota(jnp.int32, (C, GROUP_W), 0)
    s_idx = lax.broadcasted_iota(jnp.int32, (C, GROUP_W), 1) % C
    rr = lax.broadcasted_iota(jnp.int32, (GROUP_W, GROUP_W), 0) // HEAD
    cc = lax.broadcasted_iota(jnp.int32, (GROUP_W, GROUP_W), 1) // HEAD
    masks = (t_idx, s_idx, rr == cc)
    tri = jnp.where(lax.broadcasted_iota(jnp.int32, (C, C), 1)
                    <= lax.broadcasted_iota(jnp.int32, (C, C), 0), 1.0, 0.0).astype(BF16)
    row0 = lax.broadcasted_iota(jnp.int32, (bb, C, N_SHIFT), 1) == 0

    half = c % 2
    box = {}

    def f_ga():
        t = seg(O_GA, O_U)
        ga_ref[...] = (t * _sigmoid(t)).astype(ga_ref.dtype).reshape(bb, C, D_A)

    def f_ma(lo, hi):
        def f():
            ma_ref[:, :, lo:hi] = _sigmoid(seg(O_MA + lo, O_MA + hi)).astype(ma_ref.dtype).reshape(bb, C, hi - lo)
        return f

    def f_u():
        box["u"] = _gelu_tanh(seg(O_U, O_VB))

    def f_vn():
        vb = _gelu_tanh(seg(O_VB, O_GB))
        d = vb - jnp.mean(vb, axis=-1, keepdims=True)
        var = jnp.mean(d * d, axis=-1, keepdims=True)
        box["vn"] = d * lax.rsqrt(var + LN_EPS) * lng_ref[...] + lnb_ref[...]

    def f_mix():
        vn = box["vn"]
        srow = lax.broadcasted_iota(jnp.int32, (C, SPATIAL_CHUNK), 0) + half * C
        scol = lax.broadcasted_iota(jnp.int32, (C, SPATIAL_CHUNK), 1)
        keep = scol <= srow
        wm = [jnp.where(keep, sp_a_ref[g, pl.ds(half * C, C), :], 0.0).astype(BF16) for g in range(N_GROUPS_B)]
        bias = sp_b_ref[pl.ds(half * C, C), :]
        outs = []
        for bi in range(bb):
            cur = vn[bi * C:(bi + 1) * C]
            first = jnp.where(half == 0, cur, vn_scr[bi])
            both = jnp.concatenate([first, cur], axis=0).astype(BF16)
            vn_scr[bi] = cur
            cols = [jnp.dot(wm[g], both[:, g * GROUP_B:(g + 1) * GROUP_B], preferred_element_type=F32)
                    + bias[:, g:g + 1] for g in range(N_GROUPS_B)]
            outs.append(jnp.concatenate(cols, axis=1))
        box["mixed"] = jnp.concatenate(outs, axis=0)

    def f_yb():
        t = seg(O_GB, O_MA)
        box["yb"] = (box["u"] * box["mixed"] * (t * _sigmoid(t))).astype(BF16)

    def f_pb(lo, hi):
        def f():
            br_b = jnp.dot(box["yb"], wbb_ref[:, lo:hi], preferred_element_type=F32)
            pb_ref[:, :, lo:hi] = (_sigmoid(seg(O_MB + lo, O_MB + hi)) * br_b).astype(pb_ref.dtype).reshape(bb, C, hi - lo)
        return f

    hm = D_MODEL // 2
    fillers = [f_ga, f_ma(0, hm), f_ma(hm, D_MODEL), f_u, f_vn, f_mix, f_yb, f_pb(0, hm), f_pb(hm, D_MODEL)]

    def fill():
        if fillers:
            fillers.pop(0)()

    z3 = z_all.reshape(bb, C, N_SHIFT)
    z_prev = jnp.where(row0, zp_scr[...], pltpu.roll(z_all, 1, 0).reshape(bb, C, N_SHIFT))
    zp_scr[...] = z3[:, C - 1:C, :]
    zl_ref[...] = z3[:, C - 1:C, :]
    vres = (vgate, vf_ref[...].reshape(rows, D_A)) if has_vres else None
    r, k2, v, aa, bbv, lw, bonus = _timemix_pre(
        z_all, z_prev.reshape(rows, N_SHIFT), mu_ref[...], w0_ref[...], wup_ref[...], a0_ref[...], aup_ref[...],
        kk_ref[...], ka_ref[...], rk_ref[...], vres, ones_g, fill)
    v_ref[...] = v.reshape(bb, C, D_A)

    p1 = lw.astype(BF16)
    r1 = lw - p1.astype(F32)
    p2 = r1.astype(BF16)
    p3 = (r1 - p2.astype(F32)).astype(BF16)
    fill()
    cum = jnp.concatenate(
        [_dg(tri, p1[bi * C:(bi + 1) * C], NN)
         + (_dg(tri, p2[bi * C:(bi + 1) * C], NN) + _dg(tri, p3[bi * C:(bi + 1) * C], NN))
         for bi in range(bb)], axis=0)
    fill()
    cum3 = cum.reshape(bb, C, D_A)
    cum_c = cum3[:, C - 1:C, :]
    wc = jnp.exp(cum)
    wi = jnp.exp(-cum)
    wrel = jnp.exp(cum_c - cum3).reshape(rows, D_A)
    wcc = jnp.exp(cum_c)
    fill()
    At = aa * jnp.exp(cum - lw)
    Rt = r * wc
    Bt = bbv * wi
    Kt = k2 * wi
    Bh = bbv * wrel
    Kh = k2 * wrel
    fill()

    chains = []
    for bi in range(bb):
        rs = slice(bi * C, (bi + 1) * C)
        for g in range(N_HGROUPS):
            sl = slice(g * GROUP_W, (g + 1) * GROUP_W)
            chains.append((At[rs, sl], Rt[rs, sl], Bt[rs, sl], Kt[rs, sl], Bh[rs, sl], Kh[rs, sl],
                           v[rs, sl], s_ref[bi, g], wcc[bi][:, sl]))

    res = _chunk_groups(chains, masks, fillers)

    for bi in range(bb):
        for g in range(N_HGROUPS):
            s_ref[bi, g] = res[bi * N_HGROUPS + g][1]
    y = jnp.concatenate(
        [jnp.concatenate([res[bi * N_HGROUPS + g][0] for g in range(N_HGROUPS)], axis=1) for bi in range(bb)],
        axis=0)
    d = y - _headsums([y], ones_g)[0] * (1.0 / HEAD)
    var = _headsums([d * d], ones_g)[0] * (1.0 / HEAD)
    y_ref[...] = (d * lax.rsqrt(var + GN_EPS) * gng_ref[...] + gnb_ref[...] + bonus).reshape(bb, C, D_A)


def _front_call(x3, w_in_all, layer, p, v_first, s0_bd, shift0, bb):
    B, T, _ = x3.shape
    C = WKV_CHUNK
    vres = p["vres"]
    has_vres = vres is not None

    def tok(w):
        return pl.BlockSpec((bb, C, w), lambda b, c: (b, c, 0))

    def full(a):
        return pl.BlockSpec(a.shape, lambda b, c: (0,) * a.ndim, pipeline_mode=pl.Buffered(1))

    w_spec = pl.BlockSpec((None, D_MODEL, D_IN), lambda b, c: (layer, 0, 0), pipeline_mode=pl.Buffered(1))
    s_spec = pl.BlockSpec((bb, N_HGROUPS, GROUP_W, GROUP_W), lambda b, c: (b, 0, 0, 0))
    row_spec = pl.BlockSpec((bb, 1, N_SHIFT), lambda b, c: (b, 0, 0))
    consts = [p["ln_g"], p["ln_b"], p["sp_w"], p["sp_bT"]]
    wbb_spec = pl.BlockSpec((None,) + p["w_br_b"].shape[1:], lambda b, c: (layer, 0, 0),
                            pipeline_mode=pl.Buffered(1))
    in_specs = [tok(D_MODEL), full(p["norm_g"]), w_spec] + [full(a) for a in consts] + [wbb_spec]
    args = [x3, p["norm_g"], w_in_all] + consts + [p["w_br_b"]]
    if has_vres:
        in_specs += [full(a) for a in vres] + [tok(D_A)]
        args += list(vres) + [v_first]
    in_specs += [s_spec, row_spec]
    args += [s0_bd, shift0]
    for name in ("mu", "w0", "wup", "a0", "aup", "k_k", "k_a", "r_k", "gn_g", "gn_b"):
        in_specs.append(full(p[name]))
        args.append(p[name])
    widths = [D_A, D_MODEL, D_MODEL, D_A, D_A]
    return pl.pallas_call(
        functools.partial(_front_kernel, bb=bb, has_vres=has_vres),
        grid=(B // bb, T // C),
        in_specs=in_specs,
        out_specs=[tok(w) for w in widths] + [row_spec, s_spec],
        out_shape=[jax.ShapeDtypeStruct((B, T, w), GATE_DTYPE if j < 3 else F32) for j, w in enumerate(widths)]
        + [jax.ShapeDtypeStruct((B, 1, N_SHIFT), F32), jax.ShapeDtypeStruct(s0_bd.shape, F32)],
        scratch_shapes=[pltpu.VMEM((bb, 1, N_SHIFT), F32), pltpu.VMEM((bb, C, D_B), F32)],
        compiler_params=pltpu.CompilerParams(
            dimension_semantics=("arbitrary", "arbitrary"), vmem_limit_bytes=VMEM_LIMIT_BYTES),
        name="front_vres" if has_vres else "front",
    )(*args)


def _wkv_pre_kernel(*refs, has_vres):
    if has_vres:
        (z_ref, zp_ref, vg_ref, vf_ref, mu_ref, w0_ref, wup_ref, a0_ref, aup_ref, kk_ref, ka_ref, rk_ref,
         r_o, w_o, k_o, vt_o, a_o, b_o, bonus_o, v_o) = refs
        vres = (vg_ref[...], vf_ref[...])
    else:
        (z_ref, zp_ref, mu_ref, w0_ref, wup_ref, a0_ref, aup_ref, kk_ref, ka_ref, rk_ref,
         r_o, w_o, k_o, vt_o, a_o, b_o, bonus_o, v_o) = refs
        vres = None
    r, k2, v, aa, bbv, lw, bonus = _timemix_pre(
        z_ref[...], zp_ref[...], mu_ref[...], w0_ref[...], wup_ref[...], a0_ref[...], aup_ref[...],
        kk_ref[...], ka_ref[...], rk_ref[...], vres, _head_ones(HS_W))
    r_o[...] = r.T
    w_o[...] = jnp.exp(lw).T
    k_o[...] = k2.T
    vt_o[...] = v.T
    a_o[...] = aa.T
    b_o[...] = bbv.T
    bonus_o[...] = bonus.T
    v_o[...] = v


def _wkv_pre_call(z, z_prev, vres, p):
    n = z.shape[0]
    has_vres = vres is not None
    args = [z, z_prev] + (list(vres) if has_vres else [])
    args += [p[k] for k in ("mu", "w0", "wup", "a0", "aup", "k_k", "k_a", "r_k")]
    return pl.pallas_call(
        functools.partial(_wkv_pre_kernel, has_vres=has_vres),
        out_shape=[jax.ShapeDtypeStruct((D_A, n), F32)] * 7 + [jax.ShapeDtypeStruct((n, D_A), F32)],
        compiler_params=pltpu.CompilerParams(vmem_limit_bytes=VMEM_LIMIT_BYTES),
        name="wkv_pre_vres" if has_vres else "wkv_pre",
    )(*args)


def _wkv_step_kernel(s_ref, r_ref, w_ref, k_ref, v_ref, a_ref, b_ref, bonus_ref, gng_ref, gnb_ref,
                     y_ref, so_ref, y_scr):
    aT = a_ref[...]
    wT = w_ref[...]
    bT = b_ref[...]
    kT = k_ref[...]
    rT = r_ref[...]

    def body(i, carry):
        si = s_ref[i]
        sa = jnp.sum(si * aT, axis=0, keepdims=True)
        s2 = si * wT + sa * bT + v_ref[pl.ds(i, 1), :] * kT
        so_ref[i] = s2
        y_scr[pl.ds(i, 1), :] = jnp.sum(s2 * rT, axis=0, keepdims=True)
        return carry

    lax.fori_loop(0, HEAD, body, 0, unroll=4)
    y = y_scr[...]
    d = y - jnp.mean(y, axis=0, keepdims=True)
    var = jnp.mean(d * d, axis=0, keepdims=True)
    y_ref[...] = d * lax.rsqrt(var + GN_EPS) * gng_ref[...] + gnb_ref[...] + bonus_ref[...]


def _wkv_step_call(s_all, layer, vecs, gng_col, gnb_col):
    n = s_all.shape[-1]
    vec_spec = pl.BlockSpec((HEAD, n), lambda h: (h, 0))
    col_spec = pl.BlockSpec((HEAD, 1), lambda h: (h, 0))
    s_in = pl.BlockSpec((None, None, HEAD, HEAD, n), lambda h: (layer, h, 0, 0, 0))
    s_out = pl.BlockSpec((None, HEAD, HEAD, n), lambda h: (h, 0, 0, 0))
    return pl.pallas_call(
        _wkv_step_kernel,
        grid=(N_HEADS,),
        in_specs=[s_in] + [vec_spec] * 7 + [col_spec, col_spec],
        out_specs=[vec_spec, s_out],
        out_shape=[jax.ShapeDtypeStruct((D_A, n), F32), jax.ShapeDtypeStruct(s_all.shape[1:], F32)],
        scratch_shapes=[pltpu.VMEM((HEAD, n), F32)],
        compiler_params=pltpu.CompilerParams(
            dimension_semantics=("arbitrary",), vmem_limit_bytes=VMEM_LIMIT_BYTES),
        name="wkv_step",
    )(s_all, *vecs, gng_col, gnb_col)


def _merge_kernel(*refs, final):
    (h_ref, ya_ref, ga_ref, ma_ref, pb_ref, p_ref, wba_ref, wout_ref, wple_ref, wpg_ref, bpg_ref) = refs[:11]
    rest = refs[11:]
    if final:
        fg_ref, out_ref = rest
    else:
        (out_ref,) = rest
    br_a = jnp.dot((ya_ref[...] * ga_ref[...].astype(F32)).astype(BF16), wba_ref[...],
                   preferred_element_type=F32)
    merged = ma_ref[...].astype(F32) * br_a + pb_ref[...].astype(F32)
    h = h_ref[...] + jnp.dot(merged.astype(BF16), wout_ref[...], preferred_element_type=F32)
    ple = jnp.dot(p_ref[...].astype(BF16), wple_ref[...], preferred_element_type=F32)
    gate = _sigmoid(jnp.dot(h.astype(BF16), wpg_ref[...], preferred_element_type=F32) + bpg_ref[...])
    h = h + gate * ple
    if final:
        out_ref[...] = h * lax.rsqrt(jnp.mean(h * h, axis=-1, keepdims=True) + EPS) * fg_ref[...]
    else:
        out_ref[...] = h


def _merge_call(h, ya, ga, ma, pb, pe_all, layer, wl, final_g, tm):
    n = h.shape[0]
    final = final_g is not None

    def rows(w):
        return pl.BlockSpec((tm, w), lambda i: (i, 0))

    def full(a):
        return pl.BlockSpec(a.shape, lambda i: (0,) * a.ndim, pipeline_mode=pl.Buffered(1))

    row_args = [h, ya, ga, ma, pb]
    pe_spec = pl.BlockSpec((tm, D_PLE), lambda i: (layer * (n // tm) + i, 0))
    def stacked(a):
        return pl.BlockSpec((None,) + a.shape[1:], lambda i: (layer, 0, 0), pipeline_mode=pl.Buffered(1))

    w_stacked = [wl["w_br_a"], wl["w_out"], wl["w_ple"], wl["w_pg"]]
    w_args = [wl["b_pg"]] + ([final_g] if final else [])
    return pl.pallas_call(
        functools.partial(_merge_kernel, final=final),
        grid=(n // tm,),
        in_specs=([rows(a.shape[1]) for a in row_args] + [pe_spec] + [stacked(a) for a in w_stacked]
                  + [full(a) for a in w_args]),
        out_specs=rows(D_MODEL),
        out_shape=jax.ShapeDtypeStruct((n, D_MODEL), F32),
        compiler_params=pltpu.CompilerParams(
            dimension_semantics=("arbitrary",), vmem_limit_bytes=VMEM_LIMIT_BYTES),
        name="merge" + ("_f" if final else ""),
    )(*row_args, pe_all, *w_stacked, *w_args)


def _layer_params(W, Wb, l):
    row = lambda a: a.reshape(1, -1)
    zpad = jnp.zeros((D_LORA, D_A), F32)
    p = dict(
        norm_g=row(W["norm_g"][l]),
        ln_g=row(W["ln_v_g"][l]), ln_b=row(W["ln_v_b"][l]),
        mu=row(W["shift_mu"][l]), w0=row(W["w0"][l]), a0=row(W["a0"][l]),
        wup=jnp.concatenate([W["w_up"][l], zpad], axis=0).astype(BF16),
        aup=jnp.concatenate([zpad, W["a_up"][l]], axis=0).astype(BF16),
        k_k=row(W["k_k"][l]), k_a=row(W["k_a"][l]), r_k=row(W["r_k"][l]),
        gn_g=row(W["gn_g"][l]), gn_b=row(W["gn_b"][l]),
        w_br_a=Wb["w_br_a"], w_br_b=Wb["w_br_b"], w_out=Wb["w_out"], w_ple=Wb["w_ple"],
        w_pg=Wb["w_ple_gate"], b_pg=row(W["b_ple_gate"][l]),
        sp_w=W["w_spatial"][l], sp_bT=W["b_spatial"][l].T,
        sp_w0=jnp.repeat(W["w_spatial"][l][:, 0, 0], GROUP_B).reshape(1, D_B),
        sp_b0=jnp.repeat(W["b_spatial"][l][:, 0], GROUP_B).reshape(1, D_B),
    )
    if l > 0:
        p["vres"] = (W["vres_down"][l - 1].astype(BF16), W["vres_up"][l - 1].astype(BF16),
                     row(W["vres_b"][l - 1]))
    else:
        p["vres"] = None
    return p


def _diag_blocks(s_bd):
    B = s_bd.shape[0]
    blocks = [s_bd[:, :, h * HEAD:(h + 1) * HEAD, h * HEAD:(h + 1) * HEAD] for h in range(HEADS_PER_GROUP)]
    return jnp.stack(blocks, axis=2).reshape(B, N_HEADS, HEAD, HEAD)


def _forward_prompt(x, pe, W, params, w_in_all):
    B, T, _ = x.shape
    n = B * T
    assert B % FRONT_SEQS == 0 and T % SPATIAL_CHUNK == 0 and n % MERGE_ROWS == 0, (B, T)
    h = x.reshape(n, D_MODEL)
    pe_all = pe.reshape(-1, D_PLE)
    v_first = None
    wkvs, shifts = [], []
    depth = len(params)
    s0 = jnp.zeros((B, N_HGROUPS, GROUP_W, GROUP_W), F32)
    shift0 = jnp.zeros((B, 1, N_SHIFT), F32)
    for l, p in enumerate(params):
        ga, ma, pb, ya, v, z_last, s_bd = _front_call(
            h.reshape(B, T, D_MODEL), w_in_all, l, p, v_first, s0, shift0, bb=FRONT_SEQS)
        if l == 0:
            v_first = v
        wkvs.append(_diag_blocks(s_bd))
        shifts.append(z_last.reshape(B, N_SHIFT))
        fg = W["final_g"].reshape(1, -1) if l == depth - 1 else None
        h = _merge_call(h, ya.reshape(n, D_A), ga.reshape(n, D_A), ma.reshape(n, D_MODEL),
                        pb.reshape(n, D_MODEL), pe_all, l, p, fg, tm=MERGE_ROWS)
    y = h.reshape(B, T, D_MODEL)
    return y, jnp.stack(wkvs), jnp.stack(shifts)


def _forward_sample(x, pe, wkv0, shift0, W, params, w_in_all):
    B = x.shape[0]
    h = x.reshape(B, D_MODEL)
    pe_all = pe.reshape(-1, D_PLE)
    s_all = jnp.transpose(wkv0, (0, 2, 3, 4, 1))
    v_first = None
    wkvs, shifts, chunk_vs = [], [], []
    depth = len(params)
    for l, p in enumerate(params):
        outs = _proj_call(h, w_in_all, l, p, tm=B)
        z, ga, ma, pb, vn = outs[:5]
        vres = None if l == 0 else (outs[5], v_first)
        pre = _wkv_pre_call(z, shift0[l], vres, p)
        if l == 0:
            v_first = pre[7]
        ya_t, s_new = _wkv_step_call(s_all, l, pre[:7], p["gn_g"].reshape(D_A, 1), p["gn_b"].reshape(D_A, 1))
        wkvs.append(s_new)
        shifts.append(z)
        chunk_vs.append(vn.reshape(B, 1, D_B))
        fg = W["final_g"].reshape(1, -1) if l == depth - 1 else None
        h = _merge_call(h, ya_t.T, ga, ma, pb, pe_all, l, p, fg, tm=B)
    y = h.reshape(B, 1, D_MODEL)
    wkv_new = jnp.transpose(jnp.stack(wkvs), (0, 4, 1, 2, 3))
    return y, wkv_new, jnp.stack(shifts), jnp.stack(chunk_vs)


def kernel(x_prompt, x_sample, state_rwkv_wkv, state_rwkv_shift, p_prompt, p_sample, norm_g, w_in, shift_mu, w0, w_up, a0, a_up, vres_down, vres_up, vres_b, k_k, k_a, r_k, gn_g, gn_b, ln_v_g, ln_v_b, w_spatial, b_spatial, w_br_a, w_br_b, w_out, w_ple, w_ple_gate, b_ple_gate, final_g):
    W = dict(norm_g=norm_g, w_in=w_in, shift_mu=shift_mu, w0=w0, w_up=w_up, a0=a0, a_up=a_up,
             vres_down=vres_down, vres_up=vres_up, vres_b=vres_b, k_k=k_k, k_a=k_a, r_k=r_k,
             gn_g=gn_g, gn_b=gn_b, ln_v_g=ln_v_g, ln_v_b=ln_v_b, w_spatial=w_spatial,
             b_spatial=b_spatial, w_br_a=w_br_a, w_br_b=w_br_b, w_out=w_out, w_ple=w_ple,
             w_ple_gate=w_ple_gate, b_ple_gate=b_ple_gate, final_g=final_g)
    Wb = {k: W[k].astype(BF16) for k in ("w_br_a", "w_br_b", "w_out", "w_ple", "w_ple_gate")}
    params = [_layer_params(W, Wb, l) for l in range(w_in.shape[0])]
    w_in_all = w_in.astype(BF16)
    y_p, wkv_p, shift_p = _forward_prompt(x_prompt, p_prompt, W, params, w_in_all)
    y_s, wkv_s, shift_s, chunk_v = _forward_sample(
        x_sample, p_sample, state_rwkv_wkv, state_rwkv_shift, W, params, w_in_all)
    return (y_p, y_s, wkv_p, shift_p, wkv_s, shift_s, chunk_v)
```
